```python
import math
import jax, jax.numpy as jnp
from jax import lax
import numpy as np

D_MODEL = 2048
BATCH = 2
SEQ = 4096
DEPTH = 1
DEC_BATCH = 32
DEC_SEQ = 8
PAST_LEN = 8192
PAGE_SIZE = 128

N_HEADS = 16
N_KV_HEADS = 4
HEAD_DIM = 128
KV_GROUP = N_HEADS // N_KV_HEADS
ATTN_WIDTH = N_HEADS * HEAD_DIM
KV_WIDTH = N_KV_HEADS * HEAD_DIM
ROPE_THETA = 10000.0
MOBA_BLOCK = 256
MOBA_TOPK = 3
MOBA_NSEL = MOBA_TOPK + 1
QUERY_ROWS = 128
D_RNN = D_MODEL
RNN_BLOCKS = 16
RNN_BLOCK_W = D_RNN // RNN_BLOCKS
CONV_W = 4
LRU_C = 8.0
D_FF = 4 * D_MODEL
RMS_EPS = 1e-6
NEG_INF = -1e30
IN_WIDTH = ATTN_WIDTH + 2 * KV_WIDTH + 2 * D_RNN + 2 * D_MODEL

kernel_name = 'griffin_moba_hybrid_step'


def rms_norm(x, g):
    xf = x.astype(jnp.float32)
    y = xf * lax.rsqrt(jnp.mean(xf * xf, axis=-1, keepdims=True) + RMS_EPS)
    return (y * g.astype(jnp.float32)).astype(x.dtype)


def rope(x, pos):
    half = HEAD_DIM // 2
    inv = ROPE_THETA ** (-jnp.arange(half, dtype=jnp.float32) * (2.0 / HEAD_DIM))
    ang = pos.astype(jnp.float32)[:, None] * inv[None, :]
    cos = jnp.cos(ang)[None, :, None, :]
    sin = jnp.sin(ang)[None, :, None, :]
    xf = x.astype(jnp.float32)
    x1, x2 = xf[..., :half], xf[..., half:]
    return jnp.concatenate([x1 * cos - x2 * sin, x2 * cos + x1 * sin], axis=-1).astype(x.dtype)


def _lin_combine(left, right):
    a1, b1 = left
    a2, b2 = right
    return a1 * a2, a2 * b1 + b2


def rg_lru_branch(xr, yg, conv_buf, h0, conv_w, conv_b, w_rg_a, b_rg_a, w_rg_x, b_rg_x, lru_lambda):
    B, S, C = xr.shape
    xp = jnp.concatenate([conv_buf.astype(xr.dtype), xr], axis=1)
    xc = conv_b
    for j in range(CONV_W):
        xc = xc + xp[:, j:j + S] * conv_w[j]
    new_buf = xp[:, S:]
    xb = xc.reshape(B, S, RNN_BLOCKS, RNN_BLOCK_W)
    r = jax.nn.sigmoid((jnp.einsum('bsnc,ncd->bsnd', xb, w_rg_a).reshape(B, S, C) + b_rg_a).astype(jnp.float32))
    i = jax.nn.sigmoid((jnp.einsum('bsnc,ncd->bsnd', xb, w_rg_x).reshape(B, S, C) + b_rg_x).astype(jnp.float32))
    log_a = -LRU_C * r * jax.nn.softplus(-lru_lambda.astype(jnp.float32))
    a = jnp.exp(log_a)
    b = jnp.sqrt(-jnp.expm1(2.0 * log_a)) * (i * xc.astype(jnp.float32))
    a_cum, h = lax.associative_scan(_lin_combine, (a, b), axis=1)
    h = h + a_cum * h0.astype(jnp.float32)[:, None, :]
    out = h.astype(xr.dtype) * jax.nn.gelu(yg)
    return out, h[:, -1].astype(h0.dtype), new_buf


def moba_attention(q, k, v, q_pos):
    B, Sq = q.shape[0], q.shape[1]
    Sk = k.shape[1]
    nb = -(-Sk // MOBA_BLOCK)
    pad = nb * MOBA_BLOCK - Sk
    kb = jnp.pad(k, ((0, 0), (0, pad), (0, 0), (0, 0))).reshape(B, nb, MOBA_BLOCK, N_KV_HEADS, HEAD_DIM)
    vb = jnp.pad(v, ((0, 0), (0, pad), (0, 0), (0, 0))).reshape(B, nb, MOBA_BLOCK, N_KV_HEADS, HEAD_DIM)
    k_mean = jnp.repeat(jnp.mean(kb.astype(jnp.float32), axis=2), KV_GROUP, axis=2)
    cap = max(1, QUERY_ROWS // B)
    qc = max(d for d in range(1, min(cap, Sq) + 1) if Sq % d == 0)
    n_chunks = Sq // qc
    q_chunks = q.reshape(B, n_chunks, qc, N_HEADS, HEAD_DIM).transpose(1, 0, 3, 2, 4)
    pos_chunks = q_pos.reshape(n_chunks, qc)
    b_idx = jnp.arange(B)[:, None, None, None]
    kv_idx = (jnp.arange(N_HEADS) // KV_GROUP)[None, :, None, None]
    blk_ids = jnp.arange(max(nb, MOBA_TOPK))
    scale = HEAD_DIM ** -0.5

    def one_chunk(args):
        qi, pi = args
        own = pi // MOBA_BLOCK
        gate = jnp.einsum('bhqd,bnhd->bhqn', qi.astype(jnp.float32), k_mean)
        if nb < MOBA_TOPK:
            gate = jnp.pad(gate, ((0, 0), (0, 0), (0, 0), (0, MOBA_TOPK - nb)))
        gate = jnp.where(blk_ids < own[:, None], gate, NEG_INF)
        _, top = lax.top_k(gate, MOBA_TOPK)
        valid = jnp.arange(MOBA_TOPK) < own[:, None]
        sel = jnp.where(valid, top, own[:, None])
        sel = jnp.concatenate([sel, jnp.broadcast_to(own[:, None], (B, N_HEADS, qc, 1))], axis=-1)
        valid = jnp.concatenate([valid, jnp.ones((qc, 1), dtype=bool)], axis=-1)
        kg = kb[b_idx, sel, :, kv_idx]
        vg = vb[b_idx, sel, :, kv_idx]
        key_pos = sel[..., None] * MOBA_BLOCK + jnp.arange(MOBA_BLOCK)
        ok = valid[:, :, None] & (key_pos <= pi[:, None, None])
        s = jnp.einsum('bhqd,bhqnkd->bhqnk', qi, kg).astype(jnp.float32) * scale
        s = jnp.where(ok, s, NEG_INF).reshape(B, N_HEADS, qc, MOBA_NSEL * MOBA_BLOCK)
        p = jax.nn.softmax(s, axis=-1).reshape(B, N_HEADS, qc, MOBA_NSEL, MOBA_BLOCK)
        return jnp.einsum('bhqnk,bhqnkd->bhqd', p.astype(vg.dtype), vg)

    out = lax.map(one_chunk, (q_chunks, pos_chunks))
    return out.transpose(1, 0, 3, 2, 4).reshape(B, Sq, ATTN_WIDTH)


def decoder_layer(x, pos, k_past, v_past, h0, conv_buf, norm_mix, w_in, b_gate, conv_w, conv_b,
                  w_rg_a, b_rg_a, w_rg_x, b_rg_x, lru_lambda, w_proj_a, w_proj_b, w_out,
                  norm_mlp, w_ff1, w_ff2):
    B, S, _ = x.shape
    proj = rms_norm(x, norm_mix) @ w_in
    o1 = ATTN_WIDTH
    o2 = o1 + KV_WIDTH
    o3 = o2 + KV_WIDTH
    o4 = o3 + D_RNN
    o5 = o4 + D_RNN
    q, k, v, xr, yg, gates = jnp.split(proj, [o1, o2, o3, o4, o5], axis=-1)
    q = rope(q.reshape(B, S, N_HEADS, HEAD_DIM), pos)
    k = rope(k.reshape(B, S, N_KV_HEADS, HEAD_DIM), pos)
    v = v.reshape(B, S, N_KV_HEADS, HEAD_DIM)
    if k_past is None:
        k_all, v_all = k, v
    else:
        k_all = jnp.concatenate([k_past.astype(k.dtype), k], axis=1)
        v_all = jnp.concatenate([v_past.astype(v.dtype), v], axis=1)
    attn = moba_attention(q, k_all, v_all, pos)
    rnn, h_last, conv_new = rg_lru_branch(xr, yg, conv_buf, h0, conv_w, conv_b,
                                          w_rg_a, b_rg_a, w_rg_x, b_rg_x, lru_lambda)
    g = jax.nn.sigmoid((gates + b_gate).astype(jnp.float32)).astype(x.dtype)
    merged = g[..., :D_MODEL] * (rnn @ w_proj_a) + g[..., D_MODEL:] * (attn @ w_proj_b)
    x = x + merged @ w_out
    x = x + jnp.square(jax.nn.relu(rms_norm(x, norm_mlp) @ w_ff1)) @ w_ff2
    return x, k, v, h_last, conv_new


def setup_inputs(seed: int = 0) -> dict:
    key = jax.random.key(seed)
    ks = jax.random.split(key, 26)
    n_pages = PAST_LEN // PAGE_SIZE
    n_used = DEC_BATCH * n_pages
    n_phys = n_used + -(-n_used // 4)

    def nrm(k, shape, scale):
        return jax.random.normal(k, shape, jnp.float32) * scale

    page_table = jax.random.permutation(ks[6], n_phys)[:n_used].reshape(DEC_BATCH, n_pages).astype(jnp.int32)
    u = jax.random.uniform(ks[16], (DEPTH, D_RNN), jnp.float32, minval=0.9, maxval=0.999)
    a_base = u ** (1.0 / LRU_C)
    lru_lambda = jnp.log(a_base) - jnp.log1p(-a_base)
    return {
        'x_prompt': nrm(ks[0], (BATCH, SEQ, D_MODEL), 1.0),
        'x_sample': nrm(ks[1], (DEC_BATCH, DEC_SEQ, D_MODEL), 1.0),
        'cache_k': nrm(ks[2], (DEPTH, n_phys, PAGE_SIZE, N_KV_HEADS, HEAD_DIM), 1.0),
        'cache_v': nrm(ks[3], (DEPTH, n_phys, PAGE_SIZE, N_KV_HEADS, HEAD_DIM), 1.0),
        'state_h': nrm(ks[4], (DEPTH, DEC_BATCH, D_RNN), 0.5),
        'state_conv': nrm(ks[5], (DEPTH, DEC_BATCH, CONV_W - 1, D_RNN), 1.0),
        'page_table': page_table,
        'norm_mix': 1.0 + nrm(ks[7], (DEPTH, D_MODEL), 0.02),
        'w_in': nrm(ks[8], (DEPTH, D_MODEL, IN_WIDTH), D_MODEL ** -0.5),
        'b_gate': nrm(ks[9], (DEPTH, 2 * D_MODEL), 0.1),
        'conv_w': nrm(ks[10], (DEPTH, CONV_W, D_RNN), CONV_W ** -0.5),
        'conv_b': nrm(ks[11], (DEPTH, D_RNN), 0.02),
        'w_rg_a': nrm(ks[12], (DEPTH, RNN_BLOCKS, RNN_BLOCK_W, RNN_BLOCK_W), RNN_BLOCK_W ** -0.5),
        'b_rg_a': nrm(ks[13], (DEPTH, D_RNN), 0.1),
        'w_rg_x': nrm(ks[14], (DEPTH, RNN_BLOCKS, RNN_BLOCK_W, RNN_BLOCK_W), RNN_BLOCK_W ** -0.5),
        'b_rg_x': nrm(ks[15], (DEPTH, D_RNN), 0.1),
        'lru_lambda': lru_lambda,
        'w_proj_a': nrm(ks[17], (DEPTH, D_RNN, D_MODEL), D_RNN ** -0.5),
        'w_proj_b': nrm(ks[18], (DEPTH, ATTN_WIDTH, D_MODEL), ATTN_WIDTH ** -0.5),
        'w_out': nrm(ks[19], (DEPTH, D_MODEL, D_MODEL), D_MODEL ** -0.5),
        'norm_mlp': 1.0 + nrm(ks[20], (DEPTH, D_MODEL), 0.02),
        'w_ff1': nrm(ks[21], (DEPTH, D_MODEL, D_FF), D_MODEL ** -0.5),
        'w_ff2': nrm(ks[22], (DEPTH, D_FF, D_MODEL), D_FF ** -0.5),
        'norm_final': 1.0 + nrm(ks[23], (D_MODEL,), 0.02),
    }


def reference(x_prompt, x_sample, cache_k, cache_v, state_h, state_conv, page_table,
              norm_mix, w_in, b_gate, conv_w, conv_b, w_rg_a, b_rg_a, w_rg_x, b_rg_x,
              lru_lambda, w_proj_a, w_proj_b, w_out, norm_mlp, w_ff1, w_ff2, norm_final):
    B, S, _ = x_prompt.shape
    DB, DS, _ = x_sample.shape
    past_len = page_table.shape[1] * cache_k.shape[2]
    pos_p = jnp.arange(S, dtype=jnp.int32)
    pos_s = past_len + jnp.arange(DS, dtype=jnp.int32)
    hp, hs = x_prompt, x_sample
    kp, vp, rp, cp, ks_, vs_, rs, cs = [], [], [], [], [], [], [], []
    for l in range(DEPTH):
        lw = (norm_mix[l], w_in[l], b_gate[l], conv_w[l], conv_b[l], w_rg_a[l], b_rg_a[l],
              w_rg_x[l], b_rg_x[l], lru_lambda[l], w_proj_a[l], w_proj_b[l], w_out[l],
              norm_mlp[l], w_ff1[l], w_ff2[l])
        h0_p = jnp.zeros((B, D_RNN), x_prompt.dtype)
        buf_p = jnp.zeros((B, CONV_W - 1, D_RNN), x_prompt.dtype)
        hp, k1, v1, r1, c1 = decoder_layer(hp, pos_p, None, None, h0_p, buf_p, *lw)
        k_past = cache_k[l][page_table].reshape(DB, past_len, N_KV_HEADS, HEAD_DIM)
        v_past = cache_v[l][page_table].reshape(DB, past_len, N_KV_HEADS, HEAD_DIM)
        hs, k2, v2, r2, c2 = decoder_layer(hs, pos_s, k_past, v_past, state_h[l], state_conv[l], *lw)
        kp.append(k1); vp.append(v1); rp.append(r1); cp.append(c1)
        ks_.append(k2); vs_.append(v2); rs.append(r2); cs.append(c2)
    y_prompt = rms_norm(hp, norm_final)
    y_sample = rms_norm(hs, norm_final)
    return (y_prompt, y_sample, jnp.stack(kp), jnp.stack(vp), jnp.stack(rp), jnp.stack(cp),
            jnp.stack(ks_), jnp.stack(vs_), jnp.stack(rs), jnp.stack(cs))
```

```python
import functools

import jax
import jax.numpy as jnp
from jax import lax
from jax.experimental import pallas as pl
from jax.experimental.pallas import tpu as pltpu

F32 = jnp.float32
BF16 = jnp.bfloat16

N_HEADS = 16
N_KV_HEADS = 4
HEAD_DIM = 128
KV_GROUP = N_HEADS // N_KV_HEADS
ROPE_THETA = 10000.0
MOBA_BLOCK = 256
MOBA_TOPK = 3
RNN_BLOCKS = 16
CONV_W = 4
LRU_C = 8.0
RMS_EPS = 1e-6
NEG_INF = -1e30

LANES = 128
SUBLANES = 8
VMEM_LIMIT_BYTES = 56 * 1024 * 1024
COL_CHUNK = 512

_NT = (((1,), (1,)), ((), ()))


def _params(*sem):
    return pltpu.CompilerParams(dimension_semantics=sem, vmem_limit_bytes=VMEM_LIMIT_BYTES)


def _row_tile(rows, pref):
    t = min(rows, pref)
    assert rows % t == 0, (rows, pref)
    return t


def _rmsnorm_kernel(x_ref, g_ref, o_ref):
    x = x_ref[...]
    ms = jnp.mean(x * x, axis=-1, keepdims=True)
    o_ref[...] = (x * lax.rsqrt(ms + RMS_EPS) * g_ref[...]).astype(o_ref.dtype)


def _rmsnorm(x, g):
    T, D = x.shape
    tm = _row_tile(T, 512)
    return pl.pallas_call(
        _rmsnorm_kernel,
        grid=(T // tm,),
        in_specs=[pl.BlockSpec((tm, D), lambda m: (m, 0)),
                  pl.BlockSpec((1, D), lambda m: (0, 0))],
        out_specs=pl.BlockSpec((tm, D), lambda m: (m, 0)),
        out_shape=jax.ShapeDtypeStruct((T, D), BF16),
        compiler_params=_params("arbitrary"),
        name="rmsnorm",
    )(x, g.reshape(1, D))


def _proj_kernel(u_ref, w_ref, cos_ref, sin_ref, o_ref, *, rope_lo, rope_hi):
    acc = jnp.dot(u_ref[...], w_ref[...], preferred_element_type=F32)
    n = pl.program_id(1)
    roped = jnp.logical_and(n >= rope_lo, n < rope_hi)

    @pl.when(roped)
    def _():
        cos = cos_ref[...]
        sin = sin_ref[...]
        for c in range(acc.shape[1] // HEAD_DIM):
            sl = slice(c * HEAD_DIM, (c + 1) * HEAD_DIM)
            xh = acc[:, sl]
            o_ref[:, sl] = xh * cos + pltpu.roll(xh, HEAD_DIM // 2, 1) * sin

    @pl.when(jnp.logical_not(roped))
    def _():
        o_ref[...] = acc


def _proj(u, w, cos, sin, rope_cols):
    T, D = u.shape
    N = w.shape[1]
    tm = _row_tile(T, 1024)
    tn = COL_CHUNK
    n_tab = cos.shape[0] // tm
    assert cos.shape[0] % tm == 0 and rope_cols[0] % tn == 0 and rope_cols[1] % tn == 0
    kern = functools.partial(_proj_kernel, rope_lo=rope_cols[0] // tn, rope_hi=rope_cols[1] // tn)
    return pl.pallas_call(
        kern,
        grid=(T // tm, N // tn),
        in_specs=[pl.BlockSpec((tm, D), lambda m, n: (m, 0)),
                  pl.BlockSpec((D, tn), lambda m, n: (0, n)),
                  pl.BlockSpec((tm, HEAD_DIM), lambda m, n: (m % n_tab, 0)),
                  pl.BlockSpec((tm, HEAD_DIM), lambda m, n: (m % n_tab, 0))],
        out_specs=pl.BlockSpec((tm, tn), lambda m, n: (m, n)),
        out_shape=jax.ShapeDtypeStruct((T, N), F32),
        compiler_params=_params("arbitrary", "arbitrary"),
        name="in_proj",
    )(u, w, cos, sin)


def _topk_select(gate, blk, eligible, axis):
    gm = jnp.where(eligible, gate, NEG_INF)
    rank = jnp.zeros(gate.shape, jnp.int32)
    for r in range(gate.shape[axis]):
        gr = gm[r:r + 1, :] if axis == 0 else gm[:, r:r + 1]
        beats = jnp.logical_or(gr > gm, jnp.logical_and(gr == gm, blk > r))
        rank = rank + beats.astype(jnp.int32)
    return jnp.logical_and(rank < MOBA_TOPK, eligible)


def _attn_prompt_kernel(q_ref, k_ref, v_ref, o_ref,
                        kb_ref, vt_ref, kmean_ref, sel_ref, qb_ref, m_ref, l_ref, acc_ref,
                        *, tq, nblk, scale):
    i = pl.program_id(2)

    @pl.when(i == 0)
    def _():
        for j in range(nblk):
            rows = slice(j * MOBA_BLOCK, (j + 1) * MOBA_BLOCK)
            kj = k_ref[rows, :]
            kb_ref[j] = kj.astype(BF16)
            kmean_ref[j:j + 1, :] = jnp.mean(kj, axis=0, keepdims=True)
            vt_ref[j] = v_ref[rows, :].T.astype(BF16)

    q = q_ref[...]
    qs = jnp.concatenate([q[:, h * HEAD_DIM:(h + 1) * HEAD_DIM] for h in range(KV_GROUP)], axis=0)
    qb_ref[...] = (qs * scale).astype(BF16)

    gate = lax.dot_general(kmean_ref[...], qs, _NT, precision=lax.Precision.HIGHEST,
                           preferred_element_type=F32)
    blk = lax.broadcasted_iota(jnp.int32, gate.shape, 0)
    sel = _topk_select(gate, blk, blk < i, axis=0)
    sel_ref[...] = sel.astype(F32)

    s = lax.dot_general(kb_ref[i], qb_ref[...], _NT, preferred_element_type=F32)
    key_t = lax.broadcasted_iota(jnp.int32, s.shape, 0)
    qry_t = lax.broadcasted_iota(jnp.int32, s.shape, 1) % tq
    s = jnp.where(key_t <= qry_t, s, NEG_INF)
    m0 = jnp.max(s, axis=0, keepdims=True)
    p = jnp.exp(s - m0)
    m_ref[...] = m0
    l_ref[...] = jnp.sum(p, axis=0, keepdims=True)
    acc_ref[...] = jnp.dot(vt_ref[i], p.astype(BF16), preferred_element_type=F32)

    def body(j, carry):
        s = lax.dot_general(kb_ref[j], qb_ref[...], _NT, preferred_element_type=F32)
        s = jnp.where(sel_ref[pl.ds(j, 1), :] > 0.5, s, NEG_INF)
        m_old = m_ref[...]
        m_new = jnp.maximum(m_old, jnp.max(s, axis=0, keepdims=True))
        alpha = jnp.exp(m_old - m_new)
        p = jnp.exp(s - m_new)
        m_ref[...] = m_new
        l_ref[...] = alpha * l_ref[...] + jnp.sum(p, axis=0, keepdims=True)
        acc_ref[...] = alpha * acc_ref[...] + jnp.dot(vt_ref[j], p.astype(BF16),
                                                      preferred_element_type=F32)
        return carry

    lax.fori_loop(0, i, body, 0)

    out = acc_ref[...] * (1.0 / l_ref[...])
    for h in range(KV_GROUP):
        o_ref[:, h * HEAD_DIM:(h + 1) * HEAD_DIM] = out[:, h * tq:(h + 1) * tq].T.astype(o_ref.dtype)


def _attn_prompt(proj, B, S, q_col, k_col, v_col):
    tq = MOBA_BLOCK
    assert S % MOBA_BLOCK == 0
    nblk = S // MOBA_BLOCK
    nq = S // tq
    gw = KV_GROUP * HEAD_DIM
    n = KV_GROUP * tq
    kern = functools.partial(_attn_prompt_kernel, tq=tq, nblk=nblk, scale=HEAD_DIM ** -0.5)
    return pl.pallas_call(
        kern,
        grid=(B, N_KV_HEADS, nq),
        in_specs=[pl.BlockSpec((tq, gw), lambda b, g, i: (b * nq + i, q_col // gw + g)),
                  pl.BlockSpec((S, HEAD_DIM), lambda b, g, i: (b, k_col // HEAD_DIM + g)),
                  pl.BlockSpec((S, HEAD_DIM), lambda b, g, i: (b, v_col // HEAD_DIM + g))],
        out_specs=pl.BlockSpec((tq, gw), lambda b, g, i: (b * nq + i, g)),
        out_shape=jax.ShapeDtypeStruct((B * S, N_HEADS * HEAD_DIM), BF16),
        scratch_shapes=[pltpu.VMEM((nblk, MOBA_BLOCK, HEAD_DIM), BF16),
                        pltpu.VMEM((nblk, HEAD_DIM, MOBA_BLOCK), BF16),
                        pltpu.VMEM((nblk, HEAD_DIM), F32),
                        pltpu.VMEM((nblk, n), F32),
                        pltpu.VMEM((n, HEAD_DIM), BF16),
                        pltpu.VMEM((1, n), F32),
                        pltpu.VMEM((1, n), F32),
                        pltpu.VMEM((HEAD_DIM, n), F32)],
        compiler_params=_params("arbitrary", "arbitrary", "arbitrary"),
        name="moba_prompt",
    )(proj, proj, proj)


def _attn_decode_kernel(pt_ref, q_ref, kn_ref, vn_ref, *refs, pps, ns, nblk, ds, scale):
    del pt_ref
    k_refs = refs[:pps]
    v_refs = refs[pps:2 * pps]
    o_ref = refs[2 * pps]
    wf_ref, wb_ref, s_ref, p_ref, kmean_ref, l_ref, acc_ref = refs[2 * pps + 1:]
    step = pl.program_id(1)
    bps = pps // 2
    kvw = N_KV_HEADS * HEAD_DIM
    nrow = N_HEADS * ds

    def block_of(page_refs, jj):
        return jnp.concatenate([page_refs[2 * jj][...], page_refs[2 * jj + 1][...]], axis=0)

    @pl.when(step == 0)
    def _():
        q = q_ref[...]
        zero = jnp.zeros((ds, HEAD_DIM), F32)
        rows = []
        for h in range(N_HEADS):
            pieces = [zero] * N_KV_HEADS
            pieces[h // KV_GROUP] = q[:, h * HEAD_DIM:(h + 1) * HEAD_DIM]
            rows.append(jnp.concatenate(pieces, axis=1))
        wf = jnp.concatenate(rows, axis=0)
        wf_ref[...] = wf
        wb_ref[...] = (wf * scale).astype(BF16)

    @pl.when(step < ns)
    def _():
        for jj in range(bps):
            j = step * bps + jj
            kblk = block_of(k_refs, jj)
            kmean_ref[pl.ds(j, 1), :] = jnp.mean(kblk, axis=0, keepdims=True)
            s_ref[j] = lax.dot_general(wb_ref[...], kblk.astype(BF16), _NT,
                                       preferred_element_type=F32)

    @pl.when(step == ns - 1)
    def _():
        gate = lax.dot_general(wf_ref[...], kmean_ref[...], _NT, precision=lax.Precision.HIGHEST,
                               preferred_element_type=F32)
        blk = lax.broadcasted_iota(jnp.int32, gate.shape, 1)
        sel = _topk_select(gate, blk, blk < nblk, axis=1)

        pad = jnp.zeros((nrow - ds, kvw), F32)
        knp = jnp.concatenate([kn_ref[...], pad], axis=0).astype(BF16)
        sn = lax.dot_general(wb_ref[...], knp, _NT, preferred_element_type=F32)
        qry_t = lax.broadcasted_iota(jnp.int32, sn.shape, 0) % ds
        key_t = lax.broadcasted_iota(jnp.int32, sn.shape, 1)
        sn = jnp.where(key_t <= qry_t, sn, NEG_INF)

        mx = jnp.full((nrow, MOBA_BLOCK), NEG_INF, F32)
        for j in range(nblk):
            mx = jnp.maximum(mx, jnp.where(sel[:, j:j + 1], s_ref[j], NEG_INF))
        m = jnp.maximum(jnp.max(mx, axis=1, keepdims=True), jnp.max(sn, axis=1, keepdims=True))
        lsum = jnp.zeros((nrow, MOBA_BLOCK), F32)
        for j in range(nblk):
            p = jnp.exp(jnp.where(sel[:, j:j + 1], s_ref[j], NEG_INF) - m)
            lsum = lsum + p
            p_ref[j] = p.astype(BF16)
        pn = jnp.exp(sn - m)
        l_ref[...] = jnp.sum(lsum, axis=1, keepdims=True) + jnp.sum(pn, axis=1, keepdims=True)
        vnp = jnp.concatenate([vn_ref[...], pad], axis=0).astype(BF16)
        acc_ref[...] = jnp.dot(pn.astype(BF16), vnp, preferred_element_type=F32)

    @pl.when(step >= ns)
    def _():
        for jj in range(bps):
            j = (step - ns) * bps + jj
            vblk = block_of(v_refs, jj).astype(BF16)
            acc_ref[...] += jnp.dot(p_ref[j], vblk, preferred_element_type=F32)

    @pl.when(step == 2 * ns - 1)
    def _():
        acc = acc_ref[...] * (1.0 / l_ref[...])
        outs = []
        for h in range(N_HEADS):
            g = h // KV_GROUP
            outs.append(acc[h * ds:(h + 1) * ds, g * HEAD_DIM:(g + 1) * HEAD_DIM])
        o_ref[...] = jnp.concatenate(outs, axis=1)


def _attn_decode(proj, cache_k, cache_v, page_table, DB, DS, q_col, k_col, v_col):
    n_phys, page, nkv, hd = cache_k.shape
    n_pages = page_table.shape[1]
    past_len = n_pages * page
    kvw = nkv * hd
    assert (nkv, hd) == (N_KV_HEADS, HEAD_DIM) and MOBA_BLOCK == 2 * page
    assert past_len % MOBA_BLOCK == 0 and DS == SUBLANES
    nblk = past_len // MOBA_BLOCK
    pps = 8
    assert n_pages % pps == 0
    ns = n_pages // pps
    nrow = N_HEADS * DS
    ck = cache_k.reshape(n_phys, page, kvw)
    cv = cache_v.reshape(n_phys, page, kvw)

    def k_map(r):
        return lambda b, s, pt: (pt[b, jnp.minimum(s, ns - 1) * pps + r], 0, 0)

    def v_map(r):
        return lambda b, s, pt: (pt[b, jnp.maximum(s - ns, 0) * pps + r], 0, 0)

    qw = N_HEADS * HEAD_DIM
    in_specs = [pl.BlockSpec((DS, qw), lambda b, s, pt: (b, q_col // qw)),
                pl.BlockSpec((DS, kvw), lambda b, s, pt: (b, k_col // kvw)),
                pl.BlockSpec((DS, kvw), lambda b, s, pt: (b, v_col // kvw))]
    in_specs += [pl.BlockSpec((None, page, kvw), k_map(r)) for r in range(pps)]
    in_specs += [pl.BlockSpec((None, page, kvw), v_map(r)) for r in range(pps)]
    kern = functools.partial(_attn_decode_kernel, pps=pps, ns=ns, nblk=nblk, ds=DS,
                             scale=HEAD_DIM ** -0.5)
    grid_spec = pltpu.PrefetchScalarGridSpec(
        num_scalar_prefetch=1,
        grid=(DB, 2 * ns),
        in_specs=in_specs,
        out_specs=pl.BlockSpec((DS, qw), lambda b, s, pt: (b, 0)),
        scratch_shapes=[pltpu.VMEM((nrow, kvw), F32),
                        pltpu.VMEM((nrow, kvw), BF16),
                        pltpu.VMEM((nblk, nrow, MOBA_BLOCK), F32),
                        pltpu.VMEM((nblk, nrow, MOBA_BLOCK), BF16),
                        pltpu.VMEM((nblk, kvw), F32),
                        pltpu.VMEM((nrow, 1), F32),
                        pltpu.VMEM((nrow, kvw), F32)])
    return pl.pallas_call(
        kern,
        grid_spec=grid_spec,
        out_shape=jax.ShapeDtypeStruct((DB * DS, qw), F32),
        compiler_params=_params("arbitrary", "arbitrary"),
        name="moba_decode",
    )(page_table, proj, proj, proj, *([ck] * pps), *([cv] * pps))


def _shift_rows(x, d, fill, row):
    return jnp.where(row >= d, pltpu.roll(x, d, 0), fill)


def _rglru_kernel(xr_ref, yg_ref, h0_ref, cb_ref, cw_ref, cbias_ref, wa_ref, ba_ref, wx_ref, bx_ref,
                  lam_ref, rnn_ref, hlast_ref, cnew_ref, halo_ref, hcarry_ref, *, L):
    c = pl.program_id(1)
    C = xr_ref.shape[1]

    @pl.when(c == 0)
    def _():
        hcarry_ref[...] = h0_ref[...]
        halo_ref[...] = jnp.concatenate(
            [jnp.zeros((SUBLANES - (CONV_W - 1), C), F32), cb_ref[...]], axis=0)

    xr = xr_ref[...]
    xp = jnp.concatenate([halo_ref[...], xr], axis=0)
    cw = cw_ref[...]
    xc = cbias_ref[...]
    base = SUBLANES - (CONV_W - 1)
    for j in range(CONV_W):
        xc = xc + xp[base + j:base + j + L, :] * cw[j:j + 1, :]

    xcb = xc.astype(BF16)
    bw = C // RNN_BLOCKS
    r_parts, i_parts = [], []
    for n in range(RNN_BLOCKS):
        xs = xcb[:, n * bw:(n + 1) * bw]
        r_parts.append(jnp.dot(xs, wa_ref[n], preferred_element_type=F32))
        i_parts.append(jnp.dot(xs, wx_ref[n], preferred_element_type=F32))
    r = jax.nn.sigmoid(jnp.concatenate(r_parts, axis=1) + ba_ref[...])
    ig = jax.nn.sigmoid(jnp.concatenate(i_parts, axis=1) + bx_ref[...])

    nl = -lam_ref[...]
    softplus = jnp.maximum(nl, 0.0) + jnp.log1p(jnp.exp(-jnp.abs(nl)))
    log_a = -LRU_C * r * softplus
    a = jnp.exp(log_a)
    bv = jnp.sqrt(-jnp.tanh(log_a) * (a * a + 1.0)) * (ig * xc)

    row = lax.broadcasted_iota(jnp.int32, (L, 1), 0)
    d = 1
    while d < L:
        bv = a * _shift_rows(bv, d, 0.0, row) + bv
        a = a * _shift_rows(a, d, 1.0, row)
        d *= 2
    h = bv + a * hcarry_ref[...]
    hcarry_ref[...] = h[L - 1:L, :]
    hlast_ref[...] = h[L - 1:L, :]

    yg = yg_ref[...]
    cdf = 0.5 * (1.0 + jnp.tanh(0.7978845608028654 * (yg + 0.044715 * (yg * yg * yg))))
    rnn_ref[...] = (h * (yg * cdf)).astype(rnn_ref.dtype)

    halo_ref[...] = xr[L - SUBLANES:L, :]
    cnew_ref[...] = xr[L - (CONV_W - 1):L, :]


def _rglru(proj, B, S, xr_col, yg_col, h0, conv_buf, conv_w, conv_b, wa, ba, wx, bx, lam):
    C = h0.shape[-1]
    L = min(S, 256)
    assert S % L == 0 and L % SUBLANES == 0 and xr_col % C == 0 and yg_col % C == 0
    nc = S // L
    out_dtype = BF16 if L % 16 == 0 else F32
    row = lambda v: v.reshape(1, C)
    const2 = lambda b, c: (0, 0)
    const3 = lambda b, c: (0, 0, 0)
    rnn, hlast, cnew = pl.pallas_call(
        functools.partial(_rglru_kernel, L=L),
        grid=(B, nc),
        in_specs=[pl.BlockSpec((L, C), lambda b, c: (b * nc + c, xr_col // C)),
                  pl.BlockSpec((L, C), lambda b, c: (b * nc + c, yg_col // C)),
                  pl.BlockSpec((None, 1, C), lambda b, c: (b, 0, 0)),
                  pl.BlockSpec((None, CONV_W - 1, C), lambda b, c: (b, 0, 0)),
                  pl.BlockSpec((CONV_W, C), const2),
                  pl.BlockSpec((1, C), const2),
                  pl.BlockSpec((RNN_BLOCKS, C // RNN_BLOCKS, C // RNN_BLOCKS), const3),
                  pl.BlockSpec((1, C), const2),
                  pl.BlockSpec((RNN_BLOCKS, C // RNN_BLOCKS, C // RNN_BLOCKS), const3),
                  pl.BlockSpec((1, C), const2),
                  pl.BlockSpec((1, C), const2)],
        out_specs=[pl.BlockSpec((L, C), lambda b, c: (b * nc + c, 0)),
                   pl.BlockSpec((None, 1, C), lambda b, c: (b, 0, 0)),
                   pl.BlockSpec((None, CONV_W - 1, C), lambda b, c: (b, 0, 0))],
        out_shape=[jax.ShapeDtypeStruct((B * S, C), out_dtype),
                   jax.ShapeDtypeStruct((B, 1, C), F32),
                   jax.ShapeDtypeStruct((B, CONV_W - 1, C), F32)],
        scratch_shapes=[pltpu.VMEM((SUBLANES, C), F32),
                        pltpu.VMEM((1, C), F32)],
        compiler_params=_params("arbitrary", "arbitrary"),
        name="rglru",
    )(proj, proj, h0.reshape(B, 1, C), conv_buf, conv_w, row(conv_b), wa, row(ba), wx, row(bx), row(lam))
    return rnn, hlast.reshape(B, C), cnew


def _merge_kernel(rnn_ref, attn_ref, u_ref, wa_ref, wb_ref, wga_ref, wgb_ref, bga_ref, bgb_ref, o_ref):
    u = u_ref[...]
    a = jnp.dot(rnn_ref[...].astype(BF16), wa_ref[...], preferred_element_type=F32)
    b = jnp.dot(attn_ref[...].astype(BF16), wb_ref[...], preferred_element_type=F32)
    ga = jax.nn.sigmoid(jnp.dot(u, wga_ref[...], preferred_element_type=F32) + bga_ref[...])
    gb = jax.nn.sigmoid(jnp.dot(u, wgb_ref[...], preferred_element_type=F32) + bgb_ref[...])
    o_ref[...] = (ga * a + gb * b).astype(o_ref.dtype)


def _merge(rnn, attn, u, wa, wb, wg, b_gate):
    T, D = u.shape
    tm = _row_tile(T, 1024)
    tn = COL_CHUNK
    nb = D // tn
    act = lambda m, n: (m, 0)
    col = lambda m, n: (0, n)
    col_hi = lambda m, n: (0, n + nb)
    bg = b_gate.reshape(1, 2 * D)
    return pl.pallas_call(
        _merge_kernel,
        grid=(T // tm, nb),
        in_specs=[pl.BlockSpec((tm, D), act), pl.BlockSpec((tm, D), act), pl.BlockSpec((tm, D), act),
                  pl.BlockSpec((D, tn), col), pl.BlockSpec((D, tn), col),
                  pl.BlockSpec((D, tn), col), pl.BlockSpec((D, tn), col_hi),
                  pl.BlockSpec((1, tn), col), pl.BlockSpec((1, tn), col_hi)],
        out_specs=pl.BlockSpec((tm, tn), lambda m, n: (m, n)),
        out_shape=jax.ShapeDtypeStruct((T, D), BF16),
        compiler_params=_params("arbitrary", "arbitrary"),
        name="merge",
    )(rnn, attn, u, wa, wb, wg, wg, bg, bg)


def _outproj_kernel(mg_ref, x_ref, w_ref, g_ref, x2_ref, u2_ref):
    mg = mg_ref[...]
    D = x_ref.shape[1]
    ssq = jnp.zeros((x_ref.shape[0], 1), F32)
    for c in range(D // COL_CHUNK):
        sl = slice(c * COL_CHUNK, (c + 1) * COL_CHUNK)
        y = x_ref[:, sl] + jnp.dot(mg, w_ref[:, sl], preferred_element_type=F32)
        x2_ref[:, sl] = y
        ssq = ssq + jnp.sum(y * y, axis=-1, keepdims=True)
    inv = lax.rsqrt(ssq / D + RMS_EPS)
    u2_ref[...] = (x2_ref[...] * inv * g_ref[...]).astype(u2_ref.dtype)


def _outproj(merged, x, w_out, g):
    T, D = x.shape
    tm = _row_tile(T, 512)
    row = lambda m: (m, 0)
    return pl.pallas_call(
        _outproj_kernel,
        grid=(T // tm,),
        in_specs=[pl.BlockSpec((tm, D), row), pl.BlockSpec((tm, D), row),
                  pl.BlockSpec((D, D), lambda m: (0, 0)), pl.BlockSpec((1, D), lambda m: (0, 0))],
        out_specs=[pl.BlockSpec((tm, D), row), pl.BlockSpec((tm, D), row)],
        out_shape=[jax.ShapeDtypeStruct((T, D), F32), jax.ShapeDtypeStruct((T, D), BF16)],
        compiler_params=_params("arbitrary"),
        name="out_proj",
    )(merged, x, w_out, g.reshape(1, D))


def _ffn_kernel(u2_ref, x2_ref, w1_ref, w2_ref, gf_ref, y_ref, *, nf, final_norm):
    f = pl.program_id(1)

    @pl.when(f == 0)
    def _():
        y_ref[...] = x2_ref[...]

    h = jnp.dot(u2_ref[...], w1_ref[...], preferred_element_type=F32)
    h = jnp.square(jnp.maximum(h, 0.0)).astype(BF16)
    for c in range(y_ref.shape[1] // COL_CHUNK):
        sl = slice(c * COL_CHUNK, (c + 1) * COL_CHUNK)
        y_ref[:, sl] += jnp.dot(h, w2_ref[:, sl], preferred_element_type=F32)

    if final_norm:
        @pl.when(f == nf - 1)
        def _():
            x3 = y_ref[...]
            ms = jnp.mean(x3 * x3, axis=-1, keepdims=True)
            y_ref[...] = x3 * lax.rsqrt(ms + RMS_EPS) * gf_ref[...]


def _ffn(u2, x2, w1, w2, g_final, final_norm):
    T, D = x2.shape
    F = w1.shape[1]
    tm = _row_tile(T, 512)
    tf = 512
    nf = F // tf
    row = lambda m, f: (m, 0)
    return pl.pallas_call(
        functools.partial(_ffn_kernel, nf=nf, final_norm=final_norm),
        grid=(T // tm, nf),
        in_specs=[pl.BlockSpec((tm, D), row), pl.BlockSpec((tm, D), row),
                  pl.BlockSpec((D, tf), lambda m, f: (0, f)), pl.BlockSpec((tf, D), lambda m, f: (f, 0)),
                  pl.BlockSpec((1, D), lambda m, f: (0, 0))],
        out_specs=pl.BlockSpec((tm, D), row),
        out_shape=jax.ShapeDtypeStruct((T, D), F32),
        compiler_params=_params("arbitrary", "arbitrary"),
        name="ffn",
    )(u2, x2, w1, w2, g_final.reshape(1, D))


def _rope_tables(pos):
    half = HEAD_DIM // 2
    inv = ROPE_THETA ** (-jnp.arange(half, dtype=F32) * (2.0 / HEAD_DIM))
    ang = pos.astype(F32)[:, None] * inv[None, :]
    cos, sin = jnp.cos(ang), jnp.sin(ang)
    return jnp.concatenate([cos, cos], axis=-1), jnp.concatenate([-sin, sin], axis=-1)


def _decoder_layer(x, B, S, cos, sin, past, h0, conv_buf, w, final_norm):
    D = x.shape[1]
    C = h0.shape[-1]
    attn_w = N_HEADS * HEAD_DIM
    kv_w = N_KV_HEADS * HEAD_DIM
    xr_col, yg_col, q_col = 0, C, 2 * C
    k_col, v_col = q_col + attn_w, q_col + attn_w + kv_w

    u = _rmsnorm(x, w["norm_mix"])
    proj = _proj(u, w["w_proj"], cos, sin, (q_col, v_col))
    if past is None:
        attn = _attn_prompt(proj, B, S, q_col, k_col, v_col)
    else:
        attn = _attn_decode(proj, past[0], past[1], past[2], B, S, q_col, k_col, v_col)
    rnn, h_last, conv_new = _rglru(proj, B, S, xr_col, yg_col, h0, conv_buf, w["conv_w"], w["conv_b"],
                                   w["w_rg_a"], w["b_rg_a"], w["w_rg_x"], w["b_rg_x"], w["lru_lambda"])
    merged = _merge(rnn, attn, u, w["w_proj_a"], w["w_proj_b"], w["w_gate"], w["b_gate"])
    x2, u2 = _outproj(merged, x, w["w_out"], w["norm_mlp"])
    y = _ffn(u2, x2, w["w_ff1"], w["w_ff2"], w["norm_final"], final_norm)
    k = proj[:, k_col:k_col + kv_w].reshape(B, S, N_KV_HEADS, HEAD_DIM)
    v = proj[:, v_col:v_col + kv_w].reshape(B, S, N_KV_HEADS, HEAD_DIM)
    return y, k, v, h_last, conv_new


def kernel(x_prompt, x_sample, cache_k, cache_v, state_h, state_conv, page_table, norm_mix, w_in, b_gate, conv_w, conv_b, w_rg_a, b_rg_a, w_rg_x, b_rg_x, lru_lambda, w_proj_a, w_proj_b, w_out, norm_mlp, w_ff1, w_ff2, norm_final):
    B, S, D = x_prompt.shape
    DB, DS, _ = x_sample.shape
    depth = w_in.shape[0]
    C = state_h.shape[-1]
    past_len = page_table.shape[1] * cache_k.shape[2]
    attn_w = N_HEADS * HEAD_DIM
    kv_w = N_KV_HEADS * HEAD_DIM
    o3 = attn_w + 2 * kv_w
    o5 = o3 + 2 * C

    cos_p, sin_p = _rope_tables(jnp.arange(S, dtype=jnp.int32))
    cos_s, sin_s = _rope_tables(past_len + jnp.arange(DS, dtype=jnp.int32))
    cos_s, sin_s = jnp.tile(cos_s, (DB, 1)), jnp.tile(sin_s, (DB, 1))

    hp = x_prompt.reshape(B * S, D)
    hs = x_sample.reshape(DB * DS, D)
    outs = [[] for _ in range(8)]
    for l in range(depth):
        wl = w_in[l]
        w = dict(
            norm_mix=norm_mix[l], norm_mlp=norm_mlp[l], norm_final=norm_final,
            w_proj=jnp.concatenate([wl[:, o3:o5], wl[:, :o3]], axis=1).astype(BF16),
            w_gate=wl[:, o5:].astype(BF16), b_gate=b_gate[l],
            conv_w=conv_w[l], conv_b=conv_b[l],
            w_rg_a=w_rg_a[l].astype(BF16), b_rg_a=b_rg_a[l],
            w_rg_x=w_rg_x[l].astype(BF16), b_rg_x=b_rg_x[l], lru_lambda=lru_lambda[l],
            w_proj_a=w_proj_a[l].astype(BF16), w_proj_b=w_proj_b[l].astype(BF16),
            w_out=w_out[l].astype(BF16), w_ff1=w_ff1[l].astype(BF16), w_ff2=w_ff2[l].astype(BF16))
        final = l == depth - 1
        h0_p = jnp.zeros((B, C), F32)
        buf_p = jnp.zeros((B, CONV_W - 1, C), F32)
        hp, k1, v1, r1, c1 = _decoder_layer(hp, B, S, cos_p, sin_p, None, h0_p, buf_p, w, final)
        past = (cache_k[l], cache_v[l], page_table)
        hs, k2, v2, r2, c2 = _decoder_layer(hs, DB, DS, cos_s, sin_s, past, state_h[l], state_conv[l], w, final)
        for lst, val in zip(outs, (k1, v1, r1, c1, k2, v2, r2, c2)):
            lst.append(val)
    return (hp.reshape(B, S, D), hs.reshape(DB, DS, D), *[jnp.stack(o) for o in outs])
```

```python
import functools

import jax
import jax.numpy as jnp
from jax import lax
from jax.experimental import pallas as pl
from jax.experimental.pallas import tpu as pltpu

F32 = jnp.float32
BF16 = jnp.bfloat16

N_HEADS = 16
N_KV_HEADS = 4
HEAD_DIM = 128
KV_GROUP = N_HEADS // N_KV_HEADS
ROPE_THETA = 10000.0
MOBA_BLOCK = 256
MOBA_TOPK = 3
RNN_BLOCKS = 16
CONV_W = 4
LRU_C = 8.0
RMS_EPS = 1e-6
NEG_INF = -1e30
LOG2_E = 1.4426950408889634

LANES = 128
SUBLANES = 8
VMEM_LIMIT_BYTES = 56 * 1024 * 1024
COL_CHUNK = 512

_NT = (((1,), (1,)), ((), ()))


def _params(*sem):
    return pltpu.CompilerParams(dimension_semantics=sem, vmem_limit_bytes=VMEM_LIMIT_BYTES)


def _row_tile(rows, pref):
    t = min(rows, pref)
    assert rows % t == 0, (rows, pref)
    return t


def _rmsnorm_kernel(x_ref, g_ref, o_ref):
    x = x_ref[...]
    ms = jnp.mean(x * x, axis=-1, keepdims=True)
    o_ref[...] = (x * lax.rsqrt(ms + RMS_EPS) * g_ref[...]).astype(o_ref.dtype)


def _rmsnorm(x, g):
    T, D = x.shape
    tm = _row_tile(T, 512)
    return pl.pallas_call(
        _rmsnorm_kernel,
        grid=(T // tm,),
        in_specs=[pl.BlockSpec((tm, D), lambda m: (m, 0)),
                  pl.BlockSpec((1, D), lambda m: (0, 0))],
        out_specs=pl.BlockSpec((tm, D), lambda m: (m, 0)),
        out_shape=jax.ShapeDtypeStruct((T, D), BF16),
        compiler_params=_params("arbitrary"),
        name="rmsnorm",
    )(x, g.reshape(1, D))


def _proj_kernel(u_ref, w_ref, cos_ref, sin_ref, o_ref, *, rope_lo, rope_hi):
    acc = jnp.dot(u_ref[...], w_ref[...], preferred_element_type=F32)
    n = pl.program_id(1)
    roped = jnp.logical_and(n >= rope_lo, n < rope_hi)

    @pl.when(roped)
    def _():
        cos = cos_ref[...]
        sin = sin_ref[...]
        for c in range(acc.shape[1] // HEAD_DIM):
            sl = slice(c * HEAD_DIM, (c + 1) * HEAD_DIM)
            xh = acc[:, sl]
            o_ref[:, sl] = xh * cos + pltpu.roll(xh, HEAD_DIM // 2, 1) * sin

    @pl.when(jnp.logical_not(roped))
    def _():
        o_ref[...] = acc


def _proj(u, w, cos, sin, rope_cols):
    T, D = u.shape
    N = w.shape[1]
    tm = _row_tile(T, 1024)
    tn = COL_CHUNK
    n_tab = cos.shape[0] // tm
    assert cos.shape[0] % tm == 0 and rope_cols[0] % tn == 0 and rope_cols[1] % tn == 0
    kern = functools.partial(_proj_kernel, rope_lo=rope_cols[0] // tn, rope_hi=rope_cols[1] // tn)
    return pl.pallas_call(
        kern,
        grid=(T // tm, N // tn),
        in_specs=[pl.BlockSpec((tm, D), lambda m, n: (m, 0)),
                  pl.BlockSpec((D, tn), lambda m, n: (0, n)),
                  pl.BlockSpec((tm, HEAD_DIM), lambda m, n: (m % n_tab, 0)),
                  pl.BlockSpec((tm, HEAD_DIM), lambda m, n: (m % n_tab, 0))],
        out_specs=pl.BlockSpec((tm, tn), lambda m, n: (m, n)),
        out_shape=jax.ShapeDtypeStruct((T, N), F32),
        compiler_params=_params("arbitrary", "arbitrary"),
        name="in_proj",
    )(u, w, cos, sin)


def _topk_select(gate, blk, eligible, axis):
    gm = jnp.where(eligible, gate, NEG_INF)
    rank = jnp.zeros(gate.shape, jnp.int32)
    for r in range(gate.shape[axis]):
        gr = gm[r:r + 1, :] if axis == 0 else gm[:, r:r + 1]
        beats = jnp.logical_or(gr > gm, jnp.logical_and(gr == gm, blk > r))
        rank = rank + beats.astype(jnp.int32)
    return jnp.logical_and(rank < MOBA_TOPK, eligible)


def _attn_prompt_kernel(q_ref, k_ref, v_ref, o_ref,
                        kb_ref, vt_ref, kmean_ref, sel_ref, qt_ref, s_ref, m_ref, l_ref, acc_ref,
                        *, tq, nblk, scale):
    i = pl.program_id(2)

    @pl.when(i == 0)
    def _():
        for j in range(nblk):
            rows = slice(j * MOBA_BLOCK, (j + 1) * MOBA_BLOCK)
            kj = k_ref[rows, :]
            kb_ref[j] = kj.astype(BF16)
            kmean_ref[j:j + 1, :] = jnp.mean(kj, axis=0, keepdims=True)
            vt_ref[j] = v_ref[rows, :].T.astype(BF16)

    q = q_ref[...]
    qs = jnp.concatenate([q[:, h * HEAD_DIM:(h + 1) * HEAD_DIM] for h in range(KV_GROUP)], axis=0)
    qt_ref[...] = (qs * (scale * LOG2_E)).T.astype(BF16)

    gate = lax.dot_general(kmean_ref[...], qs, _NT, precision=lax.Precision.HIGHEST,
                           preferred_element_type=F32)
    blk = lax.broadcasted_iota(jnp.int32, gate.shape, 0)
    sel = _topk_select(gate, blk, blk < i, axis=0)
    sel_ref[...] = sel.astype(F32)

    def scores(j):
        return jnp.dot(kb_ref[j], qt_ref[...], preferred_element_type=F32)

    s = scores(i)
    key_t = lax.broadcasted_iota(jnp.int32, s.shape, 0)
    qry_t = lax.broadcasted_iota(jnp.int32, s.shape, 1) % tq
    s = jnp.where(key_t <= qry_t, s, NEG_INF)
    m0 = jnp.max(s, axis=0, keepdims=True)
    p = jnp.exp2(s - m0)
    m_ref[...] = m0
    l_ref[...] = jnp.sum(p, axis=0, keepdims=True)
    acc_ref[...] = jnp.dot(vt_ref[i], p.astype(BF16), preferred_element_type=F32)

    def absorb(slot, j):
        s = s_ref[slot]
        picked = sel_ref[pl.ds(j, 1), :] > 0.5
        m_old = m_ref[...]
        m_new = jnp.maximum(m_old, jnp.where(picked, jnp.max(s, axis=0, keepdims=True), NEG_INF))
        alpha = jnp.exp2(m_old - m_new)
        p = jnp.exp2(s - jnp.where(picked, m_new, -NEG_INF))
        m_ref[...] = m_new
        l_ref[...] = alpha * l_ref[...] + jnp.sum(p, axis=0, keepdims=True)
        acc_ref[...] = alpha * acc_ref[...] + jnp.dot(vt_ref[j], p.astype(BF16),
                                                      preferred_element_type=F32)

    s_ref[0] = scores(0)

    def body(t, carry):
        ja = 2 * t
        s_ref[1] = scores(ja + 1)
        absorb(0, ja)
        s_ref[0] = scores(jnp.minimum(ja + 2, nblk - 1))
        absorb(1, ja + 1)
        return carry

    lax.fori_loop(0, (i + 1) // 2, body, 0)

    out = acc_ref[...] * (1.0 / l_ref[...])
    for h in range(KV_GROUP):
        o_ref[:, h * HEAD_DIM:(h + 1) * HEAD_DIM] = out[:, h * tq:(h + 1) * tq].T.astype(o_ref.dtype)


def _attn_prompt(proj, B, S, q_col, k_col, v_col):
    tq = MOBA_BLOCK
    assert S % MOBA_BLOCK == 0
    nblk = S // MOBA_BLOCK
    nq = S // tq
    gw = KV_GROUP * HEAD_DIM
    n = KV_GROUP * tq
    kern = functools.partial(_attn_prompt_kernel, tq=tq, nblk=nblk, scale=HEAD_DIM ** -0.5)
    return pl.pallas_call(
        kern,
        grid=(B, N_KV_HEADS, nq),
        in_specs=[pl.BlockSpec((tq, gw), lambda b, g, i: (b * nq + i, q_col // gw + g)),
                  pl.BlockSpec((S, HEAD_DIM), lambda b, g, i: (b, k_col // HEAD_DIM + g)),
                  pl.BlockSpec((S, HEAD_DIM), lambda b, g, i: (b, v_col // HEAD_DIM + g))],
        out_specs=pl.BlockSpec((tq, gw), lambda b, g, i: (b * nq + i, g)),
        out_shape=jax.ShapeDtypeStruct((B * S, N_HEADS * HEAD_DIM), BF16),
        scratch_shapes=[pltpu.VMEM((nblk, MOBA_BLOCK, HEAD_DIM), BF16),
                        pltpu.VMEM((nblk, HEAD_DIM, MOBA_BLOCK), BF16),
                        pltpu.VMEM((nblk, HEAD_DIM), F32),
                        pltpu.VMEM((nblk, n), F32),
                        pltpu.VMEM((HEAD_DIM, n), BF16),
                        pltpu.VMEM((2, MOBA_BLOCK, n), F32),
                        pltpu.VMEM((1, n), F32),
                        pltpu.VMEM((1, n), F32),
                        pltpu.VMEM((HEAD_DIM, n), F32)],
        compiler_params=_params("arbitrary", "arbitrary", "arbitrary"),
        name="moba_prompt",
    )(proj, proj, proj)


def _attn_decode_kernel(pt_ref, q_ref, kn_ref, vn_ref, *refs, pps, ns, nblk, ds, scale):
    del pt_ref
    k_refs = refs[:pps]
    v_refs = refs[pps:2 * pps]
    o_ref = refs[2 * pps]
    wf_ref, wb_ref, s_ref, p_ref, kmean_ref, l_ref, acc_ref = refs[2 * pps + 1:]
    step = pl.program_id(1)
    bps = pps // 2
    kvw = N_KV_HEADS * HEAD_DIM
    nrow = N_HEADS * ds

    def block_of(page_refs, jj):
        return jnp.concatenate([page_refs[2 * jj][...], page_refs[2 * jj + 1][...]], axis=0)

    @pl.when(step == 0)
    def _():
        q = q_ref[...]
        zero = jnp.zeros((ds, HEAD_DIM), F32)
        rows = []
        for h in range(N_HEADS):
            pieces = [zero] * N_KV_HEADS
            pieces[h // KV_GROUP] = q[:, h * HEAD_DIM:(h + 1) * HEAD_DIM]
            rows.append(jnp.concatenate(pieces, axis=1))
        wf = jnp.concatenate(rows, axis=0)
        wf_ref[...] = wf
        wb_ref[...] = (wf * scale).astype(BF16)

    @pl.when(step < ns)
    def _():
        for jj in range(bps):
            j = step * bps + jj
            kblk = block_of(k_refs, jj)
            kmean_ref[pl.ds(j, 1), :] = jnp.mean(kblk, axis=0, keepdims=True)
            s_ref[j] = lax.dot_general(wb_ref[...], kblk.astype(BF16), _NT,
                                       preferred_element_type=F32)

    @pl.when(step == ns - 1)
    def _():
        gate = lax.dot_general(wf_ref[...], kmean_ref[...], _NT, precision=lax.Precision.HIGHEST,
                               preferred_element_type=F32)
        blk = lax.broadcasted_iota(jnp.int32, gate.shape, 1)
        sel = _topk_select(gate, blk, blk < nblk, axis=1)

        pad = jnp.zeros((nrow - ds, kvw), F32)
        knp = jnp.concatenate([kn_ref[...], pad], axis=0).astype(BF16)
        sn = lax.dot_general(wb_ref[...], knp, _NT, preferred_element_type=F32)
        qry_t = lax.broadcasted_iota(jnp.int32, sn.shape, 0) % ds
        key_t = lax.broadcasted_iota(jnp.int32, sn.shape, 1)
        sn = jnp.where(key_t <= qry_t, sn, NEG_INF)

        mx = jnp.full((nrow, MOBA_BLOCK), NEG_INF, F32)
        for j in range(nblk):
            mx = jnp.maximum(mx, jnp.where(sel[:, j:j + 1], s_ref[j], NEG_INF))
        m = jnp.maximum(jnp.max(mx, axis=1, keepdims=True), jnp.max(sn, axis=1, keepdims=True))
        lsum = jnp.zeros((nrow, MOBA_BLOCK), F32)
        for j in range(nblk):
            p = jnp.exp(jnp.where(sel[:, j:j + 1], s_ref[j], NEG_INF) - m)
            lsum = lsum + p
            p_ref[j] = p.astype(BF16)
        pn = jnp.exp(sn - m)
        l_ref[...] = jnp.sum(lsum, axis=1, keepdims=True) + jnp.sum(pn, axis=1, keepdims=True)
        vnp = jnp.concatenate([vn_ref[...], pad], axis=0).astype(BF16)
        acc_ref[...] = jnp.dot(pn.astype(BF16), vnp, preferred_element_type=F32)

    @pl.when(step >= ns)
    def _():
        for jj in range(bps):
            j = (step - ns) * bps + jj
            vblk = block_of(v_refs, jj).astype(BF16)
            acc_ref[...] += jnp.dot(p_ref[j], vblk, preferred_element_type=F32)

    @pl.when(step == 2 * ns - 1)
    def _():
        acc = acc_ref[...] * (1.0 / l_ref[...])
        outs = []
        for h in range(N_HEADS):
            g = h // KV_GROUP
            outs.append(acc[h * ds:(h + 1) * ds, g * HEAD_DIM:(g + 1) * HEAD_DIM])
        o_ref[...] = jnp.concatenate(outs, axis=1)


def _attn_decode(proj, cache_k, cache_v, layer, page_table, DB, DS, q_col, k_col, v_col):
    depth, n_phys, page, nkv, hd = cache_k.shape
    n_pages = page_table.shape[1]
    past_len = n_pages * page
    kvw = nkv * hd
    assert (nkv, hd) == (N_KV_HEADS, HEAD_DIM) and MOBA_BLOCK == 2 * page
    assert past_len % MOBA_BLOCK == 0 and DS == SUBLANES
    nblk = past_len // MOBA_BLOCK
    pps = 16
    assert n_pages % pps == 0
    ns = n_pages // pps
    nrow = N_HEADS * DS
    ck = cache_k.reshape(depth * n_phys, page, kvw)
    cv = cache_v.reshape(depth * n_phys, page, kvw)
    base = layer * n_phys

    def k_map(r):
        return lambda b, s, pt: (base + pt[b, jnp.minimum(s, ns - 1) * pps + r], 0, 0)

    def v_map(r):
        return lambda b, s, pt: (base + pt[b, jnp.maximum(s - ns, 0) * pps + r], 0, 0)

    qw = N_HEADS * HEAD_DIM
    in_specs = [pl.BlockSpec((DS, qw), lambda b, s, pt: (b, q_col // qw)),
                pl.BlockSpec((DS, kvw), lambda b, s, pt: (b, k_col // kvw)),
                pl.BlockSpec((DS, kvw), lambda b, s, pt: (b, v_col // kvw))]
    in_specs += [pl.BlockSpec((None, page, kvw), k_map(r)) for r in range(pps)]
    in_specs += [pl.BlockSpec((None, page, kvw), v_map(r)) for r in range(pps)]
    kern = functools.partial(_attn_decode_kernel, pps=pps, ns=ns, nblk=nblk, ds=DS,
                             scale=HEAD_DIM ** -0.5)
    grid_spec = pltpu.PrefetchScalarGridSpec(
        num_scalar_prefetch=1,
        grid=(DB, 2 * ns),
        in_specs=in_specs,
        out_specs=pl.BlockSpec((DS, qw), lambda b, s, pt: (b, 0)),
        scratch_shapes=[pltpu.VMEM((nrow, kvw), F32),
                        pltpu.VMEM((nrow, kvw), BF16),
                        pltpu.VMEM((nblk, nrow, MOBA_BLOCK), F32),
                        pltpu.VMEM((nblk, nrow, MOBA_BLOCK), BF16),
                        pltpu.VMEM((nblk, kvw), F32),
                        pltpu.VMEM((nrow, 1), F32),
                        pltpu.VMEM((nrow, kvw), F32)])
    return pl.pallas_call(
        kern,
        grid_spec=grid_spec,
        out_shape=jax.ShapeDtypeStruct((DB * DS, qw), F32),
        compiler_params=_params("arbitrary", "arbitrary"),
        name="moba_decode",
    )(page_table, proj, proj, proj, *([ck] * pps), *([cv] * pps))


def _rglru_kernel(xr_ref, yg_ref, h0_ref, cb_ref, cw_ref, cbias_ref, wa_ref, ba_ref, wx_ref, bx_ref,
                  lam_ref, rnn_ref, hlast_ref, cnew_ref, halo_ref, hcarry_ref, *, L):
    c = pl.program_id(1)
    C = xr_ref.shape[1]

    @pl.when(c == 0)
    def _():
        hcarry_ref[...] = h0_ref[...]
        halo_ref[...] = jnp.concatenate(
            [jnp.zeros((SUBLANES - (CONV_W - 1), C), F32), cb_ref[...]], axis=0)

    G = L // SUBLANES
    xr = xr_ref[...]
    xg = jnp.concatenate([halo_ref[...], xr], axis=0).reshape(G + 1, SUBLANES, C)
    rin = lax.broadcasted_iota(jnp.int32, (1, SUBLANES, 1), 1)
    cw = cw_ref[...]
    xc = cbias_ref[...].reshape(1, 1, C)
    for j in range(CONV_W):
        k = CONV_W - 1 - j
        if k == 0:
            xk = xg[1:]
        else:
            rolled = pltpu.roll(xg, k, 1)
            xk = jnp.where(rin >= k, rolled[1:], rolled[:-1])
        xc = xc + xk * cw[j:j + 1, :].reshape(1, 1, C)
    xc = xc.reshape(L, C)

    xcb = xc.astype(BF16)
    bw = C // RNN_BLOCKS
    r_parts, i_parts = [], []
    for n in range(RNN_BLOCKS):
        xs = xcb[:, n * bw:(n + 1) * bw]
        r_parts.append(jnp.dot(xs, wa_ref[n], preferred_element_type=F32))
        i_parts.append(jnp.dot(xs, wx_ref[n], preferred_element_type=F32))
    r = jax.nn.sigmoid(jnp.concatenate(r_parts, axis=1) + ba_ref[...])
    ig = jax.nn.sigmoid(jnp.concatenate(i_parts, axis=1) + bx_ref[...])

    nl = -lam_ref[...]
    softplus = jnp.maximum(nl, 0.0) + jnp.log1p(jnp.exp(-jnp.abs(nl)))
    log_a = -LRU_C * r * softplus
    a = jnp.exp(log_a)
    gain2 = -jnp.tanh(log_a) * (a * a + 1.0)
    gain = jnp.where(gain2 > 0.0, gain2 * lax.rsqrt(gain2), 0.0)
    bv = gain * (ig * xc)

    a = a.reshape(G, SUBLANES, C)
    bv = bv.reshape(G, SUBLANES, C)
    d = 1
    while d < SUBLANES:
        same_group = rin >= d
        bv = a * jnp.where(same_group, pltpu.roll(bv, d, 1), 0.0) + bv
        a = a * jnp.where(same_group, pltpu.roll(a, d, 1), 1.0)
        d *= 2
    carry = hcarry_ref[...]
    groups = []
    for g in range(G):
        hg = bv[g] + a[g] * carry
        groups.append(hg)
        carry = hg[SUBLANES - 1:SUBLANES, :]
    h = groups[0] if G == 1 else jnp.concatenate(groups, axis=0)
    hcarry_ref[...] = carry
    hlast_ref[...] = carry

    yg = yg_ref[...]
    cdf = 0.5 * (1.0 + jnp.tanh(0.7978845608028654 * (yg + 0.044715 * (yg * yg * yg))))
    rnn_ref[...] = (h * (yg * cdf)).astype(rnn_ref.dtype)

    halo_ref[...] = xr[L - SUBLANES:L, :]
    cnew_ref[...] = xr[L - (CONV_W - 1):L, :]


def _rglru(proj, B, S, xr_col, yg_col, h0, conv_buf, conv_w, conv_b, wa, ba, wx, bx, lam):
    C = h0.shape[-1]
    L = min(S, 256)
    assert S % L == 0 and L % SUBLANES == 0 and xr_col % C == 0 and yg_col % C == 0
    nc = S // L
    out_dtype = BF16 if L % 16 == 0 else F32
    row = lambda v: v.reshape(1, C)
    const2 = lambda b, c: (0, 0)
    const3 = lambda b, c: (0, 0, 0)
    rnn, hlast, cnew = pl.pallas_call(
        functools.partial(_rglru_kernel, L=L),
        grid=(B, nc),
        in_specs=[pl.BlockSpec((L, C), lambda b, c: (b * nc + c, xr_col // C)),
                  pl.BlockSpec((L, C), lambda b, c: (b * nc + c, yg_col // C)),
                  pl.BlockSpec((None, 1, C), lambda b, c: (b, 0, 0)),
                  pl.BlockSpec((None, CONV_W - 1, C), lambda b, c: (b, 0, 0)),
                  pl.BlockSpec((CONV_W, C), const2),
                  pl.BlockSpec((1, C), const2),
                  pl.BlockSpec((RNN_BLOCKS, C // RNN_BLOCKS, C // RNN_BLOCKS), const3),
                  pl.BlockSpec((1, C), const2),
                  pl.BlockSpec((RNN_BLOCKS, C // RNN_BLOCKS, C // RNN_BLOCKS), const3),
                  pl.BlockSpec((1, C), const2),
                  pl.BlockSpec((1, C), const2)],
        out_specs=[pl.BlockSpec((L, C), lambda b, c: (b * nc + c, 0)),
                   pl.BlockSpec((None, 1, C), lambda b, c: (b, 0, 0)),
                   pl.BlockSpec((None, CONV_W - 1, C), lambda b, c: (b, 0, 0))],
        out_shape=[jax.ShapeDtypeStruct((B * S, C), out_dtype),
                   jax.ShapeDtypeStruct((B, 1, C), F32),
                   jax.ShapeDtypeStruct((B, CONV_W - 1, C), F32)],
        scratch_shapes=[pltpu.VMEM((SUBLANES, C), F32),
                        pltpu.VMEM((1, C), F32)],
        compiler_params=_params("arbitrary", "arbitrary"),
        name="rglru",
    )(proj, proj, h0.reshape(B, 1, C), conv_buf, conv_w, row(conv_b), wa, row(ba), wx, row(bx), row(lam))
    return rnn, hlast.reshape(B, C), cnew


def _merge_kernel(rnn_ref, attn_ref, u_ref, wa_ref, wb_ref, wga_ref, wgb_ref, bga_ref, bgb_ref, o_ref):
    u = u_ref[...]
    a = jnp.dot(rnn_ref[...].astype(BF16), wa_ref[...], preferred_element_type=F32)
    b = jnp.dot(attn_ref[...].astype(BF16), wb_ref[...], preferred_element_type=F32)
    ga = jax.nn.sigmoid(jnp.dot(u, wga_ref[...], preferred_element_type=F32) + bga_ref[...])
    gb = jax.nn.sigmoid(jnp.dot(u, wgb_ref[...], preferred_element_type=F32) + bgb_ref[...])
    o_ref[...] = (ga * a + gb * b).astype(o_ref.dtype)


def _merge(rnn, attn, u, wa, wb, wg, b_gate):
    T, D = u.shape
    tm = _row_tile(T, 1024)
    tn = COL_CHUNK
    nb = D // tn
    act = lambda m, n: (m, 0)
    col = lambda m, n: (0, n)
    col_hi = lambda m, n: (0, n + nb)
    bg = b_gate.reshape(1, 2 * D)
    return pl.pallas_call(
        _merge_kernel,
        grid=(T // tm, nb),
        in_specs=[pl.BlockSpec((tm, D), act), pl.BlockSpec((tm, D), act), pl.BlockSpec((tm, D), act),
                  pl.BlockSpec((D, tn), col), pl.BlockSpec((D, tn), col),
                  pl.BlockSpec((D, tn), col), pl.BlockSpec((D, tn), col_hi),
                  pl.BlockSpec((1, tn), col), pl.BlockSpec((1, tn), col_hi)],
        out_specs=pl.BlockSpec((tm, tn), lambda m, n: (m, n)),
        out_shape=jax.ShapeDtypeStruct((T, D), BF16),
        compiler_params=_params("arbitrary", "arbitrary"),
        name="merge",
    )(rnn, attn, u, wa, wb, wg, wg, bg, bg)


def _outproj_kernel(mg_ref, x_ref, w_ref, g_ref, x2_ref, u2_ref):
    mg = mg_ref[...]
    D = x_ref.shape[1]
    ssq = jnp.zeros((x_ref.shape[0], 1), F32)
    for c in range(D // COL_CHUNK):
        sl = slice(c * COL_CHUNK, (c + 1) * COL_CHUNK)
        y = x_ref[:, sl] + jnp.dot(mg, w_ref[:, sl], preferred_element_type=F32)
        x2_ref[:, sl] = y
        ssq = ssq + jnp.sum(y * y, axis=-1, keepdims=True)
    inv = lax.rsqrt(ssq / D + RMS_EPS)
    u2_ref[...] = (x2_ref[...] * inv * g_ref[...]).astype(u2_ref.dtype)


def _outproj(merged, x, w_out, g):
    T, D = x.shape
    tm = _row_tile(T, 512)
    row = lambda m: (m, 0)
    return pl.pallas_call(
        _outproj_kernel,
        grid=(T // tm,),
        in_specs=[pl.BlockSpec((tm, D), row), pl.BlockSpec((tm, D), row),
                  pl.BlockSpec((D, D), lambda m: (0, 0)), pl.BlockSpec((1, D), lambda m: (0, 0))],
        out_specs=[pl.BlockSpec((tm, D), row), pl.BlockSpec((tm, D), row)],
        out_shape=[jax.ShapeDtypeStruct((T, D), F32), jax.ShapeDtypeStruct((T, D), BF16)],
        compiler_params=_params("arbitrary"),
        name="out_proj",
    )(merged, x, w_out, g.reshape(1, D))


def _ffn_kernel(u2_ref, x2_ref, w1_ref, w2_ref, gf_ref, y_ref, *, nf, final_norm):
    f = pl.program_id(1)

    @pl.when(f == 0)
    def _():
        y_ref[...] = x2_ref[...]

    h = jnp.dot(u2_ref[...], w1_ref[...], preferred_element_type=F32)
    h = jnp.square(jnp.maximum(h, 0.0)).astype(BF16)
    for c in range(y_ref.shape[1] // COL_CHUNK):
        sl = slice(c * COL_CHUNK, (c + 1) * COL_CHUNK)
        y_ref[:, sl] += jnp.dot(h, w2_ref[:, sl], preferred_element_type=F32)

    if final_norm:
        @pl.when(f == nf - 1)
        def _():
            x3 = y_ref[...]
            ms = jnp.mean(x3 * x3, axis=-1, keepdims=True)
            y_ref[...] = x3 * lax.rsqrt(ms + RMS_EPS) * gf_ref[...]


def _ffn(u2, x2, w1, w2, g_final, final_norm):
    T, D = x2.shape
    F = w1.shape[1]
    tm = _row_tile(T, 1024)
    tf = 512
    nf = F // tf
    row = lambda m, f: (m, 0)
    return pl.pallas_call(
        functools.partial(_ffn_kernel, nf=nf, final_norm=final_norm),
        grid=(T // tm, nf),
        in_specs=[pl.BlockSpec((tm, D), row), pl.BlockSpec((tm, D), row),
                  pl.BlockSpec((D, tf), lambda m, f: (0, f)), pl.BlockSpec((tf, D), lambda m, f: (f, 0)),
                  pl.BlockSpec((1, D), lambda m, f: (0, 0))],
        out_specs=pl.BlockSpec((tm, D), row),
        out_shape=jax.ShapeDtypeStruct((T, D), F32),
        compiler_params=_params("arbitrary", "arbitrary"),
        name="ffn",
    )(u2, x2, w1, w2, g_final.reshape(1, D))


def _rope_tables(pos):
    half = HEAD_DIM // 2
    inv = ROPE_THETA ** (-jnp.arange(half, dtype=F32) * (2.0 / HEAD_DIM))
    ang = pos.astype(F32)[:, None] * inv[None, :]
    cos, sin = jnp.cos(ang), jnp.sin(ang)
    return jnp.concatenate([cos, cos], axis=-1), jnp.concatenate([-sin, sin], axis=-1)


def _decoder_layer(x, B, S, cos, sin, past, h0, conv_buf, w, final_norm):
    D = x.shape[1]
    C = h0.shape[-1]
    attn_w = N_HEADS * HEAD_DIM
    kv_w = N_KV_HEADS * HEAD_DIM
    xr_col, yg_col, q_col = 0, C, 2 * C
    k_col, v_col = q_col + attn_w, q_col + attn_w + kv_w

    u = _rmsnorm(x, w["norm_mix"])
    proj = _proj(u, w["w_proj"], cos, sin, (q_col, v_col))
    if past is None:
        attn = _attn_prompt(proj, B, S, q_col, k_col, v_col)
    else:
        attn = _attn_decode(proj, past[0], past[1], past[2], past[3], B, S, q_col, k_col, v_col)
    rnn, h_last, conv_new = _rglru(proj, B, S, xr_col, yg_col, h0, conv_buf, w["conv_w"], w["conv_b"],
                                   w["w_rg_a"], w["b_rg_a"], w["w_rg_x"], w["b_rg_x"], w["lru_lambda"])
    merged = _merge(rnn, attn, u, w["w_proj_a"], w["w_proj_b"], w["w_gate"], w["b_gate"])
    x2, u2 = _outproj(merged, x, w["w_out"], w["norm_mlp"])
    y = _ffn(u2, x2, w["w_ff1"], w["w_ff2"], w["norm_final"], final_norm)
    k = proj[:, k_col:k_col + kv_w].reshape(B, S, N_KV_HEADS, HEAD_DIM)
    v = proj[:, v_col:v_col + kv_w].reshape(B, S, N_KV_HEADS, HEAD_DIM)
    return y, k, v, h_last, conv_new


def kernel(x_prompt, x_sample, cache_k, cache_v, state_h, state_conv, page_table, norm_mix, w_in, b_gate, conv_w, conv_b, w_rg_a, b_rg_a, w_rg_x, b_rg_x, lru_lambda, w_proj_a, w_proj_b, w_out, norm_mlp, w_ff1, w_ff2, norm_final):
    B, S, D = x_prompt.shape
    DB, DS, _ = x_sample.shape
    depth = w_in.shape[0]
    C = state_h.shape[-1]
    past_len = page_table.shape[1] * cache_k.shape[2]
    attn_w = N_HEADS * HEAD_DIM
    kv_w = N_KV_HEADS * HEAD_DIM
    o3 = attn_w + 2 * kv_w
    o5 = o3 + 2 * C

    cos_p, sin_p = _rope_tables(jnp.arange(S, dtype=jnp.int32))
    cos_s, sin_s = _rope_tables(past_len + jnp.arange(DS, dtype=jnp.int32))
    cos_s, sin_s = jnp.tile(cos_s, (DB, 1)), jnp.tile(sin_s, (DB, 1))

    hp = x_prompt.reshape(B * S, D)
    hs = x_sample.reshape(DB * DS, D)
    outs = [[] for _ in range(8)]
    for l in range(depth):
        wl = w_in[l]
        w = dict(
            norm_mix=norm_mix[l], norm_mlp=norm_mlp[l], norm_final=norm_final,
            w_proj=jnp.concatenate([wl[:, o3:o5], wl[:, :o3]], axis=1).astype(BF16),
            w_gate=wl[:, o5:].astype(BF16), b_gate=b_gate[l],
            conv_w=conv_w[l], conv_b=conv_b[l],
            w_rg_a=w_rg_a[l].astype(BF16), b_rg_a=b_rg_a[l],
            w_rg_x=w_rg_x[l].astype(BF16), b_rg_x=b_rg_x[l], lru_lambda=lru_lambda[l],
            w_proj_a=w_proj_a[l].astype(BF16), w_proj_b=w_proj_b[l].astype(BF16),
            w_out=w_out[l].astype(BF16), w_ff1=w_ff1[l].astype(BF16), w_ff2=w_ff2[l].astype(BF16))
        final = l == depth - 1
        h0_p = jnp.zeros((B, C), F32)
        buf_p = jnp.zeros((B, CONV_W - 1, C), F32)
        hp, k1, v1, r1, c1 = _decoder_layer(hp, B, S, cos_p, sin_p, None, h0_p, buf_p, w, final)
        past = (cache_k, cache_v, l, page_table)
        hs, k2, v2, r2, c2 = _decoder_layer(hs, DB, DS, cos_s, sin_s, past, state_h[l], state_conv[l], w, final)
        for lst, val in zip(outs, (k1, v1, r1, c1, k2, v2, r2, c2)):
            lst.append(val)
    return (hp.reshape(B, S, D), hs.reshape(DB, DS, D), *[jnp.stack(o) for o in outs])
```

```python
import functools

import jax
import jax.numpy as jnp
from jax import lax
from jax.experimental import pallas as pl
from jax.experimental.pallas import tpu as pltpu

F32 = jnp.float32
BF16 = jnp.bfloat16

N_HEADS = 16
N_KV_HEADS = 4
HEAD_DIM = 128
KV_GROUP = N_HEADS // N_KV_HEADS
ROPE_THETA = 10000.0
MOBA_BLOCK = 256
MOBA_TOPK = 3
RNN_BLOCKS = 16
CONV_W = 4
LRU_C = 8.0
RMS_EPS = 1e-6
NEG_INF = -1e30
LOG2_E = 1.4426950408889634

LANES = 128
SUBLANES = 8
VMEM_LIMIT_BYTES = 56 * 1024 * 1024
COL_CHUNK = 512

_NT = (((1,), (1,)), ((), ()))


def _params(*sem):
    return pltpu.CompilerParams(dimension_semantics=sem, vmem_limit_bytes=VMEM_LIMIT_BYTES)


def _row_tile(rows, pref):
    t = min(rows, pref)
    assert rows % t == 0, (rows, pref)
    return t


def _rmsnorm_kernel(x_ref, g_ref, o_ref):
    x = x_ref[...]
    ms = jnp.mean(x * x, axis=-1, keepdims=True)
    o_ref[...] = (x * lax.rsqrt(ms + RMS_EPS) * g_ref[...]).astype(o_ref.dtype)


def _rmsnorm(x, g):
    T, D = x.shape
    tm = _row_tile(T, 512)
    return pl.pallas_call(
        _rmsnorm_kernel,
        grid=(T // tm,),
        in_specs=[pl.BlockSpec((tm, D), lambda m: (m, 0)),
                  pl.BlockSpec((1, D), lambda m: (0, 0))],
        out_specs=pl.BlockSpec((tm, D), lambda m: (m, 0)),
        out_shape=jax.ShapeDtypeStruct((T, D), BF16),
        compiler_params=_params("arbitrary"),
        name="rmsnorm",
    )(x, g.reshape(1, D))


def _proj_kernel(u_ref, w_ref, cos_ref, sin_ref, o_ref, *, rope_lo, rope_hi):
    acc = jnp.dot(u_ref[...], w_ref[...], preferred_element_type=F32)
    n = pl.program_id(1)
    roped = jnp.logical_and(n >= rope_lo, n < rope_hi)

    @pl.when(roped)
    def _():
        cos = cos_ref[...]
        sin = sin_ref[...]
        for c in range(acc.shape[1] // HEAD_DIM):
            sl = slice(c * HEAD_DIM, (c + 1) * HEAD_DIM)
            o_ref[:, sl] = _rope(acc[:, sl], cos, sin).astype(o_ref.dtype)

    @pl.when(jnp.logical_not(roped))
    def _():
        o_ref[...] = acc.astype(o_ref.dtype)


def _rope(xh, cos, sin):
    return xh * cos + pltpu.roll(xh, HEAD_DIM // 2, 1) * sin


def _segment_block(segments, tn):
    assert all(s % tn == 0 and w % tn == 0 for s, w in segments)

    def block(n):
        idx, first = None, 0
        for start, width in segments:
            here = start // tn + n - first
            idx = here if idx is None else jnp.where(n >= first, here, idx)
            first += width // tn
        return idx

    return block


def _proj(u, w, segments, cos, sin, rope_cols, out_dtype):
    T, D = u.shape
    N = sum(width for _, width in segments)
    tm = _row_tile(T, 1024)
    tn = COL_CHUNK
    n_tab = cos.shape[0] // tm
    assert cos.shape[0] % tm == 0 and rope_cols[0] % tn == 0 and rope_cols[1] % tn == 0
    kern = functools.partial(_proj_kernel, rope_lo=rope_cols[0] // tn, rope_hi=rope_cols[1] // tn)
    wblock = _segment_block(segments, tn)
    return pl.pallas_call(
        kern,
        grid=(T // tm, N // tn),
        in_specs=[pl.BlockSpec((tm, D), lambda m, n: (m, 0)),
                  pl.BlockSpec((D, tn), lambda m, n: (0, wblock(n))),
                  pl.BlockSpec((tm, HEAD_DIM), lambda m, n: (m % n_tab, 0)),
                  pl.BlockSpec((tm, HEAD_DIM), lambda m, n: (m % n_tab, 0))],
        out_specs=pl.BlockSpec((tm, tn), lambda m, n: (m, n)),
        out_shape=jax.ShapeDtypeStruct((T, N), out_dtype),
        compiler_params=_params("arbitrary", "arbitrary"),
        name="in_proj",
    )(u, w, cos, sin)


def _kv_proj_kernel(u_ref, wk_ref, wv_ref, cos_ref, sin_ref, k_ref, v_ref):
    u = u_ref[...]
    tm = u.shape[0]
    cos = cos_ref[...]
    sin = sin_ref[...]
    k = jnp.dot(u, wk_ref[...], preferred_element_type=F32)
    for g in range(N_KV_HEADS):
        k_ref[pl.ds(g, tm, stride=N_KV_HEADS), :] = _rope(k[:, g * HEAD_DIM:(g + 1) * HEAD_DIM], cos, sin)
    v = jnp.dot(u, wv_ref[...], preferred_element_type=F32)
    for g in range(N_KV_HEADS):
        v_ref[pl.ds(g, tm, stride=N_KV_HEADS), :] = v[:, g * HEAD_DIM:(g + 1) * HEAD_DIM]


def _kv_proj(u, w, k_col, v_col, cos, sin):
    T, D = u.shape
    kvw = N_KV_HEADS * HEAD_DIM
    assert k_col % kvw == 0 and v_col % kvw == 0
    tm = _row_tile(T, 1024)
    n_tab = cos.shape[0] // tm
    assert cos.shape[0] % tm == 0
    out = jax.ShapeDtypeStruct((T * N_KV_HEADS, HEAD_DIM), F32)
    out_spec = pl.BlockSpec((tm * N_KV_HEADS, HEAD_DIM), lambda m: (m, 0))
    return pl.pallas_call(
        _kv_proj_kernel,
        grid=(T // tm,),
        in_specs=[pl.BlockSpec((tm, D), lambda m: (m, 0)),
                  pl.BlockSpec((D, kvw), lambda m: (0, k_col // kvw)),
                  pl.BlockSpec((D, kvw), lambda m: (0, v_col // kvw)),
                  pl.BlockSpec((tm, HEAD_DIM), lambda m: (m % n_tab, 0)),
                  pl.BlockSpec((tm, HEAD_DIM), lambda m: (m % n_tab, 0))],
        out_specs=[out_spec, out_spec],
        out_shape=[out, out],
        compiler_params=_params("arbitrary"),
        name="kv_proj",
    )(u, w, w, cos, sin)


def _topk_select(gate, blk, eligible, axis):
    gm = jnp.where(eligible, gate, NEG_INF)
    rank = jnp.zeros(gate.shape, jnp.int32)
    for r in range(gate.shape[axis]):
        gr = gm[r:r + 1, :] if axis == 0 else gm[:, r:r + 1]
        beats = jnp.logical_or(gr > gm, jnp.logical_and(gr == gm, blk > r))
        rank = rank + beats.astype(jnp.int32)
    return jnp.logical_and(rank < MOBA_TOPK, eligible)


def _attn_prompt_kernel(q_ref, k_ref, v_ref, o_ref,
                        kb_ref, vt_ref, kmean_ref, sel_ref, qt_ref, s_ref, m_ref, l_ref, acc_ref,
                        *, tq, nblk, scale):
    g = pl.program_id(1)
    i = pl.program_id(2)

    @pl.when(i == 0)
    def _():
        for j in range(nblk):
            rows = pl.ds(j * MOBA_BLOCK * N_KV_HEADS + g, MOBA_BLOCK, stride=N_KV_HEADS)
            kj = k_ref[rows, :]
            kb_ref[j] = kj.astype(BF16)
            kmean_ref[j:j + 1, :] = jnp.mean(kj, axis=0, keepdims=True)
            vt_ref[j] = v_ref[rows, :].T.astype(BF16)

    q = q_ref[...].astype(F32)
    qs = jnp.concatenate([q[:, h * HEAD_DIM:(h + 1) * HEAD_DIM] for h in range(KV_GROUP)], axis=0)
    qt_ref[...] = (qs * (scale * LOG2_E)).T.astype(BF16)

    gate = lax.dot_general(kmean_ref[...], qs, _NT, precision=lax.Precision.HIGHEST,
                           preferred_element_type=F32)
    blk = lax.broadcasted_iota(jnp.int32, gate.shape, 0)
    sel = _topk_select(gate, blk, blk < i, axis=0)
    sel_ref[...] = sel.astype(F32)

    def scores(j):
        return jnp.dot(kb_ref[j], qt_ref[...], preferred_element_type=F32)

    s = scores(i)
    key_t = lax.broadcasted_iota(jnp.int32, s.shape, 0)
    qry_t = lax.broadcasted_iota(jnp.int32, s.shape, 1) % tq
    s = jnp.where(key_t <= qry_t, s, NEG_INF)
    m0 = jnp.max(s, axis=0, keepdims=True)
    p = jnp.exp2(s - m0)
    m_ref[...] = m0
    l_ref[...] = jnp.sum(p, axis=0, keepdims=True)
    acc_ref[...] = jnp.dot(vt_ref[i], p.astype(BF16), preferred_element_type=F32)

    def absorb(slot, j):
        s = s_ref[slot]
        picked = sel_ref[pl.ds(j, 1), :] > 0.5
        m_old = m_ref[...]
        m_new = jnp.maximum(m_old, jnp.where(picked, jnp.max(s, axis=0, keepdims=True), NEG_INF))
        alpha = jnp.exp2(m_old - m_new)
        p = jnp.exp2(s - jnp.where(picked, m_new, -NEG_INF))
        m_ref[...] = m_new
        l_ref[...] = alpha * l_ref[...] + jnp.sum(p, axis=0, keepdims=True)
        acc_ref[...] = alpha * acc_ref[...] + jnp.dot(vt_ref[j], p.astype(BF16),
                                                      preferred_element_type=F32)

    s_ref[0] = scores(0)

    def body(t, carry):
        ja = 2 * t
        s_ref[1] = scores(ja + 1)
        absorb(0, ja)
        s_ref[0] = scores(jnp.minimum(ja + 2, nblk - 1))
        absorb(1, ja + 1)
        return carry

    lax.fori_loop(0, (i + 1) // 2, body, 0)

    out = acc_ref[...] * (1.0 / l_ref[...])
    for h in range(KV_GROUP):
        o_ref[:, h * HEAD_DIM:(h + 1) * HEAD_DIM] = out[:, h * tq:(h + 1) * tq].T.astype(o_ref.dtype)


def _attn_prompt(qarr, q_col, k4, v4, B, S):
    tq = MOBA_BLOCK
    assert S % MOBA_BLOCK == 0
    nblk = S // MOBA_BLOCK
    nq = S // tq
    gw = KV_GROUP * HEAD_DIM
    n = KV_GROUP * tq
    kern = functools.partial(_attn_prompt_kernel, tq=tq, nblk=nblk, scale=HEAD_DIM ** -0.5)
    return pl.pallas_call(
        kern,
        grid=(B, N_KV_HEADS, nq),
        in_specs=[pl.BlockSpec((tq, gw), lambda b, g, i: (b * nq + i, q_col // gw + g)),
                  pl.BlockSpec((S * N_KV_HEADS, HEAD_DIM), lambda b, g, i: (b, 0)),
                  pl.BlockSpec((S * N_KV_HEADS, HEAD_DIM), lambda b, g, i: (b, 0))],
        out_specs=pl.BlockSpec((tq, gw), lambda b, g, i: (b * nq + i, g)),
        out_shape=jax.ShapeDtypeStruct((B * S, N_HEADS * HEAD_DIM), BF16),
        scratch_shapes=[pltpu.VMEM((nblk, MOBA_BLOCK, HEAD_DIM), BF16),
                        pltpu.VMEM((nblk, HEAD_DIM, MOBA_BLOCK), BF16),
                        pltpu.VMEM((nblk, HEAD_DIM), F32),
                        pltpu.VMEM((nblk, n), F32),
                        pltpu.VMEM((HEAD_DIM, n), BF16),
                        pltpu.VMEM((2, MOBA_BLOCK, n), F32),
                        pltpu.VMEM((1, n), F32),
                        pltpu.VMEM((1, n), F32),
                        pltpu.VMEM((HEAD_DIM, n), F32)],
        compiler_params=_params("arbitrary", "arbitrary", "arbitrary"),
        name="moba_prompt",
    )(qarr, k4, v4)


def _attn_decode_kernel(pt_ref, q_ref, kn_ref, vn_ref, *refs, pps, ns, nblk, ds, scale):
    del pt_ref
    k_refs = refs[:pps]
    v_refs = refs[pps:2 * pps]
    o_ref = refs[2 * pps]
    wf_ref, wb_ref, s_ref, p_ref, kmean_ref, l_ref, acc_ref = refs[2 * pps + 1:]
    step = pl.program_id(1)
    bps = pps // 2
    kvw = N_KV_HEADS * HEAD_DIM
    nrow = N_HEADS * ds

    def by_head(ref):
        tokens = ref.shape[0] // N_KV_HEADS
        return jnp.concatenate([ref[pl.ds(g, tokens, stride=N_KV_HEADS), :]
                                for g in range(N_KV_HEADS)], axis=1)

    def block_of(page_refs, jj):
        return jnp.concatenate([by_head(page_refs[2 * jj]), by_head(page_refs[2 * jj + 1])], axis=0)

    @pl.when(step == 0)
    def _():
        q = q_ref[...]
        zero = jnp.zeros((ds, HEAD_DIM), F32)
        rows = []
        for h in range(N_HEADS):
            pieces = [zero] * N_KV_HEADS
            pieces[h // KV_GROUP] = q[:, h * HEAD_DIM:(h + 1) * HEAD_DIM]
            rows.append(jnp.concatenate(pieces, axis=1))
        wf = jnp.concatenate(rows, axis=0)
        wf_ref[...] = wf
        wb_ref[...] = (wf * scale).astype(BF16)

    @pl.when(step < ns)
    def _():
        for jj in range(bps):
            j = step * bps + jj
            kblk = block_of(k_refs, jj)
            kmean_ref[pl.ds(j, 1), :] = jnp.mean(kblk, axis=0, keepdims=True)
            s_ref[j] = lax.dot_general(wb_ref[...], kblk.astype(BF16), _NT,
                                       preferred_element_type=F32)

    @pl.when(step == ns - 1)
    def _():
        gate = lax.dot_general(wf_ref[...], kmean_ref[...], _NT, precision=lax.Precision.HIGHEST,
                               preferred_element_type=F32)
        blk = lax.broadcasted_iota(jnp.int32, gate.shape, 1)
        sel = _topk_select(gate, blk, blk < nblk, axis=1)

        pad = jnp.zeros((nrow - ds, kvw), F32)
        knp = jnp.concatenate([by_head(kn_ref), pad], axis=0).astype(BF16)
        sn = lax.dot_general(wb_ref[...], knp, _NT, preferred_element_type=F32)
        qry_t = lax.broadcasted_iota(jnp.int32, sn.shape, 0) % ds
        key_t = lax.broadcasted_iota(jnp.int32, sn.shape, 1)
        sn = jnp.where(key_t <= qry_t, sn, NEG_INF)

        mx = jnp.full((nrow, MOBA_BLOCK), NEG_INF, F32)
        for j in range(nblk):
            mx = jnp.maximum(mx, jnp.where(sel[:, j:j + 1], s_ref[j], NEG_INF))
        m = jnp.maximum(jnp.max(mx, axis=1, keepdims=True), jnp.max(sn, axis=1, keepdims=True))
        lsum = jnp.zeros((nrow, MOBA_BLOCK), F32)
        for j in range(nblk):
            p = jnp.exp(jnp.where(sel[:, j:j + 1], s_ref[j], NEG_INF) - m)
            lsum = lsum + p
            p_ref[j] = p.astype(BF16)
        pn = jnp.exp(sn - m)
        l_ref[...] = jnp.sum(lsum, axis=1, keepdims=True) + jnp.sum(pn, axis=1, keepdims=True)
        vnp = jnp.concatenate([by_head(vn_ref), pad], axis=0).astype(BF16)
        acc_ref[...] = jnp.dot(pn.astype(BF16), vnp, preferred_element_type=F32)

    @pl.when(step >= ns)
    def _():
        for jj in range(bps):
            j = (step - ns) * bps + jj
            vblk = block_of(v_refs, jj).astype(BF16)
            acc_ref[...] += jnp.dot(p_ref[j], vblk, preferred_element_type=F32)

    @pl.when(step == 2 * ns - 1)
    def _():
        acc = acc_ref[...] * (1.0 / l_ref[...])
        outs = []
        for h in range(N_HEADS):
            g = h // KV_GROUP
            outs.append(acc[h * ds:(h + 1) * ds, g * HEAD_DIM:(g + 1) * HEAD_DIM])
        o_ref[...] = jnp.concatenate(outs, axis=1)


def _attn_decode(qarr, q_col, k4, v4, cache_k, cache_v, layer, page_table, DB, DS):
    depth, n_phys, page, nkv, hd = cache_k.shape
    n_pages = page_table.shape[1]
    past_len = n_pages * page
    kvw = nkv * hd
    assert (nkv, hd) == (N_KV_HEADS, HEAD_DIM) and MOBA_BLOCK == 2 * page
    assert past_len % MOBA_BLOCK == 0 and DS == SUBLANES
    nblk = past_len // MOBA_BLOCK
    pps = 16
    assert n_pages % pps == 0
    ns = n_pages // pps
    nrow = N_HEADS * DS
    ck = cache_k.reshape(depth * n_phys, page * nkv, hd)
    cv = cache_v.reshape(depth * n_phys, page * nkv, hd)
    base = layer * n_phys

    def k_map(r):
        return lambda b, s, pt: (base + pt[b, jnp.minimum(s, ns - 1) * pps + r], 0, 0)

    def v_map(r):
        return lambda b, s, pt: (base + pt[b, jnp.maximum(s - ns, 0) * pps + r], 0, 0)

    qw = N_HEADS * HEAD_DIM
    in_specs = [pl.BlockSpec((DS, qw), lambda b, s, pt: (b, q_col // qw)),
                pl.BlockSpec((DS * nkv, hd), lambda b, s, pt: (b, 0)),
                pl.BlockSpec((DS * nkv, hd), lambda b, s, pt: (b, 0))]
    in_specs += [pl.BlockSpec((None, page * nkv, hd), k_map(r)) for r in range(pps)]
    in_specs += [pl.BlockSpec((None, page * nkv, hd), v_map(r)) for r in range(pps)]
    kern = functools.partial(_attn_decode_kernel, pps=pps, ns=ns, nblk=nblk, ds=DS,
                             scale=HEAD_DIM ** -0.5)
    grid_spec = pltpu.PrefetchScalarGridSpec(
        num_scalar_prefetch=1,
        grid=(DB, 2 * ns),
        in_specs=in_specs,
        out_specs=pl.BlockSpec((DS, qw), lambda b, s, pt: (b, 0)),
        scratch_shapes=[pltpu.VMEM((nrow, kvw), F32),
                        pltpu.VMEM((nrow, kvw), BF16),
                        pltpu.VMEM((nblk, nrow, MOBA_BLOCK), F32),
                        pltpu.VMEM((nblk, nrow, MOBA_BLOCK), BF16),
                        pltpu.VMEM((nblk, kvw), F32),
                        pltpu.VMEM((nrow, 1), F32),
                        pltpu.VMEM((nrow, kvw), F32)])
    return pl.pallas_call(
        kern,
        grid_spec=grid_spec,
        out_shape=jax.ShapeDtypeStruct((DB * DS, qw), F32),
        compiler_params=_params("arbitrary", "arbitrary"),
        name="moba_decode",
    )(page_table, qarr, k4, v4, *([ck] * pps), *([cv] * pps))


def _rglru_kernel(xr_ref, yg_ref, h0_ref, cb_ref, cw_ref, cbias_ref, wa_ref, ba_ref, wx_ref, bx_ref,
                  lam_ref, rnn_ref, hlast_ref, cnew_ref, halo_ref, hcarry_ref, *, L):
    c = pl.program_id(1)
    C = xr_ref.shape[1]

    @pl.when(c == 0)
    def _():
        hcarry_ref[...] = h0_ref[...]
        halo_ref[...] = jnp.concatenate(
            [jnp.zeros((SUBLANES - (CONV_W - 1), C), F32), cb_ref[...]], axis=0)

    G = L // SUBLANES
    xr = xr_ref[...]
    xg = jnp.concatenate([halo_ref[...], xr], axis=0).reshape(G + 1, SUBLANES, C)
    rin = lax.broadcasted_iota(jnp.int32, (1, SUBLANES, 1), 1)
    cw = cw_ref[...]
    xc = cbias_ref[...].reshape(1, 1, C)
    for j in range(CONV_W):
        k = CONV_W - 1 - j
        if k == 0:
            xk = xg[1:]
        else:
            rolled = pltpu.roll(xg, k, 1)
            xk = jnp.where(rin >= k, rolled[1:], rolled[:-1])
        xc = xc + xk * cw[j:j + 1, :].reshape(1, 1, C)
    xc = xc.reshape(L, C)

    xcb = xc.astype(BF16)
    bw = C // RNN_BLOCKS
    r_parts, i_parts = [], []
    for n in range(RNN_BLOCKS):
        xs = xcb[:, n * bw:(n + 1) * bw]
        r_parts.append(jnp.dot(xs, wa_ref[n], preferred_element_type=F32))
        i_parts.append(jnp.dot(xs, wx_ref[n], preferred_element_type=F32))
    r = jax.nn.sigmoid(jnp.concatenate(r_parts, axis=1) + ba_ref[...])
    ig = jax.nn.sigmoid(jnp.concatenate(i_parts, axis=1) + bx_ref[...])

    nl = -lam_ref[...]
    softplus = jnp.maximum(nl, 0.0) + jnp.log1p(jnp.exp(-jnp.abs(nl)))
    log_a = -LRU_C * r * softplus
    a = jnp.exp(log_a)
    gain2 = -jnp.tanh(log_a) * (a * a + 1.0)
    gain = jnp.where(gain2 > 0.0, gain2 * lax.rsqrt(gain2), 0.0)
    bv = gain * (ig * xc)

    a = a.reshape(G, SUBLANES, C)
    bv = bv.reshape(G, SUBLANES, C)
    d = 1
    while d < SUBLANES:
        same_group = rin >= d
        bv = a * jnp.where(same_group, pltpu.roll(bv, d, 1), 0.0) + bv
        a = a * jnp.where(same_group, pltpu.roll(a, d, 1), 1.0)
        d *= 2
    carry = hcarry_ref[...]
    groups = []
    for g in range(G):
        hg = bv[g] + a[g] * carry
        groups.append(hg)
        carry = hg[SUBLANES - 1:SUBLANES, :]
    h = groups[0] if G == 1 else jnp.concatenate(groups, axis=0)
    hcarry_ref[...] = carry
    hlast_ref[...] = carry

    yg = yg_ref[...].astype(F32)
    cdf = 0.5 * (1.0 + jnp.tanh(0.7978845608028654 * (yg + 0.044715 * (yg * yg * yg))))
    rnn_ref[...] = (h * (yg * cdf)).astype(rnn_ref.dtype)

    halo_ref[...] = xr[L - SUBLANES:L, :]
    cnew_ref[...] = xr[L - (CONV_W - 1):L, :]


def _rglru(xr_arr, xr_col, yg_arr, yg_col, B, S, h0, conv_buf, conv_w, conv_b, wa, ba, wx, bx, lam):
    C = h0.shape[-1]
    L = min(S, 256)
    assert S % L == 0 and L % SUBLANES == 0 and xr_col % C == 0 and yg_col % C == 0
    nc = S // L
    out_dtype = BF16 if L % 16 == 0 else F32
    row = lambda v: v.reshape(1, C)
    const2 = lambda b, c: (0, 0)
    const3 = lambda b, c: (0, 0, 0)
    rnn, hlast, cnew = pl.pallas_call(
        functools.partial(_rglru_kernel, L=L),
        grid=(B, nc),
        in_specs=[pl.BlockSpec((L, C), lambda b, c: (b * nc + c, xr_col // C)),
                  pl.BlockSpec((L, C), lambda b, c: (b * nc + c, yg_col // C)),
                  pl.BlockSpec((None, 1, C), lambda b, c: (b, 0, 0)),
                  pl.BlockSpec((None, CONV_W - 1, C), lambda b, c: (b, 0, 0)),
                  pl.BlockSpec((CONV_W, C), const2),
                  pl.BlockSpec((1, C), const2),
                  pl.BlockSpec((RNN_BLOCKS, C // RNN_BLOCKS, C // RNN_BLOCKS), const3),
                  pl.BlockSpec((1, C), const2),
                  pl.BlockSpec((RNN_BLOCKS, C // RNN_BLOCKS, C // RNN_BLOCKS), const3),
                  pl.BlockSpec((1, C), const2),
                  pl.BlockSpec((1, C), const2)],
        out_specs=[pl.BlockSpec((L, C), lambda b, c: (b * nc + c, 0)),
                   pl.BlockSpec((None, 1, C), lambda b, c: (b, 0, 0)),
                   pl.BlockSpec((None, CONV_W - 1, C), lambda b, c: (b, 0, 0))],
        out_shape=[jax.ShapeDtypeStruct((B * S, C), out_dtype),
                   jax.ShapeDtypeStruct((B, 1, C), F32),
                   jax.ShapeDtypeStruct((B, CONV_W - 1, C), F32)],
        scratch_shapes=[pltpu.VMEM((SUBLANES, C), F32),
                        pltpu.VMEM((1, C), F32)],
        compiler_params=_params("arbitrary", "arbitrary"),
        name="rglru",
    )(xr_arr, yg_arr, h0.reshape(B, 1, C), conv_buf, conv_w, row(conv_b), wa, row(ba), wx, row(bx), row(lam))
    return rnn, hlast.reshape(B, C), cnew


def _merge_kernel(rnn_ref, attn_ref, u_ref, wa_ref, wb_ref, wga_ref, wgb_ref, bga_ref, bgb_ref, o_ref):
    u = u_ref[...]
    a = jnp.dot(rnn_ref[...].astype(BF16), wa_ref[...], preferred_element_type=F32)
    b = jnp.dot(attn_ref[...].astype(BF16), wb_ref[...], preferred_element_type=F32)
    ga = jax.nn.sigmoid(jnp.dot(u, wga_ref[...], preferred_element_type=F32) + bga_ref[...])
    gb = jax.nn.sigmoid(jnp.dot(u, wgb_ref[...], preferred_element_type=F32) + bgb_ref[...])
    o_ref[...] = (ga * a + gb * b).astype(o_ref.dtype)


def _merge(rnn, attn, u, wa, wb, w_in, gate_col, b_gate):
    T, D = u.shape
    tm = _row_tile(T, 1024)
    tn = COL_CHUNK
    nb = D // tn
    assert gate_col % tn == 0
    act = lambda m, n: (m, 0)
    col = lambda m, n: (0, n)
    col_hi = lambda m, n: (0, n + nb)
    gcol = lambda m, n: (0, gate_col // tn + n)
    gcol_hi = lambda m, n: (0, gate_col // tn + nb + n)
    bg = b_gate.reshape(1, 2 * D)
    return pl.pallas_call(
        _merge_kernel,
        grid=(T // tm, nb),
        in_specs=[pl.BlockSpec((tm, D), act), pl.BlockSpec((tm, D), act), pl.BlockSpec((tm, D), act),
                  pl.BlockSpec((D, tn), col), pl.BlockSpec((D, tn), col),
                  pl.BlockSpec((D, tn), gcol), pl.BlockSpec((D, tn), gcol_hi),
                  pl.BlockSpec((1, tn), col), pl.BlockSpec((1, tn), col_hi)],
        out_specs=pl.BlockSpec((tm, tn), lambda m, n: (m, n)),
        out_shape=jax.ShapeDtypeStruct((T, D), BF16),
        compiler_params=_params("arbitrary", "arbitrary"),
        name="merge",
    )(rnn, attn, u, wa, wb, w_in, w_in, bg, bg)


def _outproj_kernel(mg_ref, x_ref, w_ref, g_ref, x2_ref, u2_ref):
    mg = mg_ref[...]
    D = x_ref.shape[1]
    ssq = jnp.zeros((x_ref.shape[0], 1), F32)
    for c in range(D // COL_CHUNK):
        sl = slice(c * COL_CHUNK, (c + 1) * COL_CHUNK)
        y = x_ref[:, sl] + jnp.dot(mg, w_ref[:, sl], preferred_element_type=F32)
        x2_ref[:, sl] = y
        ssq = ssq + jnp.sum(y * y, axis=-1, keepdims=True)
    inv = lax.rsqrt(ssq / D + RMS_EPS)
    u2_ref[...] = (x2_ref[...] * inv * g_ref[...]).astype(u2_ref.dtype)


def _outproj(merged, x, w_out, g):
    T, D = x.shape
    tm = _row_tile(T, 512)
    row = lambda m: (m, 0)
    return pl.pallas_call(
        _outproj_kernel,
        grid=(T // tm,),
        in_specs=[pl.BlockSpec((tm, D), row), pl.BlockSpec((tm, D), row),
                  pl.BlockSpec((D, D), lambda m: (0, 0)), pl.BlockSpec((1, D), lambda m: (0, 0))],
        out_specs=[pl.BlockSpec((tm, D), row), pl.BlockSpec((tm, D), row)],
        out_shape=[jax.ShapeDtypeStruct((T, D), F32), jax.ShapeDtypeStruct((T, D), BF16)],
        compiler_params=_params("arbitrary"),
        name="out_proj",
    )(merged, x, w_out, g.reshape(1, D))


def _ffn_kernel(u2_ref, x2_ref, w1_ref, w2_ref, gf_ref, y_ref, *, nf, final_norm):
    f = pl.program_id(1)

    @pl.when(f == 0)
    def _():
        y_ref[...] = x2_ref[...]

    h = jnp.dot(u2_ref[...], w1_ref[...], preferred_element_type=F32)
    h = jnp.square(jnp.maximum(h, 0.0)).astype(BF16)
    for c in range(y_ref.shape[1] // COL_CHUNK):
        sl = slice(c * COL_CHUNK, (c + 1) * COL_CHUNK)
        y_ref[:, sl] += jnp.dot(h, w2_ref[:, sl], preferred_element_type=F32)

    if final_norm:
        @pl.when(f == nf - 1)
        def _():
            x3 = y_ref[...]
            ms = jnp.mean(x3 * x3, axis=-1, keepdims=True)
            y_ref[...] = x3 * lax.rsqrt(ms + RMS_EPS) * gf_ref[...]


def _ffn(u2, x2, w1, w2, g_final, final_norm):
    T, D = x2.shape
    F = w1.shape[1]
    tm = _row_tile(T, 1024)
    tf = 512
    nf = F // tf
    row = lambda m, f: (m, 0)
    return pl.pallas_call(
        functools.partial(_ffn_kernel, nf=nf, final_norm=final_norm),
        grid=(T // tm, nf),
        in_specs=[pl.BlockSpec((tm, D), row), pl.BlockSpec((tm, D), row),
                  pl.BlockSpec((D, tf), lambda m, f: (0, f)), pl.BlockSpec((tf, D), lambda m, f: (f, 0)),
                  pl.BlockSpec((1, D), lambda m, f: (0, 0))],
        out_specs=pl.BlockSpec((tm, D), row),
        out_shape=jax.ShapeDtypeStruct((T, D), F32),
        compiler_params=_params("arbitrary", "arbitrary"),
        name="ffn",
    )(u2, x2, w1, w2, g_final.reshape(1, D))


def _rope_tables(pos):
    half = HEAD_DIM // 2
    inv = ROPE_THETA ** (-jnp.arange(half, dtype=F32) * (2.0 / HEAD_DIM))
    ang = pos.astype(F32)[:, None] * inv[None, :]
    cos, sin = jnp.cos(ang), jnp.sin(ang)
    return jnp.concatenate([cos, cos], axis=-1), jnp.concatenate([-sin, sin], axis=-1)


def _decoder_layer(x, B, S, cos, sin, past, h0, conv_buf, w, final_norm):
    C = h0.shape[-1]
    attn_w = N_HEADS * HEAD_DIM
    act_dtype = BF16 if S % 16 == 0 else F32

    kv_w = N_KV_HEADS * HEAD_DIM
    q_col, k_col, v_col = 0, attn_w, attn_w + kv_w
    xr_col = v_col + kv_w
    yg_col = xr_col + C
    gate_col = yg_col + C
    w_in = w["w_in"]

    u = _rmsnorm(x, w["norm_mix"])
    xr = _proj(u, w_in, [(xr_col, C)], cos, sin, (0, 0), F32)
    ygq = _proj(u, w_in, [(yg_col, C), (q_col, attn_w)], cos, sin,
                (C, C + attn_w), act_dtype)
    k4, v4 = _kv_proj(u, w_in, k_col, v_col, cos, sin)
    if past is None:
        attn = _attn_prompt(ygq, C, k4, v4, B, S)
    else:
        attn = _attn_decode(ygq, C, k4, v4, past[0], past[1], past[2], past[3], B, S)
    rnn, h_last, conv_new = _rglru(xr, 0, ygq, 0, B, S, h0, conv_buf, w["conv_w"], w["conv_b"],
                                   w["w_rg_a"], w["b_rg_a"], w["w_rg_x"], w["b_rg_x"], w["lru_lambda"])
    merged = _merge(rnn, attn, u, w["w_proj_a"], w["w_proj_b"], w_in, gate_col, w["b_gate"])
    x2, u2 = _outproj(merged, x, w["w_out"], w["norm_mlp"])
    y = _ffn(u2, x2, w["w_ff1"], w["w_ff2"], w["norm_final"], final_norm)
    k = k4.reshape(B, S, N_KV_HEADS, HEAD_DIM)
    v = v4.reshape(B, S, N_KV_HEADS, HEAD_DIM)
    return y, k, v, h_last, conv_new


def kernel(x_prompt, x_sample, cache_k, cache_v, state_h, state_conv, page_table, norm_mix, w_in, b_gate, conv_w, conv_b, w_rg_a, b_rg_a, w_rg_x, b_rg_x, lru_lambda, w_proj_a, w_proj_b, w_out, norm_mlp, w_ff1, w_ff2, norm_final):
    B, S, D = x_prompt.shape
    DB, DS, _ = x_sample.shape
    depth = w_in.shape[0]
    C = state_h.shape[-1]
    past_len = page_table.shape[1] * cache_k.shape[2]

    cos_p, sin_p = _rope_tables(jnp.arange(S, dtype=jnp.int32))
    cos_s, sin_s = _rope_tables(past_len + jnp.arange(DS, dtype=jnp.int32))
    cos_s, sin_s = jnp.tile(cos_s, (DB, 1)), jnp.tile(sin_s, (DB, 1))

    hp = x_prompt.reshape(B * S, D)
    hs = x_sample.reshape(DB * DS, D)
    outs = [[] for _ in range(8)]
    for l in range(depth):
        wl = w_in[l]
        w = dict(
            norm_mix=norm_mix[l], norm_mlp=norm_mlp[l], norm_final=norm_final,
            w_in=wl.astype(BF16), b_gate=b_gate[l],
            conv_w=conv_w[l], conv_b=conv_b[l],
            w_rg_a=w_rg_a[l].astype(BF16), b_rg_a=b_rg_a[l],
            w_rg_x=w_rg_x[l].astype(BF16), b_rg_x=b_rg_x[l], lru_lambda=lru_lambda[l],
            w_proj_a=w_proj_a[l].astype(BF16), w_proj_b=w_proj_b[l].astype(BF16),
            w_out=w_out[l].astype(BF16), w_ff1=w_ff1[l].astype(BF16), w_ff2=w_ff2[l].astype(BF16))
        final = l == depth - 1
        h0_p = jnp.zeros((B, C), F32)
        buf_p = jnp.zeros((B, CONV_W - 1, C), F32)
        hp, k1, v1, r1, c1 = _decoder_layer(hp, B, S, cos_p, sin_p, None, h0_p, buf_p, w, final)
        past = (cache_k, cache_v, l, page_table)
        hs, k2, v2, r2, c2 = _decoder_layer(hs, DB, DS, cos_s, sin_s, past, state_h[l], state_conv[l], w, final)
        for lst, val in zip(outs, (k1, v1, r1, c1, k2, v2, r2, c2)):
            lst.append(val)
    return (hp.reshape(B, S, D), hs.reshape(DB, DS, D), *[jnp.stack(o) for o in outs])
```

```python
import functools

import jax
import jax.numpy as jnp
from jax import lax
from jax.experimental import pallas as pl
from jax.experimental.pallas import tpu as pltpu

F32 = jnp.float32
BF16 = jnp.bfloat16

N_HEADS = 16
N_KV_HEADS = 4
HEAD_DIM = 128
KV_GROUP = N_HEADS // N_KV_HEADS
ROPE_THETA = 10000.0
MOBA_BLOCK = 256
MOBA_TOPK = 3
RNN_BLOCKS = 16
CONV_W = 4
LRU_C = 8.0
RMS_EPS = 1e-6
NEG_INF = -1e30
LOG2_E = 1.4426950408889634

LANES = 128
SUBLANES = 8
VMEM_LIMIT_BYTES = 56 * 1024 * 1024
COL_CHUNK = 512

_NT = (((1,), (1,)), ((), ()))


def _params(*sem):
    return pltpu.CompilerParams(dimension_semantics=sem, vmem_limit_bytes=VMEM_LIMIT_BYTES)


def _row_tile(rows, pref):
    t = min(rows, pref)
    assert rows % t == 0, (rows, pref)
    return t


def _rmsnorm_kernel(x_ref, g_ref, o_ref):
    x = x_ref[...]
    ms = jnp.mean(x * x, axis=-1, keepdims=True)
    o_ref[...] = (x * lax.rsqrt(ms + RMS_EPS) * g_ref[...]).astype(o_ref.dtype)


def _rmsnorm(x, g):
    T, D = x.shape
    tm = _row_tile(T, 512)
    return pl.pallas_call(
        _rmsnorm_kernel,
        grid=(T // tm,),
        in_specs=[pl.BlockSpec((tm, D), lambda m: (m, 0)),
                  pl.BlockSpec((1, D), lambda m: (0, 0))],
        out_specs=pl.BlockSpec((tm, D), lambda m: (m, 0)),
        out_shape=jax.ShapeDtypeStruct((T, D), BF16),
        compiler_params=_params("arbitrary"),
        name="rmsnorm",
    )(x, g.reshape(1, D))


def _proj_kernel(u_ref, w_ref, cos_ref, sin_ref, o_ref, *, rope_lo, rope_hi):
    acc = jnp.dot(u_ref[...], w_ref[...], preferred_element_type=F32)
    n = pl.program_id(1)
    roped = jnp.logical_and(n >= rope_lo, n < rope_hi)

    @pl.when(roped)
    def _():
        cos = cos_ref[...]
        sin = sin_ref[...]
        for c in range(acc.shape[1] // HEAD_DIM):
            sl = slice(c * HEAD_DIM, (c + 1) * HEAD_DIM)
            o_ref[:, sl] = _rope(acc[:, sl], cos, sin).astype(o_ref.dtype)

    @pl.when(jnp.logical_not(roped))
    def _():
        o_ref[...] = acc.astype(o_ref.dtype)


def _rope(xh, cos, sin):
    return xh * cos + pltpu.roll(xh, HEAD_DIM // 2, 1) * sin


def _segment_block(segments, tn):
    assert all(s % tn == 0 and w % tn == 0 for s, w in segments)

    def block(n):
        idx, first = None, 0
        for start, width in segments:
            here = start // tn + n - first
            idx = here if idx is None else jnp.where(n >= first, here, idx)
            first += width // tn
        return idx

    return block


def _proj(u, w, segments, cos, sin, rope_cols, out_dtype):
    T, D = u.shape
    N = sum(width for _, width in segments)
    tm = _row_tile(T, 1024)
    tn = 2 * COL_CHUNK
    n_tab = cos.shape[0] // tm
    assert cos.shape[0] % tm == 0 and rope_cols[0] % tn == 0 and rope_cols[1] % tn == 0
    kern = functools.partial(_proj_kernel, rope_lo=rope_cols[0] // tn, rope_hi=rope_cols[1] // tn)
    wblock = _segment_block(segments, tn)
    return pl.pallas_call(
        kern,
        grid=(T // tm, N // tn),
        in_specs=[pl.BlockSpec((tm, D), lambda m, n: (m, 0)),
                  pl.BlockSpec((D, tn), lambda m, n: (0, wblock(n))),
                  pl.BlockSpec((tm, HEAD_DIM), lambda m, n: (m % n_tab, 0)),
                  pl.BlockSpec((tm, HEAD_DIM), lambda m, n: (m % n_tab, 0))],
        out_specs=pl.BlockSpec((tm, tn), lambda m, n: (m, n)),
        out_shape=jax.ShapeDtypeStruct((T, N), out_dtype),
        compiler_params=_params("arbitrary", "arbitrary"),
        name="in_proj",
    )(u, w, cos, sin)


def _kv_proj_kernel(u_ref, wk_ref, wv_ref, cos_ref, sin_ref, k_ref, v_ref):
    u = u_ref[...]
    tm = u.shape[0]
    cos = cos_ref[...]
    sin = sin_ref[...]
    k = jnp.dot(u, wk_ref[...], preferred_element_type=F32)
    for g in range(N_KV_HEADS):
        k_ref[pl.ds(g, tm, stride=N_KV_HEADS), :] = _rope(k[:, g * HEAD_DIM:(g + 1) * HEAD_DIM], cos, sin)
    v = jnp.dot(u, wv_ref[...], preferred_element_type=F32)
    for g in range(N_KV_HEADS):
        v_ref[pl.ds(g, tm, stride=N_KV_HEADS), :] = v[:, g * HEAD_DIM:(g + 1) * HEAD_DIM]


def _kv_proj(u, w, k_col, v_col, cos, sin):
    T, D = u.shape
    kvw = N_KV_HEADS * HEAD_DIM
    assert k_col % kvw == 0 and v_col % kvw == 0
    tm = _row_tile(T, 1024)
    n_tab = cos.shape[0] // tm
    assert cos.shape[0] % tm == 0
    out = jax.ShapeDtypeStruct((T * N_KV_HEADS, HEAD_DIM), F32)
    out_spec = pl.BlockSpec((tm * N_KV_HEADS, HEAD_DIM), lambda m: (m, 0))
    return pl.pallas_call(
        _kv_proj_kernel,
        grid=(T // tm,),
        in_specs=[pl.BlockSpec((tm, D), lambda m: (m, 0)),
                  pl.BlockSpec((D, kvw), lambda m: (0, k_col // kvw)),
                  pl.BlockSpec((D, kvw), lambda m: (0, v_col // kvw)),
                  pl.BlockSpec((tm, HEAD_DIM), lambda m: (m % n_tab, 0)),
                  pl.BlockSpec((tm, HEAD_DIM), lambda m: (m % n_tab, 0))],
        out_specs=[out_spec, out_spec],
        out_shape=[out, out],
        compiler_params=_params("arbitrary"),
        name="kv_proj",
    )(u, w, w, cos, sin)


def _topk_select(gate, blk, eligible, axis):
    gm = jnp.where(eligible, gate, NEG_INF)
    rank = jnp.zeros(gate.shape, jnp.int32)
    for r in range(gate.shape[axis]):
        gr = gm[r:r + 1, :] if axis == 0 else gm[:, r:r + 1]
        beats = jnp.logical_or(gr > gm, jnp.logical_and(gr == gm, blk > r))
        rank = rank + beats.astype(jnp.int32)
    return jnp.logical_and(rank < MOBA_TOPK, eligible)


def _attn_prompt_kernel(q_ref, k_ref, v_ref, o_ref,
                        kb_ref, vt_ref, kmean_ref, sel_ref, qt_ref, s_ref, m_ref, l_ref, acc_ref,
                        *, tq, nblk, scale):
    g = pl.program_id(1)
    i = pl.program_id(2)

    @pl.when(i == 0)
    def _():
        for j in range(nblk):
            rows = pl.ds(j * MOBA_BLOCK * N_KV_HEADS + g, MOBA_BLOCK, stride=N_KV_HEADS)
            kj = k_ref[rows, :]
            kb_ref[j] = kj.astype(BF16)
            kmean_ref[j:j + 1, :] = jnp.mean(kj, axis=0, keepdims=True)
            vt_ref[j] = v_ref[rows, :].T.astype(BF16)

    q = q_ref[...].astype(F32)
    qs = jnp.concatenate([q[:, h * HEAD_DIM:(h + 1) * HEAD_DIM] for h in range(KV_GROUP)], axis=0)
    qt_ref[...] = (qs * (scale * LOG2_E)).T.astype(BF16)

    gate = lax.dot_general(kmean_ref[...], qs, _NT, precision=lax.Precision.HIGHEST,
                           preferred_element_type=F32)
    blk = lax.broadcasted_iota(jnp.int32, gate.shape, 0)
    sel = _topk_select(gate, blk, blk < i, axis=0)
    sel_ref[...] = sel.astype(F32)

    def scores(j):
        return jnp.dot(kb_ref[j], qt_ref[...], preferred_element_type=F32)

    s = scores(i)
    key_t = lax.broadcasted_iota(jnp.int32, s.shape, 0)
    qry_t = lax.broadcasted_iota(jnp.int32, s.shape, 1) % tq
    s = jnp.where(key_t <= qry_t, s, NEG_INF)
    m0 = jnp.max(s, axis=0, keepdims=True)
    p = jnp.exp2(s - m0)
    m_ref[...] = m0
    l_ref[...] = jnp.sum(p, axis=0, keepdims=True)
    acc_ref[...] = jnp.dot(vt_ref[i], p.astype(BF16), preferred_element_type=F32)

    def absorb(slot, j):
        s = s_ref[slot]
        picked = sel_ref[pl.ds(j, 1), :] > 0.5
        m_old = m_ref[...]
        m_new = jnp.maximum(m_old, jnp.where(picked, jnp.max(s, axis=0, keepdims=True), NEG_INF))
        alpha = jnp.exp2(m_old - m_new)
        p = jnp.exp2(s - jnp.where(picked, m_new, -NEG_INF))
        m_ref[...] = m_new
        l_ref[...] = alpha * l_ref[...] + jnp.sum(p, axis=0, keepdims=True)
        acc_ref[...] = alpha * acc_ref[...] + jnp.dot(vt_ref[j], p.astype(BF16),
                                                      preferred_element_type=F32)

    s_ref[0] = scores(0)

    def body(t, carry):
        ja = 2 * t
        s_ref[1] = scores(ja + 1)
        absorb(0, ja)
        s_ref[0] = scores(jnp.minimum(ja + 2, nblk - 1))
        absorb(1, ja + 1)
        return carry

    lax.fori_loop(0, (i + 1) // 2, body, 0)

    out = acc_ref[...] * (1.0 / l_ref[...])
    for h in range(KV_GROUP):
        o_ref[:, h * HEAD_DIM:(h + 1) * HEAD_DIM] = out[:, h * tq:(h + 1) * tq].T.astype(o_ref.dtype)


def _attn_prompt(qarr, q_col, k4, v4, B, S):
    tq = MOBA_BLOCK
    assert S % MOBA_BLOCK == 0
    nblk = S // MOBA_BLOCK
    nq = S // tq
    gw = KV_GROUP * HEAD_DIM
    n = KV_GROUP * tq
    kern = functools.partial(_attn_prompt_kernel, tq=tq, nblk=nblk, scale=HEAD_DIM ** -0.5)
    return pl.pallas_call(
        kern,
        grid=(B, N_KV_HEADS, nq),
        in_specs=[pl.BlockSpec((tq, gw), lambda b, g, i: (b * nq + i, q_col // gw + g)),
                  pl.BlockSpec((S * N_KV_HEADS, HEAD_DIM), lambda b, g, i: (b, 0)),
                  pl.BlockSpec((S * N_KV_HEADS, HEAD_DIM), lambda b, g, i: (b, 0))],
        out_specs=pl.BlockSpec((tq, gw), lambda b, g, i: (b * nq + i, g)),
        out_shape=jax.ShapeDtypeStruct((B * S, N_HEADS * HEAD_DIM), BF16),
        scratch_shapes=[pltpu.VMEM((nblk, MOBA_BLOCK, HEAD_DIM), BF16),
                        pltpu.VMEM((nblk, HEAD_DIM, MOBA_BLOCK), BF16),
                        pltpu.VMEM((nblk, HEAD_DIM), F32),
                        pltpu.VMEM((nblk, n), F32),
                        pltpu.VMEM((HEAD_DIM, n), BF16),
                        pltpu.VMEM((2, MOBA_BLOCK, n), F32),
                        pltpu.VMEM((1, n), F32),
                        pltpu.VMEM((1, n), F32),
                        pltpu.VMEM((HEAD_DIM, n), F32)],
        compiler_params=_params("arbitrary", "arbitrary", "arbitrary"),
        name="moba_prompt",
    )(qarr, k4, v4)


def _attn_decode_kernel(pt_ref, q_ref, kn_ref, vn_ref, *refs, pps, ns, nblk, ds, nseq, scale):
    del pt_ref
    k_refs = refs[:pps]
    v_refs = refs[pps:2 * pps]
    o_ref = refs[2 * pps]
    wf_ref, wb_ref, s_ref, p_ref, kmean_ref, l_ref, acc_ref = refs[2 * pps + 1:]
    r = pl.program_id(0)
    step = pl.program_id(1)
    bps = pps // 2
    kvw = N_KV_HEADS * HEAD_DIM
    nrow = N_HEADS * ds
    has_values = r >= 1
    has_keys = r < nseq

    def by_head(ref):
        tokens = ref.shape[0] // N_KV_HEADS
        return jnp.concatenate([ref[pl.ds(g, tokens, stride=N_KV_HEADS), :]
                                for g in range(N_KV_HEADS)], axis=1)

    def block_of(page_refs, jj):
        return jnp.concatenate([by_head(page_refs[2 * jj]), by_head(page_refs[2 * jj + 1])], axis=0)

    @pl.when(has_values)
    def _():
        for jj in range(bps):
            j = step * bps + jj
            vblk = block_of(v_refs, jj).astype(BF16)
            acc_ref[...] += jnp.dot(p_ref[j], vblk, preferred_element_type=F32)

    @pl.when(jnp.logical_and(has_values, step == ns - 1))
    def _():
        acc = acc_ref[...] * (1.0 / l_ref[...])
        outs = []
        for h in range(N_HEADS):
            g = h // KV_GROUP
            outs.append(acc[h * ds:(h + 1) * ds, g * HEAD_DIM:(g + 1) * HEAD_DIM])
        o_ref[...] = jnp.concatenate(outs, axis=1)

    @pl.when(jnp.logical_and(has_keys, step == 0))
    def _():
        q = q_ref[...]
        zero = jnp.zeros((ds, HEAD_DIM), F32)
        rows = []
        for h in range(N_HEADS):
            pieces = [zero] * N_KV_HEADS
            pieces[h // KV_GROUP] = q[:, h * HEAD_DIM:(h + 1) * HEAD_DIM]
            rows.append(jnp.concatenate(pieces, axis=1))
        wf = jnp.concatenate(rows, axis=0)
        wf_ref[...] = wf
        wb_ref[...] = (wf * scale).astype(BF16)

    @pl.when(has_keys)
    def _():
        for jj in range(bps):
            j = step * bps + jj
            kblk = block_of(k_refs, jj)
            kmean_ref[pl.ds(j, 1), :] = jnp.mean(kblk, axis=0, keepdims=True)
            s_ref[j] = lax.dot_general(wb_ref[...], kblk.astype(BF16), _NT,
                                       preferred_element_type=F32)

    @pl.when(jnp.logical_and(has_keys, step == ns - 1))
    def _():
        gate = lax.dot_general(kmean_ref[...], wf_ref[...], _NT, precision=lax.Precision.HIGHEST,
                               preferred_element_type=F32)
        blk = lax.broadcasted_iota(jnp.int32, gate.shape, 0)
        sel = _topk_select(gate, blk, blk < nblk, axis=0)
        bias = jnp.concatenate([jnp.where(sel, 0.0, NEG_INF),
                                jnp.zeros((nrow - nblk, nrow), F32)], axis=0).T

        pad = jnp.zeros((nrow - ds, kvw), F32)
        knp = jnp.concatenate([by_head(kn_ref), pad], axis=0).astype(BF16)
        sn = lax.dot_general(wb_ref[...], knp, _NT, preferred_element_type=F32)
        qry_t = lax.broadcasted_iota(jnp.int32, sn.shape, 0) % ds
        key_t = lax.broadcasted_iota(jnp.int32, sn.shape, 1)
        sn = jnp.where(key_t <= qry_t, sn, NEG_INF)

        mx = jnp.full((nrow, MOBA_BLOCK), NEG_INF, F32)
        for j in range(nblk):
            sb = s_ref[j] + bias[:, j:j + 1]
            s_ref[j] = sb
            mx = jnp.maximum(mx, sb)
        m = jnp.maximum(jnp.max(mx, axis=1, keepdims=True), jnp.max(sn, axis=1, keepdims=True))
        lsum = jnp.zeros((nrow, MOBA_BLOCK), F32)
        for j in range(nblk):
            p = jnp.exp(s_ref[j] - m)
            lsum = lsum + p
            p_ref[j] = p.astype(BF16)
        pn = jnp.exp(sn - m)
        l_ref[...] = jnp.sum(lsum, axis=1, keepdims=True) + jnp.sum(pn, axis=1, keepdims=True)
        vnp = jnp.concatenate([by_head(vn_ref), pad], axis=0).astype(BF16)
        acc_ref[...] = jnp.dot(pn.astype(BF16), vnp, preferred_element_type=F32)


def _attn_decode(qarr, q_col, k4, v4, cache_k, cache_v, layer, page_table, DB, DS):
    depth, n_phys, page, nkv, hd = cache_k.shape
    n_pages = page_table.shape[1]
    past_len = n_pages * page
    kvw = nkv * hd
    assert (nkv, hd) == (N_KV_HEADS, HEAD_DIM) and MOBA_BLOCK == 2 * page
    assert past_len % MOBA_BLOCK == 0 and DS == SUBLANES
    nblk = past_len // MOBA_BLOCK
    pps = 16
    assert n_pages % pps == 0
    ns = n_pages // pps
    nrow = N_HEADS * DS
    ck = cache_k.reshape(depth * n_phys, page * nkv, hd)
    cv = cache_v.reshape(depth * n_phys, page * nkv, hd)
    base = layer * n_phys

    kseq = lambda r: jnp.minimum(r, DB - 1)
    vseq = lambda r: jnp.maximum(r - 1, 0)

    def k_map(i):
        return lambda r, s, pt: (base + pt[kseq(r), s * pps + i], 0, 0)

    def v_map(i):
        return lambda r, s, pt: (base + pt[vseq(r), s * pps + i], 0, 0)

    qw = N_HEADS * HEAD_DIM
    in_specs = [pl.BlockSpec((DS, qw), lambda r, s, pt: (kseq(r), q_col // qw)),
                pl.BlockSpec((DS * nkv, hd), lambda r, s, pt: (kseq(r), 0)),
                pl.BlockSpec((DS * nkv, hd), lambda r, s, pt: (kseq(r), 0))]
    in_specs += [pl.BlockSpec((None, page * nkv, hd), k_map(i)) for i in range(pps)]
    in_specs += [pl.BlockSpec((None, page * nkv, hd), v_map(i)) for i in range(pps)]
    kern = functools.partial(_attn_decode_kernel, pps=pps, ns=ns, nblk=nblk, ds=DS, nseq=DB,
                             scale=HEAD_DIM ** -0.5)
    grid_spec = pltpu.PrefetchScalarGridSpec(
        num_scalar_prefetch=1,
        grid=(DB + 1, ns),
        in_specs=in_specs,
        out_specs=pl.BlockSpec((DS, qw), lambda r, s, pt: (vseq(r), 0)),
        scratch_shapes=[pltpu.VMEM((nrow, kvw), F32),
                        pltpu.VMEM((nrow, kvw), BF16),
                        pltpu.VMEM((nblk, nrow, MOBA_BLOCK), F32),
                        pltpu.VMEM((nblk, nrow, MOBA_BLOCK), BF16),
                        pltpu.VMEM((nblk, kvw), F32),
                        pltpu.VMEM((nrow, 1), F32),
                        pltpu.VMEM((nrow, kvw), F32)])
    return pl.pallas_call(
        kern,
        grid_spec=grid_spec,
        out_shape=jax.ShapeDtypeStruct((DB * DS, qw), F32),
        compiler_params=_params("arbitrary", "arbitrary"),
        name="moba_decode",
    )(page_table, qarr, k4, v4, *([ck] * pps), *([cv] * pps))


def _rglru_kernel(xr_ref, yg_ref, h0_ref, cb_ref, cw_ref, cbias_ref, wa_ref, ba_ref, wx_ref, bx_ref,
                  lam_ref, rnn_ref, hlast_ref, cnew_ref, halo_ref, hcarry_ref, *, L):
    c = pl.program_id(1)
    C = xr_ref.shape[1]

    @pl.when(c == 0)
    def _():
        hcarry_ref[...] = h0_ref[...]
        halo_ref[...] = jnp.concatenate(
            [jnp.zeros((SUBLANES - (CONV_W - 1), C), F32), cb_ref[...]], axis=0)

    G = L // SUBLANES
    xr = xr_ref[...]
    xg = jnp.concatenate([halo_ref[...], xr], axis=0).reshape(G + 1, SUBLANES, C)
    rin = lax.broadcasted_iota(jnp.int32, (1, SUBLANES, 1), 1)
    cw = cw_ref[...]
    xc = cbias_ref[...].reshape(1, 1, C)
    for j in range(CONV_W):
        k = CONV_W - 1 - j
        if k == 0:
            xk = xg[1:]
        else:
            rolled = pltpu.roll(xg, k, 1)
            xk = jnp.where(rin >= k, rolled[1:], rolled[:-1])
        xc = xc + xk * cw[j:j + 1, :].reshape(1, 1, C)
    xc = xc.reshape(L, C)

    xcb = xc.astype(BF16)
    bw = C // RNN_BLOCKS
    r_parts, i_parts = [], []
    for n in range(RNN_BLOCKS):
        xs = xcb[:, n * bw:(n + 1) * bw]
        r_parts.append(jnp.dot(xs, wa_ref[n], preferred_element_type=F32))
        i_parts.append(jnp.dot(xs, wx_ref[n], preferred_element_type=F32))
    r = jax.nn.sigmoid(jnp.concatenate(r_parts, axis=1) + ba_ref[...])
    ig = jax.nn.sigmoid(jnp.concatenate(i_parts, axis=1) + bx_ref[...])

    nl = -lam_ref[...]
    softplus = jnp.maximum(nl, 0.0) + jnp.log1p(jnp.exp(-jnp.abs(nl)))
    log_a = -LRU_C * r * softplus
    a = jnp.exp(log_a)
    gain2 = -jnp.tanh(log_a) * (a * a + 1.0)
    gain = jnp.where(gain2 > 0.0, gain2 * lax.rsqrt(gain2), 0.0)
    bv = gain * (ig * xc)

    a = a.reshape(G, SUBLANES, C)
    bv = bv.reshape(G, SUBLANES, C)
    d = 1
    while d < SUBLANES:
        same_group = rin >= d
        bv = a * jnp.where(same_group, pltpu.roll(bv, d, 1), 0.0) + bv
        a = a * jnp.where(same_group, pltpu.roll(a, d, 1), 1.0)
        d *= 2
    carry = hcarry_ref[...]
    groups = []
    for g in range(G):
        hg = bv[g] + a[g] * carry
        groups.append(hg)
        carry = hg[SUBLANES - 1:SUBLANES, :]
    h = groups[0] if G == 1 else jnp.concatenate(groups, axis=0)
    hcarry_ref[...] = carry
    hlast_ref[...] = carry

    yg = yg_ref[...].astype(F32)
    cdf = 0.5 * (1.0 + jnp.tanh(0.7978845608028654 * (yg + 0.044715 * (yg * yg * yg))))
    rnn_ref[...] = (h * (yg * cdf)).astype(rnn_ref.dtype)

    halo_ref[...] = xr[L - SUBLANES:L, :]
    cnew_ref[...] = xr[L - (CONV_W - 1):L, :]


def _rglru(xr_arr, xr_col, yg_arr, yg_col, B, S, h0, conv_buf, conv_w, conv_b, wa, ba, wx, bx, lam):
    C = h0.shape[-1]
    L = min(S, 256)
    assert S % L == 0 and L % SUBLANES == 0 and xr_col % C == 0 and yg_col % C == 0
    nc = S // L
    out_dtype = BF16 if L % 16 == 0 else F32
    row = lambda v: v.reshape(1, C)
    const2 = lambda b, c: (0, 0)
    const3 = lambda b, c: (0, 0, 0)
    rnn, hlast, cnew = pl.pallas_call(
        functools.partial(_rglru_kernel, L=L),
        grid=(B, nc),
        in_specs=[pl.BlockSpec((L, C), lambda b, c: (b * nc + c, xr_col // C)),
                  pl.BlockSpec((L, C), lambda b, c: (b * nc + c, yg_col // C)),
                  pl.BlockSpec((None, 1, C), lambda b, c: (b, 0, 0)),
                  pl.BlockSpec((None, CONV_W - 1, C), lambda b, c: (b, 0, 0)),
                  pl.BlockSpec((CONV_W, C), const2),
                  pl.BlockSpec((1, C), const2),
                  pl.BlockSpec((RNN_BLOCKS, C // RNN_BLOCKS, C // RNN_BLOCKS), const3),
                  pl.BlockSpec((1, C), const2),
                  pl.BlockSpec((RNN_BLOCKS, C // RNN_BLOCKS, C // RNN_BLOCKS), const3),
                  pl.BlockSpec((1, C), const2),
                  pl.BlockSpec((1, C), const2)],
        out_specs=[pl.BlockSpec((L, C), lambda b, c: (b * nc + c, 0)),
                   pl.BlockSpec((None, 1, C), lambda b, c: (b, 0, 0)),
                   pl.BlockSpec((None, CONV_W - 1, C), lambda b, c: (b, 0, 0))],
        out_shape=[jax.ShapeDtypeStruct((B * S, C), out_dtype),
                   jax.ShapeDtypeStruct((B, 1, C), F32),
                   jax.ShapeDtypeStruct((B, CONV_W - 1, C), F32)],
        scratch_shapes=[pltpu.VMEM((SUBLANES, C), F32),
                        pltpu.VMEM((1, C), F32)],
        compiler_params=_params("arbitrary", "arbitrary"),
        name="rglru",
    )(xr_arr, yg_arr, h0.reshape(B, 1, C), conv_buf, conv_w, row(conv_b), wa, row(ba), wx, row(bx), row(lam))
    return rnn, hlast.reshape(B, C), cnew


def _merge_kernel(rnn_ref, attn_ref, u_ref, wa_ref, wb_ref, wga_ref, wgb_ref, bga_ref, bgb_ref, o_ref):
    u = u_ref[...]
    a = jnp.dot(rnn_ref[...].astype(BF16), wa_ref[...], preferred_element_type=F32)
    b = jnp.dot(attn_ref[...].astype(BF16), wb_ref[...], preferred_element_type=F32)
    ga = jax.nn.sigmoid(jnp.dot(u, wga_ref[...], preferred_element_type=F32) + bga_ref[...])
    gb = jax.nn.sigmoid(jnp.dot(u, wgb_ref[...], preferred_element_type=F32) + bgb_ref[...])
    o_ref[...] = (ga * a + gb * b).astype(o_ref.dtype)


def _merge(rnn, attn, u, wa, wb, w_in, gate_col, b_gate):
    T, D = u.shape
    tm = _row_tile(T, 1024)
    tn = COL_CHUNK
    nb = D // tn
    assert gate_col % tn == 0
    act = lambda m, n: (m, 0)
    col = lambda m, n: (0, n)
    col_hi = lambda m, n: (0, n + nb)
    gcol = lambda m, n: (0, gate_col // tn + n)
    gcol_hi = lambda m, n: (0, gate_col // tn + nb + n)
    bg = b_gate.reshape(1, 2 * D)
    return pl.pallas_call(
        _merge_kernel,
        grid=(T // tm, nb),
        in_specs=[pl.BlockSpec((tm, D), act), pl.BlockSpec((tm, D), act), pl.BlockSpec((tm, D), act),
                  pl.BlockSpec((D, tn), col), pl.BlockSpec((D, tn), col),
                  pl.BlockSpec((D, tn), gcol), pl.BlockSpec((D, tn), gcol_hi),
                  pl.BlockSpec((1, tn), col), pl.BlockSpec((1, tn), col_hi)],
        out_specs=pl.BlockSpec((tm, tn), lambda m, n: (m, n)),
        out_shape=jax.ShapeDtypeStruct((T, D), BF16),
        compiler_params=_params("arbitrary", "arbitrary"),
        name="merge",
    )(rnn, attn, u, wa, wb, w_in, w_in, bg, bg)


def _outproj_kernel(mg_ref, x_ref, w_ref, g_ref, x2_ref, u2_ref):
    mg = mg_ref[...]
    D = x_ref.shape[1]
    ssq = jnp.zeros((x_ref.shape[0], 1), F32)
    for c in range(D // COL_CHUNK):
        sl = slice(c * COL_CHUNK, (c + 1) * COL_CHUNK)
        y = x_ref[:, sl] + jnp.dot(mg, w_ref[:, sl], preferred_element_type=F32)
        x2_ref[:, sl] = y
        ssq = ssq + jnp.sum(y * y, axis=-1, keepdims=True)
    inv = lax.rsqrt(ssq / D + RMS_EPS)
    u2_ref[...] = (x2_ref[...] * inv * g_ref[...]).astype(u2_ref.dtype)


def _outproj(merged, x, w_out, g):
    T, D = x.shape
    tm = _row_tile(T, 512)
    row = lambda m: (m, 0)
    return pl.pallas_call(
        _outproj_kernel,
        grid=(T // tm,),
        in_specs=[pl.BlockSpec((tm, D), row), pl.BlockSpec((tm, D), row),
                  pl.BlockSpec((D, D), lambda m: (0, 0)), pl.BlockSpec((1, D), lambda m: (0, 0))],
        out_specs=[pl.BlockSpec((tm, D), row), pl.BlockSpec((tm, D), row)],
        out_shape=[jax.ShapeDtypeStruct((T, D), F32), jax.ShapeDtypeStruct((T, D), BF16)],
        compiler_params=_params("arbitrary"),
        name="out_proj",
    )(merged, x, w_out, g.reshape(1, D))


def _ffn_kernel(u2_ref, x2_ref, w1_ref, w2_ref, gf_ref, y_ref, *, nf, final_norm):
    f = pl.program_id(1)

    @pl.when(f == 0)
    def _():
        y_ref[...] = x2_ref[...]

    h = jnp.dot(u2_ref[...], w1_ref[...], preferred_element_type=F32)
    h = jnp.square(jnp.maximum(h, 0.0)).astype(BF16)
    for c in range(y_ref.shape[1] // COL_CHUNK):
        sl = slice(c * COL_CHUNK, (c + 1) * COL_CHUNK)
        y_ref[:, sl] += jnp.dot(h, w2_ref[:, sl], preferred_element_type=F32)

    if final_norm:
        @pl.when(f == nf - 1)
        def _():
            x3 = y_ref[...]
            ms = jnp.mean(x3 * x3, axis=-1, keepdims=True)
            y_ref[...] = x3 * lax.rsqrt(ms + RMS_EPS) * gf_ref[...]


def _ffn(u2, x2, w1, w2, g_final, final_norm):
    T, D = x2.shape
    F = w1.shape[1]
    tm = _row_tile(T, 1024)
    tf = 512
    nf = F // tf
    row = lambda m, f: (m, 0)
    return pl.pallas_call(
        functools.partial(_ffn_kernel, nf=nf, final_norm=final_norm),
        grid=(T // tm, nf),
        in_specs=[pl.BlockSpec((tm, D), row), pl.BlockSpec((tm, D), row),
                  pl.BlockSpec((D, tf), lambda m, f: (0, f)), pl.BlockSpec((tf, D), lambda m, f: (f, 0)),
                  pl.BlockSpec((1, D), lambda m, f: (0, 0))],
        out_specs=pl.BlockSpec((tm, D), row),
        out_shape=jax.ShapeDtypeStruct((T, D), F32),
        compiler_params=_params("arbitrary", "arbitrary"),
        name="ffn",
    )(u2, x2, w1, w2, g_final.reshape(1, D))


def _rope_tables(pos):
    half = HEAD_DIM // 2
    inv = ROPE_THETA ** (-jnp.arange(half, dtype=F32) * (2.0 / HEAD_DIM))
    ang = pos.astype(F32)[:, None] * inv[None, :]
    cos, sin = jnp.cos(ang), jnp.sin(ang)
    return jnp.concatenate([cos, cos], axis=-1), jnp.concatenate([-sin, sin], axis=-1)


def _decoder_layer(x, B, S, cos, sin, past, h0, conv_buf, w, final_norm):
    C = h0.shape[-1]
    attn_w = N_HEADS * HEAD_DIM
    act_dtype = BF16 if S % 16 == 0 else F32

    kv_w = N_KV_HEADS * HEAD_DIM
    q_col, k_col, v_col = 0, attn_w, attn_w + kv_w
    xr_col = v_col + kv_w
    yg_col = xr_col + C
    gate_col = yg_col + C
    w_in = w["w_in"]

    u = _rmsnorm(x, w["norm_mix"])
    xr = _proj(u, w_in, [(xr_col, C)], cos, sin, (0, 0), F32)
    ygq = _proj(u, w_in, [(yg_col, C), (q_col, attn_w)], cos, sin,
                (C, C + attn_w), act_dtype)
    k4, v4 = _kv_proj(u, w_in, k_col, v_col, cos, sin)
    if past is None:
        attn = _attn_prompt(ygq, C, k4, v4, B, S)
    else:
        attn = _attn_decode(ygq, C, k4, v4, past[0], past[1], past[2], past[3], B, S)
    rnn, h_last, conv_new = _rglru(xr, 0, ygq, 0, B, S, h0, conv_buf, w["conv_w"], w["conv_b"],
                                   w["w_rg_a"], w["b_rg_a"], w["w_rg_x"], w["b_rg_x"], w["lru_lambda"])
    merged = _merge(rnn, attn, u, w["w_proj_a"], w["w_proj_b"], w_in, gate_col, w["b_gate"])
    x2, u2 = _outproj(merged, x, w["w_out"], w["norm_mlp"])
    y = _ffn(u2, x2, w["w_ff1"], w["w_ff2"], w["norm_final"], final_norm)
    k = k4.reshape(B, S, N_KV_HEADS, HEAD_DIM)
    v = v4.reshape(B, S, N_KV_HEADS, HEAD_DIM)
    return y, k, v, h_last, conv_new


def kernel(x_prompt, x_sample, cache_k, cache_v, state_h, state_conv, page_table, norm_mix, w_in, b_gate, conv_w, conv_b, w_rg_a, b_rg_a, w_rg_x, b_rg_x, lru_lambda, w_proj_a, w_proj_b, w_out, norm_mlp, w_ff1, w_ff2, norm_final):
    B, S, D = x_prompt.shape
    DB, DS, _ = x_sample.shape
    depth = w_in.shape[0]
    C = state_h.shape[-1]
    past_len = page_table.shape[1] * cache_k.shape[2]

    cos_p, sin_p = _rope_tables(jnp.arange(S, dtype=jnp.int32))
    cos_s, sin_s = _rope_tables(past_len + jnp.arange(DS, dtype=jnp.int32))
    cos_s, sin_s = jnp.tile(cos_s, (DB, 1)), jnp.tile(sin_s, (DB, 1))

    hp = x_prompt.reshape(B * S, D)
    hs = x_sample.reshape(DB * DS, D)
    outs = [[] for _ in range(8)]
    for l in range(depth):
        wl = w_in[l]
        w = dict(
            norm_mix=norm_mix[l], norm_mlp=norm_mlp[l], norm_final=norm_final,
            w_in=wl.astype(BF16), b_gate=b_gate[l],
            conv_w=conv_w[l], conv_b=conv_b[l],
            w_rg_a=w_rg_a[l].astype(BF16), b_rg_a=b_rg_a[l],
            w_rg_x=w_rg_x[l].astype(BF16), b_rg_x=b_rg_x[l], lru_lambda=lru_lambda[l],
            w_proj_a=w_proj_a[l].astype(BF16), w_proj_b=w_proj_b[l].astype(BF16),
            w_out=w_out[l].astype(BF16), w_ff1=w_ff1[l].astype(BF16), w_ff2=w_ff2[l].astype(BF16))
        final = l == depth - 1
        h0_p = jnp.zeros((B, C), F32)
        buf_p = jnp.zeros((B, CONV_W - 1, C), F32)
        hp, k1, v1, r1, c1 = _decoder_layer(hp, B, S, cos_p, sin_p, None, h0_p, buf_p, w, final)
        past = (cache_k, cache_v, l, page_table)
        hs, k2, v2, r2, c2 = _decoder_layer(hs, DB, DS, cos_s, sin_s, past, state_h[l], state_conv[l], w, final)
        for lst, val in zip(outs, (k1, v1, r1, c1, k2, v2, r2, c2)):
            lst.append(val)
    return (hp.reshape(B, S, D), hs.reshape(DB, DS, D), *[jnp.stack(o) for o in outs])
```

```python
import functools

import jax
import jax.numpy as jnp
from jax import lax
from jax.experimental import pallas as pl
from jax.experimental.pallas import tpu as pltpu

F32 = jnp.float32
BF16 = jnp.bfloat16

N_HEADS = 16
N_KV_HEADS = 4
HEAD_DIM = 128
KV_GROUP = N_HEADS // N_KV_HEADS
ROPE_THETA = 10000.0
MOBA_BLOCK = 256
MOBA_TOPK = 3
RNN_BLOCKS = 16
CONV_W = 4
LRU_C = 8.0
RMS_EPS = 1e-6
NEG_INF = -1e30
LOG2_E = 1.4426950408889634

LANES = 128
SUBLANES = 8
VMEM_LIMIT_BYTES = 56 * 1024 * 1024
COL_CHUNK = 512

_NT = (((1,), (1,)), ((), ()))


def _params(*sem):
    return pltpu.CompilerParams(dimension_semantics=sem, vmem_limit_bytes=VMEM_LIMIT_BYTES)


def _row_tile(rows, pref):
    t = min(rows, pref)
    assert rows % t == 0, (rows, pref)
    return t


def _proj_kernel(u_ref, w_ref, cos_ref, sin_ref, o_ref, *, rope_lo, rope_hi):
    acc = jnp.dot(u_ref[...], w_ref[...], preferred_element_type=F32)
    n = pl.program_id(1)
    roped = jnp.logical_and(n >= rope_lo, n < rope_hi)

    @pl.when(roped)
    def _():
        cos = cos_ref[...]
        sin = sin_ref[...]
        for c in range(acc.shape[1] // HEAD_DIM):
            sl = slice(c * HEAD_DIM, (c + 1) * HEAD_DIM)
            o_ref[:, sl] = _rope(acc[:, sl], cos, sin).astype(o_ref.dtype)

    @pl.when(jnp.logical_not(roped))
    def _():
        o_ref[...] = acc.astype(o_ref.dtype)


def _rope(xh, cos, sin):
    return xh * cos + pltpu.roll(xh, HEAD_DIM // 2, 1) * sin


def _segment_block(segments, tn):
    assert all(s % tn == 0 and w % tn == 0 for s, w in segments)

    def block(n):
        idx, first = None, 0
        for start, width in segments:
            here = start // tn + n - first
            idx = here if idx is None else jnp.where(n >= first, here, idx)
            first += width // tn
        return idx

    return block


def _proj(u, w, segments, cos, sin, rope_cols, out_dtype):
    T, D = u.shape
    N = sum(width for _, width in segments)
    tm = _row_tile(T, 1024)
    tn = 2 * COL_CHUNK
    n_tab = cos.shape[0] // tm
    assert cos.shape[0] % tm == 0 and rope_cols[0] % tn == 0 and rope_cols[1] % tn == 0
    kern = functools.partial(_proj_kernel, rope_lo=rope_cols[0] // tn, rope_hi=rope_cols[1] // tn)
    wblock = _segment_block(segments, tn)
    return pl.pallas_call(
        kern,
        grid=(T // tm, N // tn),
        in_specs=[pl.BlockSpec((tm, D), lambda m, n: (m, 0)),
                  pl.BlockSpec((D, tn), lambda m, n: (0, wblock(n))),
                  pl.BlockSpec((tm, HEAD_DIM), lambda m, n: (m % n_tab, 0)),
                  pl.BlockSpec((tm, HEAD_DIM), lambda m, n: (m % n_tab, 0))],
        out_specs=pl.BlockSpec((tm, tn), lambda m, n: (m, n)),
        out_shape=jax.ShapeDtypeStruct((T, N), out_dtype),
        compiler_params=_params("arbitrary", "arbitrary"),
        name="in_proj",
    )(u, w, cos, sin)


def _norm_proj_kernel(x_ref, g_ref, w_ref, u_ref, o_ref):
    @pl.when(pl.program_id(1) == 0)
    def _():
        x = x_ref[...]
        ms = jnp.mean(x * x, axis=-1, keepdims=True)
        u_ref[...] = (x * lax.rsqrt(ms + RMS_EPS) * g_ref[...]).astype(u_ref.dtype)

    o_ref[...] = jnp.dot(u_ref[...], w_ref[...], preferred_element_type=F32).astype(o_ref.dtype)


def _norm_proj(x, g, w, col, width, out_dtype):
    T, D = x.shape
    tm = _row_tile(T, 1024)
    tn = 2 * COL_CHUNK
    assert col % tn == 0 and width % tn == 0
    return pl.pallas_call(
        _norm_proj_kernel,
        grid=(T // tm, width // tn),
        in_specs=[pl.BlockSpec((tm, D), lambda m, n: (m, 0)),
                  pl.BlockSpec((1, D), lambda m, n: (0, 0)),
                  pl.BlockSpec((D, tn), lambda m, n: (0, col // tn + n))],
        out_specs=[pl.BlockSpec((tm, D), lambda m, n: (m, 0)),
                   pl.BlockSpec((tm, tn), lambda m, n: (m, n))],
        out_shape=[jax.ShapeDtypeStruct((T, D), BF16), jax.ShapeDtypeStruct((T, width), out_dtype)],
        compiler_params=_params("arbitrary", "arbitrary"),
        name="norm_proj",
    )(x, g.reshape(1, D), w)


def _kv_proj_kernel(u_ref, wk_ref, wv_ref, cos_ref, sin_ref, k_ref, v_ref):
    u = u_ref[...]
    tm = u.shape[0]
    cos = cos_ref[...]
    sin = sin_ref[...]
    k = jnp.dot(u, wk_ref[...], preferred_element_type=F32)
    for g in range(N_KV_HEADS):
        k_ref[pl.ds(g, tm, stride=N_KV_HEADS), :] = _rope(k[:, g * HEAD_DIM:(g + 1) * HEAD_DIM], cos, sin)
    v = jnp.dot(u, wv_ref[...], preferred_element_type=F32)
    for g in range(N_KV_HEADS):
        v_ref[pl.ds(g, tm, stride=N_KV_HEADS), :] = v[:, g * HEAD_DIM:(g + 1) * HEAD_DIM]


def _kv_proj(u, w, k_col, v_col, cos, sin):
    T, D = u.shape
    kvw = N_KV_HEADS * HEAD_DIM
    assert k_col % kvw == 0 and v_col % kvw == 0
    tm = _row_tile(T, 1024)
    n_tab = cos.shape[0] // tm
    assert cos.shape[0] % tm == 0
    out = jax.ShapeDtypeStruct((T * N_KV_HEADS, HEAD_DIM), F32)
    out_spec = pl.BlockSpec((tm * N_KV_HEADS, HEAD_DIM), lambda m: (m, 0))
    return pl.pallas_call(
        _kv_proj_kernel,
        grid=(T // tm,),
        in_specs=[pl.BlockSpec((tm, D), lambda m: (m, 0)),
                  pl.BlockSpec((D, kvw), lambda m: (0, k_col // kvw)),
                  pl.BlockSpec((D, kvw), lambda m: (0, v_col // kvw)),
                  pl.BlockSpec((tm, HEAD_DIM), lambda m: (m % n_tab, 0)),
                  pl.BlockSpec((tm, HEAD_DIM), lambda m: (m % n_tab, 0))],
        out_specs=[out_spec, out_spec],
        out_shape=[out, out],
        compiler_params=_params("arbitrary"),
        name="kv_proj",
    )(u, w, w, cos, sin)


def _topk_select(gate, blk, eligible, axis):
    gm = jnp.where(eligible, gate, NEG_INF)
    rank = jnp.zeros(gate.shape, jnp.int32)
    for r in range(gate.shape[axis]):
        gr = gm[r:r + 1, :] if axis == 0 else gm[:, r:r + 1]
        beats = jnp.logical_or(gr > gm, jnp.logical_and(gr == gm, blk > r))
        rank = rank + beats.astype(jnp.int32)
    return jnp.logical_and(rank < MOBA_TOPK, eligible)


def _attn_prompt_kernel(q_ref, k_ref, v_ref, o_ref,
                        kb_ref, vt_ref, kmean_ref, sel_ref, qt_ref, s_ref, m_ref, l_ref, acc_ref,
                        *, tq, nblk, scale):
    g = pl.program_id(1)
    i = pl.program_id(2)

    @pl.when(i == 0)
    def _():
        for j in range(nblk):
            rows = pl.ds(j * MOBA_BLOCK * N_KV_HEADS + g, MOBA_BLOCK, stride=N_KV_HEADS)
            kj = k_ref[rows, :]
            kb_ref[j] = kj.astype(BF16)
            kmean_ref[j:j + 1, :] = jnp.mean(kj, axis=0, keepdims=True)
            vt_ref[j] = v_ref[rows, :].T.astype(BF16)

    q = q_ref[...].astype(F32)
    qs = jnp.concatenate([q[:, h * HEAD_DIM:(h + 1) * HEAD_DIM] for h in range(KV_GROUP)], axis=0)
    qt_ref[...] = (qs * (scale * LOG2_E)).T.astype(BF16)

    gate = lax.dot_general(kmean_ref[...], qs, _NT, precision=lax.Precision.HIGHEST,
                           preferred_element_type=F32)
    blk = lax.broadcasted_iota(jnp.int32, gate.shape, 0)
    sel = _topk_select(gate, blk, blk < i, axis=0)
    sel_ref[...] = sel.astype(F32)

    def scores(j):
        return jnp.dot(kb_ref[j], qt_ref[...], preferred_element_type=F32)

    s = scores(i)
    key_t = lax.broadcasted_iota(jnp.int32, s.shape, 0)
    qry_t = lax.broadcasted_iota(jnp.int32, s.shape, 1) % tq
    s = jnp.where(key_t <= qry_t, s, NEG_INF)
    m0 = jnp.max(s, axis=0, keepdims=True)
    p = jnp.exp2(s - m0)
    m_ref[...] = m0
    l_ref[...] = jnp.sum(p, axis=0, keepdims=True)
    acc_ref[...] = jnp.dot(vt_ref[i], p.astype(BF16), preferred_element_type=F32)

    def absorb(slot, j):
        s = s_ref[slot]
        picked = sel_ref[pl.ds(j, 1), :] > 0.5
        m_old = m_ref[...]
        m_new = jnp.maximum(m_old, jnp.where(picked, jnp.max(s, axis=0, keepdims=True), NEG_INF))
        alpha = jnp.exp2(m_old - m_new)
        p = jnp.exp2(s - jnp.where(picked, m_new, -NEG_INF))
        m_ref[...] = m_new
        l_ref[...] = alpha * l_ref[...] + jnp.sum(p, axis=0, keepdims=True)
        acc_ref[...] = alpha * acc_ref[...] + jnp.dot(vt_ref[j], p.astype(BF16),
                                                      preferred_element_type=F32)

    s_ref[0] = scores(0)

    def body(t, carry):
        ja = 2 * t
        s_ref[1] = scores(ja + 1)
        absorb(0, ja)
        s_ref[0] = scores(jnp.minimum(ja + 2, nblk - 1))
        absorb(1, ja + 1)
        return carry

    lax.fori_loop(0, i // 2, body, 0)

    @pl.when(i % 2 == 1)
    def _():
        absorb(0, i - 1)

    out = acc_ref[...] * (1.0 / l_ref[...])
    for h in range(KV_GROUP):
        o_ref[:, h * HEAD_DIM:(h + 1) * HEAD_DIM] = out[:, h * tq:(h + 1) * tq].T.astype(o_ref.dtype)


def _attn_prompt(qarr, q_col, k4, v4, B, S):
    tq = MOBA_BLOCK
    assert S % MOBA_BLOCK == 0
    nblk = S // MOBA_BLOCK
    nq = S // tq
    gw = KV_GROUP * HEAD_DIM
    n = KV_GROUP * tq
    kern = functools.partial(_attn_prompt_kernel, tq=tq, nblk=nblk, scale=HEAD_DIM ** -0.5)
    return pl.pallas_call(
        kern,
        grid=(B, N_KV_HEADS, nq),
        in_specs=[pl.BlockSpec((tq, gw), lambda b, g, i: (b * nq + i, q_col // gw + g)),
                  pl.BlockSpec((S * N_KV_HEADS, HEAD_DIM), lambda b, g, i: (b, 0)),
                  pl.BlockSpec((S * N_KV_HEADS, HEAD_DIM), lambda b, g, i: (b, 0))],
        out_specs=pl.BlockSpec((tq, gw), lambda b, g, i: (b * nq + i, g)),
        out_shape=jax.ShapeDtypeStruct((B * S, N_HEADS * HEAD_DIM), BF16),
        scratch_shapes=[pltpu.VMEM((nblk, MOBA_BLOCK, HEAD_DIM), BF16),
                        pltpu.VMEM((nblk, HEAD_DIM, MOBA_BLOCK), BF16),
                        pltpu.VMEM((nblk, HEAD_DIM), F32),
                        pltpu.VMEM((nblk, n), F32),
                        pltpu.VMEM((HEAD_DIM, n), BF16),
                        pltpu.VMEM((2, MOBA_BLOCK, n), F32),
                        pltpu.VMEM((1, n), F32),
                        pltpu.VMEM((1, n), F32),
                        pltpu.VMEM((HEAD_DIM, n), F32)],
        compiler_params=_params("arbitrary", "arbitrary", "arbitrary"),
        name="moba_prompt",
    )(qarr, k4, v4)


def _attn_decode_kernel(pt_ref, q_ref, kn_ref, vn_ref, *refs, pps, ns, nblk, ds, nseq, scale):
    del pt_ref
    k_refs = refs[:pps]
    v_refs = refs[pps:2 * pps]
    o_ref = refs[2 * pps]
    wf_ref, wb_ref, s_ref, p_ref, kmean_ref, l_ref, acc_ref = refs[2 * pps + 1:]
    r = pl.program_id(0)
    step = pl.program_id(1)
    bps = pps // 2
    kvw = N_KV_HEADS * HEAD_DIM
    nrow = N_HEADS * ds
    has_values = r >= 1
    has_keys = r < nseq

    def by_head(ref):
        tokens = ref.shape[0] // N_KV_HEADS
        return jnp.concatenate([ref[pl.ds(g, tokens, stride=N_KV_HEADS), :]
                                for g in range(N_KV_HEADS)], axis=1)

    def block_of(page_refs, jj):
        return jnp.concatenate([by_head(page_refs[2 * jj]), by_head(page_refs[2 * jj + 1])], axis=0)

    @pl.when(has_values)
    def _():
        for jj in range(bps):
            j = step * bps + jj
            vblk = block_of(v_refs, jj).astype(BF16)
            acc_ref[...] += jnp.dot(p_ref[j], vblk, preferred_element_type=F32)

    @pl.when(jnp.logical_and(has_values, step == ns - 1))
    def _():
        acc = acc_ref[...] * (1.0 / l_ref[...])
        outs = []
        for h in range(N_HEADS):
            g = h // KV_GROUP
            outs.append(acc[h * ds:(h + 1) * ds, g * HEAD_DIM:(g + 1) * HEAD_DIM])
        o_ref[...] = jnp.concatenate(outs, axis=1)

    @pl.when(jnp.logical_and(has_keys, step == 0))
    def _():
        q = q_ref[...]
        zero = jnp.zeros((ds, HEAD_DIM), F32)
        rows = []
        for h in range(N_HEADS):
            pieces = [zero] * N_KV_HEADS
            pieces[h // KV_GROUP] = q[:, h * HEAD_DIM:(h + 1) * HEAD_DIM]
            rows.append(jnp.concatenate(pieces, axis=1))
        wf = jnp.concatenate(rows, axis=0)
        wf_ref[...] = wf
        wb_ref[...] = (wf * scale).astype(BF16)

    @pl.when(has_keys)
    def _():
        for jj in range(bps):
            j = step * bps + jj
            kblk = block_of(k_refs, jj)
            kmean_ref[pl.ds(j, 1), :] = jnp.mean(kblk, axis=0, keepdims=True)
            s_ref[j] = lax.dot_general(wb_ref[...], kblk.astype(BF16), _NT,
                                       preferred_element_type=F32)

    @pl.when(jnp.logical_and(has_keys, step == ns - 1))
    def _():
        gate = lax.dot_general(kmean_ref[...], wf_ref[...], _NT, precision=lax.Precision.HIGHEST,
                               preferred_element_type=F32)
        blk = lax.broadcasted_iota(jnp.int32, gate.shape, 0)
        sel = _topk_select(gate, blk, blk < nblk, axis=0)
        bias = jnp.concatenate([jnp.where(sel, 0.0, NEG_INF),
                                jnp.zeros((nrow - nblk, nrow), F32)], axis=0).T

        pad = jnp.zeros((nrow - ds, kvw), F32)
        knp = jnp.concatenate([by_head(kn_ref), pad], axis=0).astype(BF16)
        sn = lax.dot_general(wb_ref[...], knp, _NT, preferred_element_type=F32)
        qry_t = lax.broadcasted_iota(jnp.int32, sn.shape, 0) % ds
        key_t = lax.broadcasted_iota(jnp.int32, sn.shape, 1)
        sn = jnp.where(key_t <= qry_t, sn, NEG_INF)

        mx = jnp.full((nrow, MOBA_BLOCK), NEG_INF, F32)
        for j in range(nblk):
            sb = s_ref[j] + bias[:, j:j + 1]
            s_ref[j] = sb
            mx = jnp.maximum(mx, sb)
        m = jnp.maximum(jnp.max(mx, axis=1, keepdims=True), jnp.max(sn, axis=1, keepdims=True))
        lsum = jnp.zeros((nrow, MOBA_BLOCK), F32)
        for j in range(nblk):
            p = jnp.exp(s_ref[j] - m)
            lsum = lsum + p
            p_ref[j] = p.astype(BF16)
        pn = jnp.exp(sn - m)
        l_ref[...] = jnp.sum(lsum, axis=1, keepdims=True) + jnp.sum(pn, axis=1, keepdims=True)
        vnp = jnp.concatenate([by_head(vn_ref), pad], axis=0).astype(BF16)
        acc_ref[...] = jnp.dot(pn.astype(BF16), vnp, preferred_element_type=F32)


def _attn_decode(qarr, q_col, k4, v4, cache_k, cache_v, layer, page_table, DB, DS):
    depth, n_phys, page, nkv, hd = cache_k.shape
    n_pages = page_table.shape[1]
    past_len = n_pages * page
    kvw = nkv * hd
    assert (nkv, hd) == (N_KV_HEADS, HEAD_DIM) and MOBA_BLOCK == 2 * page
    assert past_len % MOBA_BLOCK == 0 and DS == SUBLANES
    nblk = past_len // MOBA_BLOCK
    pps = 16
    assert n_pages % pps == 0
    ns = n_pages // pps
    nrow = N_HEADS * DS
    ck = cache_k.reshape(depth * n_phys, page * nkv, hd)
    cv = cache_v.reshape(depth * n_phys, page * nkv, hd)
    base = layer * n_phys

    kseq = lambda r: jnp.minimum(r, DB - 1)
    vseq = lambda r: jnp.maximum(r - 1, 0)

    def k_map(i):
        return lambda r, s, pt: (base + pt[kseq(r), s * pps + i], 0, 0)

    def v_map(i):
        return lambda r, s, pt: (base + pt[vseq(r), s * pps + i], 0, 0)

    qw = N_HEADS * HEAD_DIM
    in_specs = [pl.BlockSpec((DS, qw), lambda r, s, pt: (kseq(r), q_col // qw)),
                pl.BlockSpec((DS * nkv, hd), lambda r, s, pt: (kseq(r), 0)),
                pl.BlockSpec((DS * nkv, hd), lambda r, s, pt: (kseq(r), 0))]
    in_specs += [pl.BlockSpec((None, page * nkv, hd), k_map(i)) for i in range(pps)]
    in_specs += [pl.BlockSpec((None, page * nkv, hd), v_map(i)) for i in range(pps)]
    kern = functools.partial(_attn_decode_kernel, pps=pps, ns=ns, nblk=nblk, ds=DS, nseq=DB,
                             scale=HEAD_DIM ** -0.5)
    grid_spec = pltpu.PrefetchScalarGridSpec(
        num_scalar_prefetch=1,
        grid=(DB + 1, ns),
        in_specs=in_specs,
        out_specs=pl.BlockSpec((DS, qw), lambda r, s, pt: (vseq(r), 0)),
        scratch_shapes=[pltpu.VMEM((nrow, kvw), F32),
                        pltpu.VMEM((nrow, kvw), BF16),
                        pltpu.VMEM((nblk, nrow, MOBA_BLOCK), F32),
                        pltpu.VMEM((nblk, nrow, MOBA_BLOCK), BF16),
                        pltpu.VMEM((nblk, kvw), F32),
                        pltpu.VMEM((nrow, 1), F32),
                        pltpu.VMEM((nrow, kvw), F32)])
    return pl.pallas_call(
        kern,
        grid_spec=grid_spec,
        out_shape=jax.ShapeDtypeStruct((DB * DS, qw), F32),
        compiler_params=_params("arbitrary", "arbitrary"),
        name="moba_decode",
    )(page_table, qarr, k4, v4, *([ck] * pps), *([cv] * pps))


def _rglru_kernel(xr_ref, yg_ref, h0_ref, cb_ref, cw_ref, cbias_ref, wa_ref, ba_ref, wx_ref, bx_ref,
                  lam_ref, rnn_ref, hlast_ref, cnew_ref, halo_ref, hcarry_ref, *, L):
    c = pl.program_id(1)
    C = xr_ref.shape[1]

    @pl.when(c == 0)
    def _():
        hcarry_ref[...] = h0_ref[...]
        halo_ref[...] = jnp.concatenate(
            [jnp.zeros((SUBLANES - (CONV_W - 1), C), F32), cb_ref[...]], axis=0)

    G = L // SUBLANES
    xr = xr_ref[...]
    xg = jnp.concatenate([halo_ref[...], xr], axis=0).reshape(G + 1, SUBLANES, C)
    rin = lax.broadcasted_iota(jnp.int32, (1, SUBLANES, 1), 1)
    cw = cw_ref[...]
    xc = cbias_ref[...].reshape(1, 1, C)
    for j in range(CONV_W):
        k = CONV_W - 1 - j
        if k == 0:
            xk = xg[1:]
        else:
            rolled = pltpu.roll(xg, k, 1)
            xk = jnp.where(rin >= k, rolled[1:], rolled[:-1])
        xc = xc + xk * cw[j:j + 1, :].reshape(1, 1, C)
    xc = xc.reshape(L, C)

    xcb = xc.astype(BF16)
    bw = C // RNN_BLOCKS
    r_parts, i_parts = [], []
    for n in range(RNN_BLOCKS):
        xs = xcb[:, n * bw:(n + 1) * bw]
        r_parts.append(jnp.dot(xs, wa_ref[n], preferred_element_type=F32))
        i_parts.append(jnp.dot(xs, wx_ref[n], preferred_element_type=F32))
    r = jax.nn.sigmoid(jnp.concatenate(r_parts, axis=1) + ba_ref[...])
    ig = jax.nn.sigmoid(jnp.concatenate(i_parts, axis=1) + bx_ref[...])

    nl = -lam_ref[...]
    softplus = jnp.maximum(nl, 0.0) + jnp.log1p(jnp.exp(-jnp.abs(nl)))
    log_a = -LRU_C * r * softplus
    a = jnp.exp(log_a)
    gain2 = -jnp.tanh(log_a) * (a * a + 1.0)
    gain = jnp.where(gain2 > 0.0, gain2 * lax.rsqrt(gain2), 0.0)
    bv = gain * (ig * xc)

    a = a.reshape(G, SUBLANES, C)
    bv = bv.reshape(G, SUBLANES, C)
    d = 1
    while d < SUBLANES:
        same_group = rin >= d
        bv = a * jnp.where(same_group, pltpu.roll(bv, d, 1), 0.0) + bv
        a = a * jnp.where(same_group, pltpu.roll(a, d, 1), 1.0)
        d *= 2
    carry = hcarry_ref[...]
    groups = []
    for g in range(G):
        hg = bv[g] + a[g] * carry
        groups.append(hg)
        carry = hg[SUBLANES - 1:SUBLANES, :]
    h = groups[0] if G == 1 else jnp.concatenate(groups, axis=0)
    hcarry_ref[...] = carry
    hlast_ref[...] = carry

    yg = yg_ref[...].astype(F32)
    cdf = 0.5 * (1.0 + jnp.tanh(0.7978845608028654 * (yg + 0.044715 * (yg * yg * yg))))
    rnn_ref[...] = (h * (yg * cdf)).astype(rnn_ref.dtype)

    halo_ref[...] = xr[L - SUBLANES:L, :]
    cnew_ref[...] = xr[L - (CONV_W - 1):L, :]


def _rglru(xr_arr, xr_col, yg_arr, yg_col, B, S, h0, conv_buf, conv_w, conv_b, wa, ba, wx, bx, lam):
    C = h0.shape[-1]
    L = min(S, 256)
    assert S % L == 0 and L % SUBLANES == 0 and xr_col % C == 0 and yg_col % C == 0
    nc = S // L
    out_dtype = BF16 if L % 16 == 0 else F32
    row = lambda v: v.reshape(1, C)
    const2 = lambda b, c: (0, 0)
    const3 = lambda b, c: (0, 0, 0)
    rnn, hlast, cnew = pl.pallas_call(
        functools.partial(_rglru_kernel, L=L),
        grid=(B, nc),
        in_specs=[pl.BlockSpec((L, C), lambda b, c: (b * nc + c, xr_col // C)),
                  pl.BlockSpec((L, C), lambda b, c: (b * nc + c, yg_col // C)),
                  pl.BlockSpec((None, 1, C), lambda b, c: (b, 0, 0)),
                  pl.BlockSpec((None, CONV_W - 1, C), lambda b, c: (b, 0, 0)),
                  pl.BlockSpec((CONV_W, C), const2),
                  pl.BlockSpec((1, C), const2),
                  pl.BlockSpec((RNN_BLOCKS, C // RNN_BLOCKS, C // RNN_BLOCKS), const3),
                  pl.BlockSpec((1, C), const2),
                  pl.BlockSpec((RNN_BLOCKS, C // RNN_BLOCKS, C // RNN_BLOCKS), const3),
                  pl.BlockSpec((1, C), const2),
                  pl.BlockSpec((1, C), const2)],
        out_specs=[pl.BlockSpec((L, C), lambda b, c: (b * nc + c, 0)),
                   pl.BlockSpec((None, 1, C), lambda b, c: (b, 0, 0)),
                   pl.BlockSpec((None, CONV_W - 1, C), lambda b, c: (b, 0, 0))],
        out_shape=[jax.ShapeDtypeStruct((B * S, C), out_dtype),
                   jax.ShapeDtypeStruct((B, 1, C), F32),
                   jax.ShapeDtypeStruct((B, CONV_W - 1, C), F32)],
        scratch_shapes=[pltpu.VMEM((SUBLANES, C), F32),
                        pltpu.VMEM((1, C), F32)],
        compiler_params=_params("arbitrary", "arbitrary"),
        name="rglru",
    )(xr_arr, yg_arr, h0.reshape(B, 1, C), conv_buf, conv_w, row(conv_b), wa, row(ba), wx, row(bx), row(lam))
    return rnn, hlast.reshape(B, C), cnew


def _merge_kernel(rnn_ref, attn_ref, u_ref, wa_ref, wb_ref, wga_ref, wgb_ref, bga_ref, bgb_ref, o_ref):
    u = u_ref[...]
    a = jnp.dot(rnn_ref[...].astype(BF16), wa_ref[...], preferred_element_type=F32)
    b = jnp.dot(attn_ref[...].astype(BF16), wb_ref[...], preferred_element_type=F32)
    ga = jax.nn.sigmoid(jnp.dot(u, wga_ref[...], preferred_element_type=F32) + bga_ref[...])
    gb = jax.nn.sigmoid(jnp.dot(u, wgb_ref[...], preferred_element_type=F32) + bgb_ref[...])
    o_ref[...] = (ga * a + gb * b).astype(o_ref.dtype)


def _merge(rnn, attn, u, wa, wb, w_in, gate_col, b_gate):
    T, D = u.shape
    tm = _row_tile(T, 1024)
    tn = COL_CHUNK
    nb = D // tn
    assert gate_col % tn == 0
    act = lambda m, n: (m, 0)
    col = lambda m, n: (0, n)
    col_hi = lambda m, n: (0, n + nb)
    gcol = lambda m, n: (0, gate_col // tn + n)
    gcol_hi = lambda m, n: (0, gate_col // tn + nb + n)
    bg = b_gate.reshape(1, 2 * D)
    return pl.pallas_call(
        _merge_kernel,
        grid=(T // tm, nb),
        in_specs=[pl.BlockSpec((tm, D), act), pl.BlockSpec((tm, D), act), pl.BlockSpec((tm, D), act),
                  pl.BlockSpec((D, tn), col), pl.BlockSpec((D, tn), col),
                  pl.BlockSpec((D, tn), gcol), pl.BlockSpec((D, tn), gcol_hi),
                  pl.BlockSpec((1, tn), col), pl.BlockSpec((1, tn), col_hi)],
        out_specs=pl.BlockSpec((tm, tn), lambda m, n: (m, n)),
        out_shape=jax.ShapeDtypeStruct((T, D), BF16),
        compiler_params=_params("arbitrary", "arbitrary"),
        name="merge",
    )(rnn, attn, u, wa, wb, w_in, w_in, bg, bg)


def _outproj_kernel(mg_ref, x_ref, w_ref, g_ref, x2_ref, u2_ref):
    mg = mg_ref[...]
    D = x_ref.shape[1]
    ssq = jnp.zeros((x_ref.shape[0], 1), F32)
    for c in range(D // COL_CHUNK):
        sl = slice(c * COL_CHUNK, (c + 1) * COL_CHUNK)
        y = x_ref[:, sl] + jnp.dot(mg, w_ref[:, sl], preferred_element_type=F32)
        x2_ref[:, sl] = y
        ssq = ssq + jnp.sum(y * y, axis=-1, keepdims=True)
    inv = lax.rsqrt(ssq / D + RMS_EPS)
    u2_ref[...] = (x2_ref[...] * inv * g_ref[...]).astype(u2_ref.dtype)


def _outproj(merged, x, w_out, g):
    T, D = x.shape
    tm = _row_tile(T, 512)
    row = lambda m: (m, 0)
    return pl.pallas_call(
        _outproj_kernel,
        grid=(T // tm,),
        in_specs=[pl.BlockSpec((tm, D), row), pl.BlockSpec((tm, D), row),
                  pl.BlockSpec((D, D), lambda m: (0, 0)), pl.BlockSpec((1, D), lambda m: (0, 0))],
        out_specs=[pl.BlockSpec((tm, D), row), pl.BlockSpec((tm, D), row)],
        out_shape=[jax.ShapeDtypeStruct((T, D), F32), jax.ShapeDtypeStruct((T, D), BF16)],
        compiler_params=_params("arbitrary"),
        name="out_proj",
    )(merged, x, w_out, g.reshape(1, D))


def _ffn_kernel(u2_ref, x2_ref, w1_ref, w2_ref, gf_ref, y_ref, *, nf, final_norm):
    f = pl.program_id(1)

    @pl.when(f == 0)
    def _():
        y_ref[...] = x2_ref[...]

    h = jnp.dot(u2_ref[...], w1_ref[...], preferred_element_type=F32)
    h = jnp.square(jnp.maximum(h, 0.0)).astype(BF16)
    for c in range(y_ref.shape[1] // COL_CHUNK):
        sl = slice(c * COL_CHUNK, (c + 1) * COL_CHUNK)
        y_ref[:, sl] += jnp.dot(h, w2_ref[:, sl], preferred_element_type=F32)

    if final_norm:
        @pl.when(f == nf - 1)
        def _():
            x3 = y_ref[...]
            ms = jnp.mean(x3 * x3, axis=-1, keepdims=True)
            y_ref[...] = x3 * lax.rsqrt(ms + RMS_EPS) * gf_ref[...]


def _ffn(u2, x2, w1, w2, g_final, final_norm):
    T, D = x2.shape
    F = w1.shape[1]
    tm = _row_tile(T, 1024)
    tf = 512 if tm == 1024 else 1024
    nf = F // tf
    row = lambda m, f: (m, 0)
    return pl.pallas_call(
        functools.partial(_ffn_kernel, nf=nf, final_norm=final_norm),
        grid=(T // tm, nf),
        in_specs=[pl.BlockSpec((tm, D), row), pl.BlockSpec((tm, D), row),
                  pl.BlockSpec((D, tf), lambda m, f: (0, f)), pl.BlockSpec((tf, D), lambda m, f: (f, 0)),
                  pl.BlockSpec((1, D), lambda m, f: (0, 0))],
        out_specs=pl.BlockSpec((tm, D), row),
        out_shape=jax.ShapeDtypeStruct((T, D), F32),
        compiler_params=_params("arbitrary", "arbitrary"),
        name="ffn",
    )(u2, x2, w1, w2, g_final.reshape(1, D))


def _rope_tables(pos):
    half = HEAD_DIM // 2
    inv = ROPE_THETA ** (-jnp.arange(half, dtype=F32) * (2.0 / HEAD_DIM))
    ang = pos.astype(F32)[:, None] * inv[None, :]
    cos, sin = jnp.cos(ang), jnp.sin(ang)
    return jnp.concatenate([cos, cos], axis=-1), jnp.concatenate([-sin, sin], axis=-1)


def _decoder_layer(x, B, S, cos, sin, past, h0, conv_buf, w, final_norm):
    C = h0.shape[-1]
    attn_w = N_HEADS * HEAD_DIM
    act_dtype = BF16 if S % 16 == 0 else F32

    kv_w = N_KV_HEADS * HEAD_DIM
    q_col, k_col, v_col = 0, attn_w, attn_w + kv_w
    xr_col = v_col + kv_w
    yg_col = xr_col + C
    gate_col = yg_col + C
    w_in = w["w_in"]

    u, xr = _norm_proj(x, w["norm_mix"], w_in, xr_col, C, F32)
    ygq = _proj(u, w_in, [(yg_col, C), (q_col, attn_w)], cos, sin,
                (C, C + attn_w), act_dtype)
    k4, v4 = _kv_proj(u, w_in, k_col, v_col, cos, sin)
    if past is None:
        attn = _attn_prompt(ygq, C, k4, v4, B, S)
    else:
        attn = _attn_decode(ygq, C, k4, v4, past[0], past[1], past[2], past[3], B, S)
    rnn, h_last, conv_new = _rglru(xr, 0, ygq, 0, B, S, h0, conv_buf, w["conv_w"], w["conv_b"],
                                   w["w_rg_a"], w["b_rg_a"], w["w_rg_x"], w["b_rg_x"], w["lru_lambda"])
    merged = _merge(rnn, attn, u, w["w_proj_a"], w["w_proj_b"], w_in, gate_col, w["b_gate"])
    x2, u2 = _outproj(merged, x, w["w_out"], w["norm_mlp"])
    y = _ffn(u2, x2, w["w_ff1"], w["w_ff2"], w["norm_final"], final_norm)
    k = k4.reshape(B, S, N_KV_HEADS, HEAD_DIM)
    v = v4.reshape(B, S, N_KV_HEADS, HEAD_DIM)
    return y, k, v, h_last, conv_new


def kernel(x_prompt, x_sample, cache_k, cache_v, state_h, state_conv, page_table, norm_mix, w_in, b_gate, conv_w, conv_b, w_rg_a, b_rg_a, w_rg_x, b_rg_x, lru_lambda, w_proj_a, w_proj_b, w_out, norm_mlp, w_ff1, w_ff2, norm_final):
    B, S, D = x_prompt.shape
    DB, DS, _ = x_sample.shape
    depth = w_in.shape[0]
    C = state_h.shape[-1]
    past_len = page_table.shape[1] * cache_k.shape[2]

    cos_p, sin_p = _rope_tables(jnp.arange(S, dtype=jnp.int32))
    cos_s, sin_s = _rope_tables(past_len + jnp.arange(DS, dtype=jnp.int32))
    cos_s, sin_s = jnp.tile(cos_s, (DB, 1)), jnp.tile(sin_s, (DB, 1))

    hp = x_prompt.reshape(B * S, D)
    hs = x_sample.reshape(DB * DS, D)
    outs = [[] for _ in range(8)]
    for l in range(depth):
        wl = w_in[l]
        w = dict(
            norm_mix=norm_mix[l], norm_mlp=norm_mlp[l], norm_final=norm_final,
            w_in=wl.astype(BF16), b_gate=b_gate[l],
            conv_w=conv_w[l], conv_b=conv_b[l],
            w_rg_a=w_rg_a[l].astype(BF16), b_rg_a=b_rg_a[l],
            w_rg_x=w_rg_x[l].astype(BF16), b_rg_x=b_rg_x[l], lru_lambda=lru_lambda[l],
            w_proj_a=w_proj_a[l].astype(BF16), w_proj_b=w_proj_b[l].astype(BF16),
            w_out=w_out[l].astype(BF16), w_ff1=w_ff1[l].astype(BF16), w_ff2=w_ff2[l].astype(BF16))
        final = l == depth - 1
        h0_p = jnp.zeros((B, C), F32)
        buf_p = jnp.zeros((B, CONV_W - 1, C), F32)
        hp, k1, v1, r1, c1 = _decoder_layer(hp, B, S, cos_p, sin_p, None, h0_p, buf_p, w, final)
        past = (cache_k, cache_v, l, page_table)
        hs, k2, v2, r2, c2 = _decoder_layer(hs, DB, DS, cos_s, sin_s, past, state_h[l], state_conv[l], w, final)
        for lst, val in zip(outs, (k1, v1, r1, c1, k2, v2, r2, c2)):
            lst.append(val)
    return (hp.reshape(B, S, D), hs.reshape(DB, DS, D), *[jnp.stack(o) for o in outs])
```

```python
import functools

import jax
import jax.numpy as jnp
from jax import lax
from jax.experimental import pallas as pl
from jax.experimental.pallas import tpu as pltpu

F32 = jnp.float32
BF16 = jnp.bfloat16

N_HEADS = 16
N_KV_HEADS = 4
HEAD_DIM = 128
KV_GROUP = N_HEADS // N_KV_HEADS
ROPE_THETA = 10000.0
MOBA_BLOCK = 256
MOBA_TOPK = 3
RNN_BLOCKS = 16
CONV_W = 4
LRU_C = 8.0
RMS_EPS = 1e-6
NEG_INF = -1e30
LOG2_E = 1.4426950408889634

LANES = 128
SUBLANES = 8
VMEM_LIMIT_BYTES = 56 * 1024 * 1024
COL_CHUNK = 512

_NT = (((1,), (1,)), ((), ()))


def _params(*sem):
    return pltpu.CompilerParams(dimension_semantics=sem, vmem_limit_bytes=VMEM_LIMIT_BYTES)


def _row_tile(rows, pref):
    t = min(rows, pref)
    assert rows % t == 0, (rows, pref)
    return t


def _proj_kernel(u_ref, w_ref, cos_ref, sin_ref, o_ref, *, rope_lo, rope_hi):
    acc = jnp.dot(u_ref[...], w_ref[...], preferred_element_type=F32)
    n = pl.program_id(1)
    roped = jnp.logical_and(n >= rope_lo, n < rope_hi)

    @pl.when(roped)
    def _():
        cos = cos_ref[...]
        sin = sin_ref[...]
        for c in range(acc.shape[1] // HEAD_DIM):
            sl = slice(c * HEAD_DIM, (c + 1) * HEAD_DIM)
            o_ref[:, sl] = _rope(acc[:, sl], cos, sin).astype(o_ref.dtype)

    @pl.when(jnp.logical_not(roped))
    def _():
        o_ref[...] = acc.astype(o_ref.dtype)


def _rope(xh, cos, sin):
    return xh * cos + pltpu.roll(xh, HEAD_DIM // 2, 1) * sin


def _segment_block(segments, tn):
    assert all(s % tn == 0 and w % tn == 0 for s, w in segments)

    def block(n):
        idx, first = None, 0
        for start, width in segments:
            here = start // tn + n - first
            idx = here if idx is None else jnp.where(n >= first, here, idx)
            first += width // tn
        return idx

    return block


def _proj(u, w, segments, cos, sin, rope_cols, out_dtype):
    T, D = u.shape
    N = sum(width for _, width in segments)
    tm = _row_tile(T, 1024)
    tn = 2 * COL_CHUNK
    n_tab = cos.shape[0] // tm
    assert cos.shape[0] % tm == 0 and rope_cols[0] % tn == 0 and rope_cols[1] % tn == 0
    kern = functools.partial(_proj_kernel, rope_lo=rope_cols[0] // tn, rope_hi=rope_cols[1] // tn)
    wblock = _segment_block(segments, tn)
    return pl.pallas_call(
        kern,
        grid=(T // tm, N // tn),
        in_specs=[pl.BlockSpec((tm, D), lambda m, n: (m, 0)),
                  pl.BlockSpec((D, tn), lambda m, n: (0, wblock(n))),
                  pl.BlockSpec((tm, HEAD_DIM), lambda m, n: (m % n_tab, 0)),
                  pl.BlockSpec((tm, HEAD_DIM), lambda m, n: (m % n_tab, 0))],
        out_specs=pl.BlockSpec((tm, tn), lambda m, n: (m, n)),
        out_shape=jax.ShapeDtypeStruct((T, N), out_dtype),
        compiler_params=_params("arbitrary", "arbitrary"),
        name="in_proj",
    )(u, w, cos, sin)


def _norm_proj_kernel(x_ref, g_ref, w_ref, u_ref, o_ref):
    @pl.when(pl.program_id(1) == 0)
    def _():
        x = x_ref[...]
        ms = jnp.mean(x * x, axis=-1, keepdims=True)
        u_ref[...] = (x * lax.rsqrt(ms + RMS_EPS) * g_ref[...]).astype(u_ref.dtype)

    o_ref[...] = jnp.dot(u_ref[...], w_ref[...], preferred_element_type=F32).astype(o_ref.dtype)


def _norm_proj(x, g, w, col, width, out_dtype):
    T, D = x.shape
    tm = _row_tile(T, 1024)
    tn = 2 * COL_CHUNK
    assert col % tn == 0 and width % tn == 0
    return pl.pallas_call(
        _norm_proj_kernel,
        grid=(T // tm, width // tn),
        in_specs=[pl.BlockSpec((tm, D), lambda m, n: (m, 0)),
                  pl.BlockSpec((1, D), lambda m, n: (0, 0)),
                  pl.BlockSpec((D, tn), lambda m, n: (0, col // tn + n))],
        out_specs=[pl.BlockSpec((tm, D), lambda m, n: (m, 0)),
                   pl.BlockSpec((tm, tn), lambda m, n: (m, n))],
        out_shape=[jax.ShapeDtypeStruct((T, D), BF16), jax.ShapeDtypeStruct((T, width), out_dtype)],
        compiler_params=_params("arbitrary", "arbitrary"),
        name="norm_proj",
    )(x, g.reshape(1, D), w)


def _kv_proj_kernel(u_ref, wk_ref, wv_ref, cos_ref, sin_ref, k_ref, v_ref):
    u = u_ref[...]
    tm = u.shape[0]
    cos = cos_ref[...]
    sin = sin_ref[...]
    k = jnp.dot(u, wk_ref[...], preferred_element_type=F32)
    for g in range(N_KV_HEADS):
        k_ref[pl.ds(g, tm, stride=N_KV_HEADS), :] = _rope(k[:, g * HEAD_DIM:(g + 1) * HEAD_DIM], cos, sin)
    v = jnp.dot(u, wv_ref[...], preferred_element_type=F32)
    for g in range(N_KV_HEADS):
        v_ref[pl.ds(g, tm, stride=N_KV_HEADS), :] = v[:, g * HEAD_DIM:(g + 1) * HEAD_DIM]


def _kv_proj(u, w, k_col, v_col, cos, sin):
    T, D = u.shape
    kvw = N_KV_HEADS * HEAD_DIM
    assert k_col % kvw == 0 and v_col % kvw == 0
    tm = _row_tile(T, 1024)
    n_tab = cos.shape[0] // tm
    assert cos.shape[0] % tm == 0
    out = jax.ShapeDtypeStruct((T * N_KV_HEADS, HEAD_DIM), F32)
    out_spec = pl.BlockSpec((tm * N_KV_HEADS, HEAD_DIM), lambda m: (m, 0))
    return pl.pallas_call(
        _kv_proj_kernel,
        grid=(T // tm,),
        in_specs=[pl.BlockSpec((tm, D), lambda m: (m, 0)),
                  pl.BlockSpec((D, kvw), lambda m: (0, k_col // kvw)),
                  pl.BlockSpec((D, kvw), lambda m: (0, v_col // kvw)),
                  pl.BlockSpec((tm, HEAD_DIM), lambda m: (m % n_tab, 0)),
                  pl.BlockSpec((tm, HEAD_DIM), lambda m: (m % n_tab, 0))],
        out_specs=[out_spec, out_spec],
        out_shape=[out, out],
        compiler_params=_params("arbitrary"),
        name="kv_proj",
    )(u, w, w, cos, sin)


def _topk_select(gate, blk, eligible, axis):
    gm = jnp.where(eligible, gate, NEG_INF)
    rank = jnp.zeros(gate.shape, jnp.int32)
    for r in range(gate.shape[axis]):
        gr = gm[r:r + 1, :] if axis == 0 else gm[:, r:r + 1]
        beats = jnp.logical_or(gr > gm, jnp.logical_and(gr == gm, blk > r))
        rank = rank + beats.astype(jnp.int32)
    return jnp.logical_and(rank < MOBA_TOPK, eligible)


def _attn_prompt_kernel(q_ref, k_ref, v_ref, o_ref,
                        kb_ref, vt_ref, kmean_ref, kmh_ref, sel_ref, qt_ref, s_ref, m_ref, l_ref, acc_ref,
                        *, tq, nblk, scale):
    g = pl.program_id(1)
    i = pl.program_id(2)

    @pl.when(i == 0)
    def _():
        for j in range(nblk):
            rows = pl.ds(j * MOBA_BLOCK * N_KV_HEADS + g, MOBA_BLOCK, stride=N_KV_HEADS)
            kj = k_ref[rows, :]
            kb_ref[j] = kj.astype(BF16)
            kmean_ref[j:j + 1, :] = jnp.mean(kj, axis=0, keepdims=True)
            vt_ref[j] = v_ref[rows, :].T.astype(BF16)
        km = kmean_ref[...]
        km_hi = km.astype(BF16)
        kmh_ref[...] = jnp.concatenate([km_hi, (km - km_hi.astype(F32)).astype(BF16)], axis=0)

    q = q_ref[...].astype(F32)
    qs = jnp.concatenate([q[:, h * HEAD_DIM:(h + 1) * HEAD_DIM] for h in range(KV_GROUP)], axis=0)
    qt = (qs * (scale * LOG2_E)).T
    qt_hi = qt.astype(BF16)
    qt_lo = (qt - qt_hi.astype(F32)).astype(BF16)
    qt_ref[...] = qt_hi

    def scores(j):
        return jnp.dot(kb_ref[j], qt_ref[...], preferred_element_type=F32)

    ext = jnp.dot(jnp.concatenate([kb_ref[i], kmh_ref[...]], axis=0), qt_hi,
                  preferred_element_type=F32)
    gate = (ext[MOBA_BLOCK:MOBA_BLOCK + nblk] + ext[MOBA_BLOCK + nblk:]) + jnp.dot(
        kmh_ref[:nblk, :], qt_lo, preferred_element_type=F32)
    s_ref[0] = scores(0)
    blk = lax.broadcasted_iota(jnp.int32, gate.shape, 0)
    sel = _topk_select(gate, blk, blk < i, axis=0)
    sel_ref[...] = sel.astype(F32)

    s = ext[:MOBA_BLOCK]
    key_t = lax.broadcasted_iota(jnp.int32, s.shape, 0)
    qry_t = lax.broadcasted_iota(jnp.int32, s.shape, 1) % tq
    s = jnp.where(key_t <= qry_t, s, NEG_INF)
    m0 = jnp.max(s, axis=0, keepdims=True)
    p = jnp.exp2(s - m0)
    m_ref[...] = m0
    l_ref[...] = jnp.sum(p, axis=0, keepdims=True)
    acc_ref[...] = jnp.dot(vt_ref[i], p.astype(BF16), preferred_element_type=F32)

    def absorb(slot, j):
        s = s_ref[slot]
        picked = sel_ref[pl.ds(j, 1), :] > 0.5
        m_old = m_ref[...]
        m_new = jnp.maximum(m_old, jnp.where(picked, jnp.max(s, axis=0, keepdims=True), NEG_INF))
        alpha = jnp.exp2(m_old - m_new)
        p = jnp.exp2(s - jnp.where(picked, m_new, -NEG_INF))
        m_ref[...] = m_new
        l_ref[...] = alpha * l_ref[...] + jnp.sum(p, axis=0, keepdims=True)
        acc_ref[...] = alpha * acc_ref[...] + jnp.dot(vt_ref[j], p.astype(BF16),
                                                      preferred_element_type=F32)

    def body(t, carry):
        ja = 2 * t
        s_ref[1] = scores(ja + 1)
        absorb(0, ja)
        s_ref[0] = scores(jnp.minimum(ja + 2, nblk - 1))
        absorb(1, ja + 1)
        return carry

    lax.fori_loop(0, i // 2, body, 0)

    @pl.when(i % 2 == 1)
    def _():
        absorb(0, i - 1)

    out = acc_ref[...] * (1.0 / l_ref[...])
    for h in range(KV_GROUP):
        o_ref[:, h * HEAD_DIM:(h + 1) * HEAD_DIM] = out[:, h * tq:(h + 1) * tq].T.astype(o_ref.dtype)


def _attn_prompt(qarr, q_col, k4, v4, B, S):
    tq = MOBA_BLOCK
    assert S % MOBA_BLOCK == 0
    nblk = S // MOBA_BLOCK
    nq = S // tq
    gw = KV_GROUP * HEAD_DIM
    n = KV_GROUP * tq
    kern = functools.partial(_attn_prompt_kernel, tq=tq, nblk=nblk, scale=HEAD_DIM ** -0.5)
    return pl.pallas_call(
        kern,
        grid=(B, N_KV_HEADS, nq),
        in_specs=[pl.BlockSpec((tq, gw), lambda b, g, i: (b * nq + i, q_col // gw + g)),
                  pl.BlockSpec((S * N_KV_HEADS, HEAD_DIM), lambda b, g, i: (b, 0)),
                  pl.BlockSpec((S * N_KV_HEADS, HEAD_DIM), lambda b, g, i: (b, 0))],
        out_specs=pl.BlockSpec((tq, gw), lambda b, g, i: (b * nq + i, g)),
        out_shape=jax.ShapeDtypeStruct((B * S, N_HEADS * HEAD_DIM), BF16),
        scratch_shapes=[pltpu.VMEM((nblk, MOBA_BLOCK, HEAD_DIM), BF16),
                        pltpu.VMEM((nblk, HEAD_DIM, MOBA_BLOCK), BF16),
                        pltpu.VMEM((nblk, HEAD_DIM), F32),
                        pltpu.VMEM((2 * nblk, HEAD_DIM), BF16),
                        pltpu.VMEM((nblk, n), F32),
                        pltpu.VMEM((HEAD_DIM, n), BF16),
                        pltpu.VMEM((2, MOBA_BLOCK, n), F32),
                        pltpu.VMEM((1, n), F32),
                        pltpu.VMEM((1, n), F32),
                        pltpu.VMEM((HEAD_DIM, n), F32)],
        compiler_params=_params("arbitrary", "arbitrary", "arbitrary"),
        name="moba_prompt",
    )(qarr, k4, v4)


def _attn_decode_kernel(pt_ref, q_ref, kn_ref, vn_ref, *refs, pps, ns, nblk, ds, nseq, scale):
    del pt_ref
    k_refs = refs[:pps]
    v_refs = refs[pps:2 * pps]
    o_ref = refs[2 * pps]
    wf_ref, wb_ref, s_ref, p_ref, kmean_ref, l_ref, acc_ref = refs[2 * pps + 1:]
    r = pl.program_id(0)
    step = pl.program_id(1)
    bps = pps // 2
    kvw = N_KV_HEADS * HEAD_DIM
    nrow = N_HEADS * ds
    has_values = r >= 1
    has_keys = r < nseq

    def by_head(ref):
        tokens = ref.shape[0] // N_KV_HEADS
        return jnp.concatenate([ref[pl.ds(g, tokens, stride=N_KV_HEADS), :]
                                for g in range(N_KV_HEADS)], axis=1)

    def block_of(page_refs, jj):
        return jnp.concatenate([by_head(page_refs[2 * jj]), by_head(page_refs[2 * jj + 1])], axis=0)

    @pl.when(has_values)
    def _():
        for jj in range(bps):
            j = step * bps + jj
            vblk = block_of(v_refs, jj).astype(BF16)
            acc_ref[...] += jnp.dot(p_ref[j], vblk, preferred_element_type=F32)

    @pl.when(jnp.logical_and(has_values, step == ns - 1))
    def _():
        acc = acc_ref[...] * (1.0 / l_ref[...])
        outs = []
        for h in range(N_HEADS):
            g = h // KV_GROUP
            outs.append(acc[h * ds:(h + 1) * ds, g * HEAD_DIM:(g + 1) * HEAD_DIM])
        o_ref[...] = jnp.concatenate(outs, axis=1)

    @pl.when(jnp.logical_and(has_keys, step == 0))
    def _():
        q = q_ref[...]
        zero = jnp.zeros((ds, HEAD_DIM), F32)
        rows = []
        for h in range(N_HEADS):
            pieces = [zero] * N_KV_HEADS
            pieces[h // KV_GROUP] = q[:, h * HEAD_DIM:(h + 1) * HEAD_DIM]
            rows.append(jnp.concatenate(pieces, axis=1))
        wf = jnp.concatenate(rows, axis=0)
        wf_ref[...] = wf
        wb_ref[...] = (wf * scale).astype(BF16)

    @pl.when(has_keys)
    def _():
        for jj in range(bps):
            j = step * bps + jj
            kblk = block_of(k_refs, jj)
            kmean_ref[pl.ds(j, 1), :] = jnp.mean(kblk, axis=0, keepdims=True)
            s_ref[j] = lax.dot_general(wb_ref[...], kblk.astype(BF16), _NT,
                                       preferred_element_type=F32)

    @pl.when(jnp.logical_and(has_keys, step == ns - 1))
    def _():
        gate = lax.dot_general(kmean_ref[...], wf_ref[...], _NT, precision=lax.Precision.HIGHEST,
                               preferred_element_type=F32)
        blk = lax.broadcasted_iota(jnp.int32, gate.shape, 0)
        sel = _topk_select(gate, blk, blk < nblk, axis=0)
        bias = jnp.concatenate([jnp.where(sel, 0.0, NEG_INF),
                                jnp.zeros((nrow - nblk, nrow), F32)], axis=0).T

        pad = jnp.zeros((nrow - ds, kvw), F32)
        knp = jnp.concatenate([by_head(kn_ref), pad], axis=0).astype(BF16)
        sn = lax.dot_general(wb_ref[...], knp, _NT, preferred_element_type=F32)
        qry_t = lax.broadcasted_iota(jnp.int32, sn.shape, 0) % ds
        key_t = lax.broadcasted_iota(jnp.int32, sn.shape, 1)
        sn = jnp.where(key_t <= qry_t, sn, NEG_INF)

        mx = jnp.full((nrow, MOBA_BLOCK), NEG_INF, F32)
        for j in range(nblk):
            sb = s_ref[j] + bias[:, j:j + 1]
            s_ref[j] = sb
            mx = jnp.maximum(mx, sb)
        m = jnp.maximum(jnp.max(mx, axis=1, keepdims=True), jnp.max(sn, axis=1, keepdims=True))
        lsum = jnp.zeros((nrow, MOBA_BLOCK), F32)
        for j in range(nblk):
            p = jnp.exp(s_ref[j] - m)
            lsum = lsum + p
            p_ref[j] = p.astype(BF16)
        pn = jnp.exp(sn - m)
        l_ref[...] = jnp.sum(lsum, axis=1, keepdims=True) + jnp.sum(pn, axis=1, keepdims=True)
        vnp = jnp.concatenate([by_head(vn_ref), pad], axis=0).astype(BF16)
        acc_ref[...] = jnp.dot(pn.astype(BF16), vnp, preferred_element_type=F32)


def _attn_decode(qarr, q_col, k4, v4, cache_k, cache_v, layer, page_table, DB, DS):
    depth, n_phys, page, nkv, hd = cache_k.shape
    n_pages = page_table.shape[1]
    past_len = n_pages * page
    kvw = nkv * hd
    assert (nkv, hd) == (N_KV_HEADS, HEAD_DIM) and MOBA_BLOCK == 2 * page
    assert past_len % MOBA_BLOCK == 0 and DS == SUBLANES
    nblk = past_len // MOBA_BLOCK
    pps = 16
    assert n_pages % pps == 0
    ns = n_pages // pps
    nrow = N_HEADS * DS
    ck = cache_k.reshape(depth * n_phys, page * nkv, hd)
    cv = cache_v.reshape(depth * n_phys, page * nkv, hd)
    base = layer * n_phys

    kseq = lambda r: jnp.minimum(r, DB - 1)
    vseq = lambda r: jnp.maximum(r - 1, 0)

    def k_map(i):
        return lambda r, s, pt: (base + pt[kseq(r), s * pps + i], 0, 0)

    def v_map(i):
        return lambda r, s, pt: (base + pt[vseq(r), s * pps + i], 0, 0)

    qw = N_HEADS * HEAD_DIM
    in_specs = [pl.BlockSpec((DS, qw), lambda r, s, pt: (kseq(r), q_col // qw)),
                pl.BlockSpec((DS * nkv, hd), lambda r, s, pt: (kseq(r), 0)),
                pl.BlockSpec((DS * nkv, hd), lambda r, s, pt: (kseq(r), 0))]
    in_specs += [pl.BlockSpec((None, page * nkv, hd), k_map(i)) for i in range(pps)]
    in_specs += [pl.BlockSpec((None, page * nkv, hd), v_map(i)) for i in range(pps)]
    kern = functools.partial(_attn_decode_kernel, pps=pps, ns=ns, nblk=nblk, ds=DS, nseq=DB,
                             scale=HEAD_DIM ** -0.5)
    grid_spec = pltpu.PrefetchScalarGridSpec(
        num_scalar_prefetch=1,
        grid=(DB + 1, ns),
        in_specs=in_specs,
        out_specs=pl.BlockSpec((DS, qw), lambda r, s, pt: (vseq(r), 0)),
        scratch_shapes=[pltpu.VMEM((nrow, kvw), F32),
                        pltpu.VMEM((nrow, kvw), BF16),
                        pltpu.VMEM((nblk, nrow, MOBA_BLOCK), F32),
                        pltpu.VMEM((nblk, nrow, MOBA_BLOCK), BF16),
                        pltpu.VMEM((nblk, kvw), F32),
                        pltpu.VMEM((nrow, 1), F32),
                        pltpu.VMEM((nrow, kvw), F32)])
    return pl.pallas_call(
        kern,
        grid_spec=grid_spec,
        out_shape=jax.ShapeDtypeStruct((DB * DS, qw), F32),
        compiler_params=_params("arbitrary", "arbitrary"),
        name="moba_decode",
    )(page_table, qarr, k4, v4, *([ck] * pps), *([cv] * pps))


def _rglru_kernel(xr_ref, yg_ref, h0_ref, cb_ref, cw_ref, cbias_ref, wa_ref, ba_ref, wx_ref, bx_ref,
                  lam_ref, rnn_ref, hlast_ref, cnew_ref, halo_ref, hcarry_ref, *, L):
    c = pl.program_id(1)
    C = xr_ref.shape[1]

    @pl.when(c == 0)
    def _():
        hcarry_ref[...] = h0_ref[...]
        halo_ref[...] = jnp.concatenate(
            [jnp.zeros((SUBLANES - (CONV_W - 1), C), F32), cb_ref[...]], axis=0)

    G = L // SUBLANES
    xr = xr_ref[...]
    xg = jnp.concatenate([halo_ref[...], xr], axis=0).reshape(G + 1, SUBLANES, C)
    rin = lax.broadcasted_iota(jnp.int32, (1, SUBLANES, 1), 1)
    cw = cw_ref[...]
    xc = cbias_ref[...].reshape(1, 1, C)
    for j in range(CONV_W):
        k = CONV_W - 1 - j
        if k == 0:
            xk = xg[1:]
        else:
            rolled = pltpu.roll(xg, k, 1)
            xk = jnp.where(rin >= k, rolled[1:], rolled[:-1])
        xc = xc + xk * cw[j:j + 1, :].reshape(1, 1, C)
    xc = xc.reshape(L, C)

    xcb = xc.astype(BF16)
    bw = C // RNN_BLOCKS
    r_parts, i_parts = [], []
    for n in range(RNN_BLOCKS):
        xs = xcb[:, n * bw:(n + 1) * bw]
        r_parts.append(jnp.dot(xs, wa_ref[n], preferred_element_type=F32))
        i_parts.append(jnp.dot(xs, wx_ref[n], preferred_element_type=F32))
    r = jax.nn.sigmoid(jnp.concatenate(r_parts, axis=1) + ba_ref[...])
    ig = jax.nn.sigmoid(jnp.concatenate(i_parts, axis=1) + bx_ref[...])

    nl = -lam_ref[...]
    softplus = jnp.maximum(nl, 0.0) + jnp.log1p(jnp.exp(-jnp.abs(nl)))
    log_a = -LRU_C * r * softplus
    a = jnp.exp(log_a)
    gain2 = -jnp.tanh(log_a) * (a * a + 1.0)
    gain = jnp.where(gain2 > 0.0, gain2 * lax.rsqrt(gain2), 0.0)
    bv = gain * (ig * xc)

    a = a.reshape(G, SUBLANES, C)
    bv = bv.reshape(G, SUBLANES, C)
    d = 1
    while d < SUBLANES:
        same_group = rin >= d
        bv = a * jnp.where(same_group, pltpu.roll(bv, d, 1), 0.0) + bv
        a = a * jnp.where(same_group, pltpu.roll(a, d, 1), 1.0)
        d *= 2
    carry = hcarry_ref[...]
    groups = []
    for g in range(G):
        hg = bv[g] + a[g] * carry
        groups.append(hg)
        carry = hg[SUBLANES - 1:SUBLANES, :]
    h = groups[0] if G == 1 else jnp.concatenate(groups, axis=0)
    hcarry_ref[...] = carry
    hlast_ref[...] = carry

    yg = yg_ref[...].astype(F32)
    cdf = 0.5 * (1.0 + jnp.tanh(0.7978845608028654 * (yg + 0.044715 * (yg * yg * yg))))
    rnn_ref[...] = (h * (yg * cdf)).astype(rnn_ref.dtype)

    halo_ref[...] = xr[L - SUBLANES:L, :]
    cnew_ref[...] = xr[L - (CONV_W - 1):L, :]


def _rglru(xr_arr, xr_col, yg_arr, yg_col, B, S, h0, conv_buf, conv_w, conv_b, wa, ba, wx, bx, lam):
    C = h0.shape[-1]
    L = min(S, 256)
    assert S % L == 0 and L % SUBLANES == 0 and xr_col % C == 0 and yg_col % C == 0
    nc = S // L
    out_dtype = BF16 if L % 16 == 0 else F32
    row = lambda v: v.reshape(1, C)
    const2 = lambda b, c: (0, 0)
    const3 = lambda b, c: (0, 0, 0)
    rnn, hlast, cnew = pl.pallas_call(
        functools.partial(_rglru_kernel, L=L),
        grid=(B, nc),
        in_specs=[pl.BlockSpec((L, C), lambda b, c: (b * nc + c, xr_col // C)),
                  pl.BlockSpec((L, C), lambda b, c: (b * nc + c, yg_col // C)),
                  pl.BlockSpec((None, 1, C), lambda b, c: (b, 0, 0)),
                  pl.BlockSpec((None, CONV_W - 1, C), lambda b, c: (b, 0, 0)),
                  pl.BlockSpec((CONV_W, C), const2),
                  pl.BlockSpec((1, C), const2),
                  pl.BlockSpec((RNN_BLOCKS, C // RNN_BLOCKS, C // RNN_BLOCKS), const3),
                  pl.BlockSpec((1, C), const2),
                  pl.BlockSpec((RNN_BLOCKS, C // RNN_BLOCKS, C // RNN_BLOCKS), const3),
                  pl.BlockSpec((1, C), const2),
                  pl.BlockSpec((1, C), const2)],
        out_specs=[pl.BlockSpec((L, C), lambda b, c: (b * nc + c, 0)),
                   pl.BlockSpec((None, 1, C), lambda b, c: (b, 0, 0)),
                   pl.BlockSpec((None, CONV_W - 1, C), lambda b, c: (b, 0, 0))],
        out_shape=[jax.ShapeDtypeStruct((B * S, C), out_dtype),
                   jax.ShapeDtypeStruct((B, 1, C), F32),
                   jax.ShapeDtypeStruct((B, CONV_W - 1, C), F32)],
        scratch_shapes=[pltpu.VMEM((SUBLANES, C), F32),
                        pltpu.VMEM((1, C), F32)],
        compiler_params=_params("arbitrary", "arbitrary"),
        name="rglru",
    )(xr_arr, yg_arr, h0.reshape(B, 1, C), conv_buf, conv_w, row(conv_b), wa, row(ba), wx, row(bx), row(lam))
    return rnn, hlast.reshape(B, C), cnew


def _merge_kernel(rnn_ref, attn_ref, u_ref, wa_ref, wb_ref, wga_ref, wgb_ref, bga_ref, bgb_ref, o_ref):
    u = u_ref[...]
    a = jnp.dot(rnn_ref[...].astype(BF16), wa_ref[...], preferred_element_type=F32)
    b = jnp.dot(attn_ref[...].astype(BF16), wb_ref[...], preferred_element_type=F32)
    ga = jax.nn.sigmoid(jnp.dot(u, wga_ref[...], preferred_element_type=F32) + bga_ref[...])
    gb = jax.nn.sigmoid(jnp.dot(u, wgb_ref[...], preferred_element_type=F32) + bgb_ref[...])
    o_ref[...] = (ga * a + gb * b).astype(o_ref.dtype)


def _merge(rnn, attn, u, wa, wb, w_in, gate_col, b_gate):
    T, D = u.shape
    tm = _row_tile(T, 1024)
    tn = COL_CHUNK
    nb = D // tn
    assert gate_col % tn == 0
    act = lambda m, n: (m, 0)
    col = lambda m, n: (0, n)
    col_hi = lambda m, n: (0, n + nb)
    gcol = lambda m, n: (0, gate_col // tn + n)
    gcol_hi = lambda m, n: (0, gate_col // tn + nb + n)
    bg = b_gate.reshape(1, 2 * D)
    return pl.pallas_call(
        _merge_kernel,
        grid=(T // tm, nb),
        in_specs=[pl.BlockSpec((tm, D), act), pl.BlockSpec((tm, D), act), pl.BlockSpec((tm, D), act),
                  pl.BlockSpec((D, tn), col), pl.BlockSpec((D, tn), col),
                  pl.BlockSpec((D, tn), gcol), pl.BlockSpec((D, tn), gcol_hi),
                  pl.BlockSpec((1, tn), col), pl.BlockSpec((1, tn), col_hi)],
        out_specs=pl.BlockSpec((tm, tn), lambda m, n: (m, n)),
        out_shape=jax.ShapeDtypeStruct((T, D), BF16),
        compiler_params=_params("arbitrary", "arbitrary"),
        name="merge",
    )(rnn, attn, u, wa, wb, w_in, w_in, bg, bg)


def _outproj_kernel(mg_ref, x_ref, w_ref, g_ref, x2_ref, u2_ref):
    mg = mg_ref[...]
    D = x_ref.shape[1]
    ssq = jnp.zeros((x_ref.shape[0], 1), F32)
    for c in range(D // COL_CHUNK):
        sl = slice(c * COL_CHUNK, (c + 1) * COL_CHUNK)
        y = x_ref[:, sl] + jnp.dot(mg, w_ref[:, sl], preferred_element_type=F32)
        x2_ref[:, sl] = y
        ssq = ssq + jnp.sum(y * y, axis=-1, keepdims=True)
    inv = lax.rsqrt(ssq / D + RMS_EPS)
    u2_ref[...] = (x2_ref[...] * inv * g_ref[...]).astype(u2_ref.dtype)


def _outproj(merged, x, w_out, g):
    T, D = x.shape
    tm = _row_tile(T, 512)
    row = lambda m: (m, 0)
    return pl.pallas_call(
        _outproj_kernel,
        grid=(T // tm,),
        in_specs=[pl.BlockSpec((tm, D), row), pl.BlockSpec((tm, D), row),
                  pl.BlockSpec((D, D), lambda m: (0, 0)), pl.BlockSpec((1, D), lambda m: (0, 0))],
        out_specs=[pl.BlockSpec((tm, D), row), pl.BlockSpec((tm, D), row)],
        out_shape=[jax.ShapeDtypeStruct((T, D), F32), jax.ShapeDtypeStruct((T, D), BF16)],
        compiler_params=_params("arbitrary"),
        name="out_proj",
    )(merged, x, w_out, g.reshape(1, D))


def _ffn_kernel(u2_ref, x2_ref, w1_ref, w2_ref, gf_ref, y_ref, *, nf, final_norm):
    f = pl.program_id(1)

    @pl.when(f == 0)
    def _():
        y_ref[...] = x2_ref[...]

    h = jnp.dot(u2_ref[...], w1_ref[...], preferred_element_type=F32)
    h = jnp.square(jnp.maximum(h, 0.0)).astype(BF16)
    for c in range(y_ref.shape[1] // COL_CHUNK):
        sl = slice(c * COL_CHUNK, (c + 1) * COL_CHUNK)
        y_ref[:, sl] += jnp.dot(h, w2_ref[:, sl], preferred_element_type=F32)

    if final_norm:
        @pl.when(f == nf - 1)
        def _():
            x3 = y_ref[...]
            ms = jnp.mean(x3 * x3, axis=-1, keepdims=True)
            y_ref[...] = x3 * lax.rsqrt(ms + RMS_EPS) * gf_ref[...]


def _ffn(u2, x2, w1, w2, g_final, final_norm):
    T, D = x2.shape
    F = w1.shape[1]
    tm = _row_tile(T, 1024)
    tf = 512 if tm == 1024 else 1024
    nf = F // tf
    row = lambda m, f: (m, 0)
    return pl.pallas_call(
        functools.partial(_ffn_kernel, nf=nf, final_norm=final_norm),
        grid=(T // tm, nf),
        in_specs=[pl.BlockSpec((tm, D), row), pl.BlockSpec((tm, D), row),
                  pl.BlockSpec((D, tf), lambda m, f: (0, f)), pl.BlockSpec((tf, D), lambda m, f: (f, 0)),
                  pl.BlockSpec((1, D), lambda m, f: (0, 0))],
        out_specs=pl.BlockSpec((tm, D), row),
        out_shape=jax.ShapeDtypeStruct((T, D), F32),
        compiler_params=_params("arbitrary", "arbitrary"),
        name="ffn",
    )(u2, x2, w1, w2, g_final.reshape(1, D))


def _rope_tables(pos):
    half = HEAD_DIM // 2
    inv = ROPE_THETA ** (-jnp.arange(half, dtype=F32) * (2.0 / HEAD_DIM))
    ang = pos.astype(F32)[:, None] * inv[None, :]
    cos, sin = jnp.cos(ang), jnp.sin(ang)
    return jnp.concatenate([cos, cos], axis=-1), jnp.concatenate([-sin, sin], axis=-1)


def _decoder_layer(x, B, S, cos, sin, past, h0, conv_buf, w, final_norm):
    C = h0.shape[-1]
    attn_w = N_HEADS * HEAD_DIM
    act_dtype = BF16 if S % 16 == 0 else F32

    kv_w = N_KV_HEADS * HEAD_DIM
    q_col, k_col, v_col = 0, attn_w, attn_w + kv_w
    xr_col = v_col + kv_w
    yg_col = xr_col + C
    gate_col = yg_col + C
    w_in = w["w_in"]

    u, xr = _norm_proj(x, w["norm_mix"], w_in, xr_col, C, F32)
    ygq = _proj(u, w_in, [(yg_col, C), (q_col, attn_w)], cos, sin,
                (C, C + attn_w), act_dtype)
    k4, v4 = _kv_proj(u, w_in, k_col, v_col, cos, sin)
    if past is None:
        attn = _attn_prompt(ygq, C, k4, v4, B, S)
    else:
        attn = _attn_decode(ygq, C, k4, v4, past[0], past[1], past[2], past[3], B, S)
    rnn, h_last, conv_new = _rglru(xr, 0, ygq, 0, B, S, h0, conv_buf, w["conv_w"], w["conv_b"],
                                   w["w_rg_a"], w["b_rg_a"], w["w_rg_x"], w["b_rg_x"], w["lru_lambda"])
    merged = _merge(rnn, attn, u, w["w_proj_a"], w["w_proj_b"], w_in, gate_col, w["b_gate"])
    x2, u2 = _outproj(merged, x, w["w_out"], w["norm_mlp"])
    y = _ffn(u2, x2, w["w_ff1"], w["w_ff2"], w["norm_final"], final_norm)
    k = k4.reshape(B, S, N_KV_HEADS, HEAD_DIM)
    v = v4.reshape(B, S, N_KV_HEADS, HEAD_DIM)
    return y, k, v, h_last, conv_new


def kernel(x_prompt, x_sample, cache_k, cache_v, state_h, state_conv, page_table, norm_mix, w_in, b_gate, conv_w, conv_b, w_rg_a, b_rg_a, w_rg_x, b_rg_x, lru_lambda, w_proj_a, w_proj_b, w_out, norm_mlp, w_ff1, w_ff2, norm_final):
    B, S, D = x_prompt.shape
    DB, DS, _ = x_sample.shape
    depth = w_in.shape[0]
    C = state_h.shape[-1]
    past_len = page_table.shape[1] * cache_k.shape[2]

    cos_p, sin_p = _rope_tables(jnp.arange(S, dtype=jnp.int32))
    cos_s, sin_s = _rope_tables(past_len + jnp.arange(DS, dtype=jnp.int32))
    cos_s, sin_s = jnp.tile(cos_s, (DB, 1)), jnp.tile(sin_s, (DB, 1))

    hp = x_prompt.reshape(B * S, D)
    hs = x_sample.reshape(DB * DS, D)
    outs = [[] for _ in range(8)]
    for l in range(depth):
        wl = w_in[l]
        w = dict(
            norm_mix=norm_mix[l], norm_mlp=norm_mlp[l], norm_final=norm_final,
            w_in=wl.astype(BF16), b_gate=b_gate[l],
            conv_w=conv_w[l], conv_b=conv_b[l],
            w_rg_a=w_rg_a[l].astype(BF16), b_rg_a=b_rg_a[l],
            w_rg_x=w_rg_x[l].astype(BF16), b_rg_x=b_rg_x[l], lru_lambda=lru_lambda[l],
            w_proj_a=w_proj_a[l].astype(BF16), w_proj_b=w_proj_b[l].astype(BF16),
            w_out=w_out[l].astype(BF16), w_ff1=w_ff1[l].astype(BF16), w_ff2=w_ff2[l].astype(BF16))
        final = l == depth - 1
        h0_p = jnp.zeros((B, C), F32)
        buf_p = jnp.zeros((B, CONV_W - 1, C), F32)
        hp, k1, v1, r1, c1 = _decoder_layer(hp, B, S, cos_p, sin_p, None, h0_p, buf_p, w, final)
        past = (cache_k, cache_v, l, page_table)
        hs, k2, v2, r2, c2 = _decoder_layer(hs, DB, DS, cos_s, sin_s, past, state_h[l], state_conv[l], w, final)
        for lst, val in zip(outs, (k1, v1, r1, c1, k2, v2, r2, c2)):
            lst.append(val)
    return (hp.reshape(B, S, D), hs.reshape(DB, DS, D), *[jnp.stack(o) for o in outs])
```

```python
import functools

import jax
import jax.numpy as jnp
from jax import lax
from jax.experimental import pallas as pl
from jax.experimental.pallas import tpu as pltpu

F32 = jnp.float32
BF16 = jnp.bfloat16

N_HEADS = 16
N_KV_HEADS = 4
HEAD_DIM = 128
KV_GROUP = N_HEADS // N_KV_HEADS
ROPE_THETA = 10000.0
MOBA_BLOCK = 256
MOBA_TOPK = 3
RNN_BLOCKS = 16
CONV_W = 4
LRU_C = 8.0
RMS_EPS = 1e-6
NEG_INF = -1e30
LOG2_E = 1.4426950408889634

LANES = 128
SUBLANES = 8
VMEM_LIMIT_BYTES = 56 * 1024 * 1024
COL_CHUNK = 512

_NT = (((1,), (1,)), ((), ()))


def _params(*sem):
    return pltpu.CompilerParams(dimension_semantics=sem, vmem_limit_bytes=VMEM_LIMIT_BYTES)


def _row_tile(rows, pref):
    t = min(rows, pref)
    assert rows % t == 0, (rows, pref)
    return t


def _proj_kernel(u_ref, w_ref, cos_ref, sin_ref, o_ref, *, rope_lo, rope_hi):
    acc = jnp.dot(u_ref[...], w_ref[...], preferred_element_type=F32)
    n = pl.program_id(1)
    roped = jnp.logical_and(n >= rope_lo, n < rope_hi)

    @pl.when(roped)
    def _():
        cos = cos_ref[...]
        sin = sin_ref[...]
        for c in range(acc.shape[1] // HEAD_DIM):
            sl = slice(c * HEAD_DIM, (c + 1) * HEAD_DIM)
            o_ref[:, sl] = _rope(acc[:, sl], cos, sin).astype(o_ref.dtype)

    @pl.when(jnp.logical_not(roped))
    def _():
        o_ref[...] = acc.astype(o_ref.dtype)


def _rope(xh, cos, sin):
    return xh * cos + pltpu.roll(xh, HEAD_DIM // 2, 1) * sin


def _segment_block(segments, tn):
    assert all(s % tn == 0 and w % tn == 0 for s, w in segments)

    def block(n):
        idx, first = None, 0
        for start, width in segments:
            here = start // tn + n - first
            idx = here if idx is None else jnp.where(n >= first, here, idx)
            first += width // tn
        return idx

    return block


def _proj(u, w, segments, cos, sin, rope_cols, out_dtype):
    T, D = u.shape
    N = sum(width for _, width in segments)
    tm = _row_tile(T, 1024)
    tn = 2 * COL_CHUNK
    n_tab = cos.shape[0] // tm
    assert cos.shape[0] % tm == 0 and rope_cols[0] % tn == 0 and rope_cols[1] % tn == 0
    kern = functools.partial(_proj_kernel, rope_lo=rope_cols[0] // tn, rope_hi=rope_cols[1] // tn)
    wblock = _segment_block(segments, tn)
    return pl.pallas_call(
        kern,
        grid=(T // tm, N // tn),
        in_specs=[pl.BlockSpec((tm, D), lambda m, n: (m, 0)),
                  pl.BlockSpec((D, tn), lambda m, n: (0, wblock(n))),
                  pl.BlockSpec((tm, HEAD_DIM), lambda m, n: (m % n_tab, 0)),
                  pl.BlockSpec((tm, HEAD_DIM), lambda m, n: (m % n_tab, 0))],
        out_specs=pl.BlockSpec((tm, tn), lambda m, n: (m, n)),
        out_shape=jax.ShapeDtypeStruct((T, N), out_dtype),
        compiler_params=_params("arbitrary", "arbitrary"),
        name="in_proj",
    )(u, w, cos, sin)


def _norm_proj_kernel(x_ref, g_ref, w_ref, u_ref, o_ref):
    @pl.when(pl.program_id(1) == 0)
    def _():
        x = x_ref[...]
        ms = jnp.mean(x * x, axis=-1, keepdims=True)
        u_ref[...] = (x * lax.rsqrt(ms + RMS_EPS) * g_ref[...]).astype(u_ref.dtype)

    o_ref[...] = jnp.dot(u_ref[...], w_ref[...], preferred_element_type=F32).astype(o_ref.dtype)


def _norm_proj(x, g, w, col, width, out_dtype):
    T, D = x.shape
    tm = _row_tile(T, 1024)
    tn = 2 * COL_CHUNK
    assert col % tn == 0 and width % tn == 0
    return pl.pallas_call(
        _norm_proj_kernel,
        grid=(T // tm, width // tn),
        in_specs=[pl.BlockSpec((tm, D), lambda m, n: (m, 0)),
                  pl.BlockSpec((1, D), lambda m, n: (0, 0)),
                  pl.BlockSpec((D, tn), lambda m, n: (0, col // tn + n))],
        out_specs=[pl.BlockSpec((tm, D), lambda m, n: (m, 0)),
                   pl.BlockSpec((tm, tn), lambda m, n: (m, n))],
        out_shape=[jax.ShapeDtypeStruct((T, D), BF16), jax.ShapeDtypeStruct((T, width), out_dtype)],
        compiler_params=_params("arbitrary", "arbitrary"),
        name="norm_proj",
    )(x, g.reshape(1, D), w)


def _kv_proj_kernel(u_ref, wk_ref, wv_ref, cos_ref, sin_ref, k_ref, v_ref):
    u = u_ref[...]
    tm = u.shape[0]
    cos = cos_ref[...]
    sin = sin_ref[...]
    k = jnp.dot(u, wk_ref[...], preferred_element_type=F32)
    for g in range(N_KV_HEADS):
        k_ref[pl.ds(g, tm, stride=N_KV_HEADS), :] = _rope(k[:, g * HEAD_DIM:(g + 1) * HEAD_DIM], cos, sin)
    v = jnp.dot(u, wv_ref[...], preferred_element_type=F32)
    for g in range(N_KV_HEADS):
        v_ref[pl.ds(g, tm, stride=N_KV_HEADS), :] = v[:, g * HEAD_DIM:(g + 1) * HEAD_DIM]


def _kv_proj(u, w, k_col, v_col, cos, sin):
    T, D = u.shape
    kvw = N_KV_HEADS * HEAD_DIM
    assert k_col % kvw == 0 and v_col % kvw == 0
    tm = _row_tile(T, 1024)
    n_tab = cos.shape[0] // tm
    assert cos.shape[0] % tm == 0
    out = jax.ShapeDtypeStruct((T * N_KV_HEADS, HEAD_DIM), F32)
    out_spec = pl.BlockSpec((tm * N_KV_HEADS, HEAD_DIM), lambda m: (m, 0))
    return pl.pallas_call(
        _kv_proj_kernel,
        grid=(T // tm,),
        in_specs=[pl.BlockSpec((tm, D), lambda m: (m, 0)),
                  pl.BlockSpec((D, kvw), lambda m: (0, k_col // kvw)),
                  pl.BlockSpec((D, kvw), lambda m: (0, v_col // kvw)),
                  pl.BlockSpec((tm, HEAD_DIM), lambda m: (m % n_tab, 0)),
                  pl.BlockSpec((tm, HEAD_DIM), lambda m: (m % n_tab, 0))],
        out_specs=[out_spec, out_spec],
        out_shape=[out, out],
        compiler_params=_params("arbitrary"),
        name="kv_proj",
    )(u, w, w, cos, sin)


def _topk_select(gate, blk, eligible, axis):
    gm = jnp.where(eligible, gate, NEG_INF)
    rank = jnp.zeros(gate.shape, jnp.int32)
    for r in range(gate.shape[axis]):
        gr = gm[r:r + 1, :] if axis == 0 else gm[:, r:r + 1]
        beats = jnp.logical_or(gr > gm, jnp.logical_and(gr == gm, blk > r))
        rank = rank + beats.astype(jnp.int32)
    return jnp.logical_and(rank < MOBA_TOPK, eligible)


def _attn_prompt_kernel(q_ref, k_ref, v_ref, o_ref,
                        kb_ref, vt_ref, kmean_ref, kmh_ref, sel_ref, qt_ref, s_ref, m_ref, l_ref, acc_ref,
                        *, tq, nblk, scale):
    g = pl.program_id(1)
    i = pl.program_id(2)

    @pl.when(i == 0)
    def _():
        for j in range(nblk):
            rows = pl.ds(j * MOBA_BLOCK * N_KV_HEADS + g, MOBA_BLOCK, stride=N_KV_HEADS)
            kj = k_ref[rows, :]
            kb_ref[j] = kj.astype(BF16)
            kmean_ref[j:j + 1, :] = jnp.mean(kj, axis=0, keepdims=True)
            vt_ref[j] = v_ref[rows, :].T.astype(BF16)
        km = kmean_ref[...]
        km_hi = km.astype(BF16)
        kmh_ref[...] = jnp.concatenate([km_hi, (km - km_hi.astype(F32)).astype(BF16)], axis=0)

    q = q_ref[...].astype(F32)
    qs = jnp.concatenate([q[:, h * HEAD_DIM:(h + 1) * HEAD_DIM] for h in range(KV_GROUP)], axis=0)
    qt = (qs * (scale * LOG2_E)).T
    qt_hi = qt.astype(BF16)
    qt_lo = (qt - qt_hi.astype(F32)).astype(BF16)
    qt_ref[...] = qt_hi

    def scores(j):
        return jnp.dot(kb_ref[j], qt_ref[...], preferred_element_type=F32)

    ext = jnp.dot(jnp.concatenate([kb_ref[i], kmh_ref[...]], axis=0), qt_hi,
                  preferred_element_type=F32)
    gate = (ext[MOBA_BLOCK:MOBA_BLOCK + nblk] + ext[MOBA_BLOCK + nblk:]) + jnp.dot(
        kmh_ref[:nblk, :], qt_lo, preferred_element_type=F32)
    s_ref[0] = scores(0)
    blk = lax.broadcasted_iota(jnp.int32, gate.shape, 0)
    sel = _topk_select(gate, blk, blk < i, axis=0)
    sel_ref[...] = sel.astype(F32)

    s = ext[:MOBA_BLOCK]
    key_t = lax.broadcasted_iota(jnp.int32, s.shape, 0)
    qry_t = lax.broadcasted_iota(jnp.int32, s.shape, 1) % tq
    s = jnp.where(key_t <= qry_t, s, NEG_INF)
    m0 = jnp.max(s, axis=0, keepdims=True)
    p = jnp.exp2(s - m0)
    m_ref[...] = m0
    l_ref[...] = jnp.sum(p, axis=0, keepdims=True)
    acc_ref[...] = jnp.dot(vt_ref[i], p.astype(BF16), preferred_element_type=F32)

    def absorb(slot, j):
        s = s_ref[slot]
        picked = sel_ref[pl.ds(j, 1), :] > 0.5
        m_old = m_ref[...]
        m_new = jnp.maximum(m_old, jnp.where(picked, jnp.max(s, axis=0, keepdims=True), NEG_INF))
        alpha = jnp.exp2(m_old - m_new)
        p = jnp.exp2(s - jnp.where(picked, m_new, -NEG_INF))
        m_ref[...] = m_new
        l_ref[...] = alpha * l_ref[...] + jnp.sum(p, axis=0, keepdims=True)
        acc_ref[...] = alpha * acc_ref[...] + jnp.dot(vt_ref[j], p.astype(BF16),
                                                      preferred_element_type=F32)

    def pair(ja):
        s_ref[1] = scores(ja + 1)
        absorb(0, ja)
        s_ref[0] = scores(jnp.minimum(ja + 2, nblk - 1))
        absorb(1, ja + 1)

    def body(t, carry):
        pair(4 * t)
        pair(4 * t + 2)
        return carry

    quads = i // 4
    lax.fori_loop(0, quads, body, 0)

    @pl.when(i % 4 >= 2)
    def _():
        pair(4 * quads)

    @pl.when(i % 2 == 1)
    def _():
        absorb(0, i - 1)

    out = acc_ref[...] * (1.0 / l_ref[...])
    for h in range(KV_GROUP):
        o_ref[:, h * HEAD_DIM:(h + 1) * HEAD_DIM] = out[:, h * tq:(h + 1) * tq].T.astype(o_ref.dtype)


def _attn_prompt(qarr, q_col, k4, v4, B, S):
    tq = MOBA_BLOCK
    assert S % MOBA_BLOCK == 0
    nblk = S // MOBA_BLOCK
    nq = S // tq
    gw = KV_GROUP * HEAD_DIM
    n = KV_GROUP * tq
    kern = functools.partial(_attn_prompt_kernel, tq=tq, nblk=nblk, scale=HEAD_DIM ** -0.5)
    return pl.pallas_call(
        kern,
        grid=(B, N_KV_HEADS, nq),
        in_specs=[pl.BlockSpec((tq, gw), lambda b, g, i: (b * nq + i, q_col // gw + g)),
                  pl.BlockSpec((S * N_KV_HEADS, HEAD_DIM), lambda b, g, i: (b, 0)),
                  pl.BlockSpec((S * N_KV_HEADS, HEAD_DIM), lambda b, g, i: (b, 0))],
        out_specs=pl.BlockSpec((tq, gw), lambda b, g, i: (b * nq + i, g)),
        out_shape=jax.ShapeDtypeStruct((B * S, N_HEADS * HEAD_DIM), BF16),
        scratch_shapes=[pltpu.VMEM((nblk, MOBA_BLOCK, HEAD_DIM), BF16),
                        pltpu.VMEM((nblk, HEAD_DIM, MOBA_BLOCK), BF16),
                        pltpu.VMEM((nblk, HEAD_DIM), F32),
                        pltpu.VMEM((2 * nblk, HEAD_DIM), BF16),
                        pltpu.VMEM((nblk, n), F32),
                        pltpu.VMEM((HEAD_DIM, n), BF16),
                        pltpu.VMEM((2, MOBA_BLOCK, n), F32),
                        pltpu.VMEM((1, n), F32),
                        pltpu.VMEM((1, n), F32),
                        pltpu.VMEM((HEAD_DIM, n), F32)],
        compiler_params=_params("arbitrary", "arbitrary", "arbitrary"),
        name="moba_prompt",
    )(qarr, k4, v4)


def _attn_decode_kernel(pt_ref, q_ref, kn_ref, vn_ref, *refs, pps, ns, nblk, ds, nseq, scale):
    del pt_ref
    k_refs = refs[:pps]
    v_refs = refs[pps:2 * pps]
    o_ref = refs[2 * pps]
    wf_ref, wb_ref, s_ref, p_ref, kmean_ref, l_ref, acc_ref = refs[2 * pps + 1:]
    r = pl.program_id(0)
    step = pl.program_id(1)
    bps = pps // 2
    kvw = N_KV_HEADS * HEAD_DIM
    nrow = N_HEADS * ds
    has_values = r >= 1
    has_keys = r < nseq

    def by_head(ref):
        tokens = ref.shape[0] // N_KV_HEADS
        return jnp.concatenate([ref[pl.ds(g, tokens, stride=N_KV_HEADS), :]
                                for g in range(N_KV_HEADS)], axis=1)

    def block_of(page_refs, jj):
        return jnp.concatenate([by_head(page_refs[2 * jj]), by_head(page_refs[2 * jj + 1])], axis=0)

    @pl.when(has_values)
    def _():
        for jj in range(bps):
            j = step * bps + jj
            vblk = block_of(v_refs, jj).astype(BF16)
            acc_ref[...] += jnp.dot(p_ref[j], vblk, preferred_element_type=F32)

    @pl.when(jnp.logical_and(has_values, step == ns - 1))
    def _():
        acc = acc_ref[...] * (1.0 / l_ref[...])
        outs = []
        for h in range(N_HEADS):
            g = h // KV_GROUP
            outs.append(acc[h * ds:(h + 1) * ds, g * HEAD_DIM:(g + 1) * HEAD_DIM])
        o_ref[...] = jnp.concatenate(outs, axis=1)

    @pl.when(jnp.logical_and(has_keys, step == 0))
    def _():
        q = q_ref[...]
        zero = jnp.zeros((ds, HEAD_DIM), F32)
        rows = []
        for h in range(N_HEADS):
            pieces = [zero] * N_KV_HEADS
            pieces[h // KV_GROUP] = q[:, h * HEAD_DIM:(h + 1) * HEAD_DIM]
            rows.append(jnp.concatenate(pieces, axis=1))
        wf = jnp.concatenate(rows, axis=0)
        wf_ref[...] = wf
        wb_ref[...] = (wf * scale).astype(BF16)

    @pl.when(has_keys)
    def _():
        for jj in range(bps):
            j = step * bps + jj
            kblk = block_of(k_refs, jj)
            kmean_ref[pl.ds(j, 1), :] = jnp.mean(kblk, axis=0, keepdims=True)
            s_ref[j] = lax.dot_general(wb_ref[...], kblk.astype(BF16), _NT,
                                       preferred_element_type=F32)

    @pl.when(jnp.logical_and(has_keys, step == ns - 1))
    def _():
        gate = lax.dot_general(kmean_ref[...], wf_ref[...], _NT, precision=lax.Precision.HIGHEST,
                               preferred_element_type=F32)
        blk = lax.broadcasted_iota(jnp.int32, gate.shape, 0)
        sel = _topk_select(gate, blk, blk < nblk, axis=0)
        bias = jnp.concatenate([jnp.where(sel, 0.0, NEG_INF),
                                jnp.zeros((nrow - nblk, nrow), F32)], axis=0).T

        pad = jnp.zeros((nrow - ds, kvw), F32)
        knp = jnp.concatenate([by_head(kn_ref), pad], axis=0).astype(BF16)
        sn = lax.dot_general(wb_ref[...], knp, _NT, preferred_element_type=F32)
        qry_t = lax.broadcasted_iota(jnp.int32, sn.shape, 0) % ds
        key_t = lax.broadcasted_iota(jnp.int32, sn.shape, 1)
        sn = jnp.where(key_t <= qry_t, sn, NEG_INF)

        mx = jnp.full((nrow, MOBA_BLOCK), NEG_INF, F32)
        for j in range(nblk):
            sb = s_ref[j] + bias[:, j:j + 1]
            s_ref[j] = sb
            mx = jnp.maximum(mx, sb)
        m = jnp.maximum(jnp.max(mx, axis=1, keepdims=True), jnp.max(sn, axis=1, keepdims=True))
        lsum = jnp.zeros((nrow, MOBA_BLOCK), F32)
        for j in range(nblk):
            p = jnp.exp(s_ref[j] - m)
            lsum = lsum + p
            p_ref[j] = p.astype(BF16)
        pn = jnp.exp(sn - m)
        l_ref[...] = jnp.sum(lsum, axis=1, keepdims=True) + jnp.sum(pn, axis=1, keepdims=True)
        vnp = jnp.concatenate([by_head(vn_ref), pad], axis=0).astype(BF16)
        acc_ref[...] = jnp.dot(pn.astype(BF16), vnp, preferred_element_type=F32)


def _attn_decode(qarr, q_col, k4, v4, cache_k, cache_v, layer, page_table, DB, DS):
    depth, n_phys, page, nkv, hd = cache_k.shape
    n_pages = page_table.shape[1]
    past_len = n_pages * page
    kvw = nkv * hd
    assert (nkv, hd) == (N_KV_HEADS, HEAD_DIM) and MOBA_BLOCK == 2 * page
    assert past_len % MOBA_BLOCK == 0 and DS == SUBLANES
    nblk = past_len // MOBA_BLOCK
    pps = 16
    assert n_pages % pps == 0
    ns = n_pages // pps
    nrow = N_HEADS * DS
    ck = cache_k.reshape(depth * n_phys, page * nkv, hd)
    cv = cache_v.reshape(depth * n_phys, page * nkv, hd)
    base = layer * n_phys

    kseq = lambda r: jnp.minimum(r, DB - 1)
    vseq = lambda r: jnp.maximum(r - 1, 0)

    def k_map(i):
        return lambda r, s, pt: (base + pt[kseq(r), s * pps + i], 0, 0)

    def v_map(i):
        return lambda r, s, pt: (base + pt[vseq(r), s * pps + i], 0, 0)

    qw = N_HEADS * HEAD_DIM
    in_specs = [pl.BlockSpec((DS, qw), lambda r, s, pt: (kseq(r), q_col // qw)),
                pl.BlockSpec((DS * nkv, hd), lambda r, s, pt: (kseq(r), 0)),
                pl.BlockSpec((DS * nkv, hd), lambda r, s, pt: (kseq(r), 0))]
    in_specs += [pl.BlockSpec((None, page * nkv, hd), k_map(i)) for i in range(pps)]
    in_specs += [pl.BlockSpec((None, page * nkv, hd), v_map(i)) for i in range(pps)]
    kern = functools.partial(_attn_decode_kernel, pps=pps, ns=ns, nblk=nblk, ds=DS, nseq=DB,
                             scale=HEAD_DIM ** -0.5)
    grid_spec = pltpu.PrefetchScalarGridSpec(
        num_scalar_prefetch=1,
        grid=(DB + 1, ns),
        in_specs=in_specs,
        out_specs=pl.BlockSpec((DS, qw), lambda r, s, pt: (vseq(r), 0)),
        scratch_shapes=[pltpu.VMEM((nrow, kvw), F32),
                        pltpu.VMEM((nrow, kvw), BF16),
                        pltpu.VMEM((nblk, nrow, MOBA_BLOCK), F32),
                        pltpu.VMEM((nblk, nrow, MOBA_BLOCK), BF16),
                        pltpu.VMEM((nblk, kvw), F32),
                        pltpu.VMEM((nrow, 1), F32),
                        pltpu.VMEM((nrow, kvw), F32)])
    return pl.pallas_call(
        kern,
        grid_spec=grid_spec,
        out_shape=jax.ShapeDtypeStruct((DB * DS, qw), F32),
        compiler_params=_params("arbitrary", "arbitrary"),
        name="moba_decode",
    )(page_table, qarr, k4, v4, *([ck] * pps), *([cv] * pps))


def _rglru_kernel(xr_ref, yg_ref, h0_ref, cb_ref, cw_ref, cbias_ref, wa_ref, ba_ref, wx_ref, bx_ref,
                  lam_ref, rnn_ref, hlast_ref, cnew_ref, halo_ref, hcarry_ref, *, L):
    c = pl.program_id(1)
    C = xr_ref.shape[1]

    @pl.when(c == 0)
    def _():
        hcarry_ref[...] = h0_ref[...]
        halo_ref[...] = jnp.concatenate(
            [jnp.zeros((SUBLANES - (CONV_W - 1), C), F32), cb_ref[...]], axis=0)

    G = L // SUBLANES
    xr = xr_ref[...]
    xg = jnp.concatenate([halo_ref[...], xr], axis=0).reshape(G + 1, SUBLANES, C)
    rin = lax.broadcasted_iota(jnp.int32, (1, SUBLANES, 1), 1)
    cw = cw_ref[...]
    xc = cbias_ref[...].reshape(1, 1, C)
    for j in range(CONV_W):
        k = CONV_W - 1 - j
        if k == 0:
            xk = xg[1:]
        else:
            rolled = pltpu.roll(xg, k, 1)
            xk = jnp.where(rin >= k, rolled[1:], rolled[:-1])
        xc = xc + xk * cw[j:j + 1, :].reshape(1, 1, C)
    xc = xc.reshape(L, C)

    xcb = xc.astype(BF16)
    bw = C // RNN_BLOCKS
    r_parts, i_parts = [], []
    for n in range(RNN_BLOCKS):
        xs = xcb[:, n * bw:(n + 1) * bw]
        r_parts.append(jnp.dot(xs, wa_ref[n], preferred_element_type=F32))
        i_parts.append(jnp.dot(xs, wx_ref[n], preferred_element_type=F32))
    r = jax.nn.sigmoid(jnp.concatenate(r_parts, axis=1) + ba_ref[...])
    ig = jax.nn.sigmoid(jnp.concatenate(i_parts, axis=1) + bx_ref[...])

    nl = -lam_ref[...]
    softplus = jnp.maximum(nl, 0.0) + jnp.log1p(jnp.exp(-jnp.abs(nl)))
    log_a = -LRU_C * r * softplus
    a = jnp.exp(log_a)
    gain2 = -jnp.tanh(log_a) * (a * a + 1.0)
    gain = jnp.where(gain2 > 0.0, gain2 * lax.rsqrt(gain2), 0.0)
    bv = gain * (ig * xc)

    a = a.reshape(G, SUBLANES, C)
    bv = bv.reshape(G, SUBLANES, C)
    d = 1
    while d < SUBLANES:
        same_group = rin >= d
        bv = a * jnp.where(same_group, pltpu.roll(bv, d, 1), 0.0) + bv
        a = a * jnp.where(same_group, pltpu.roll(a, d, 1), 1.0)
        d *= 2
    carry = hcarry_ref[...]
    groups = []
    for g in range(G):
        hg = bv[g] + a[g] * carry
        groups.append(hg)
        carry = hg[SUBLANES - 1:SUBLANES, :]
    h = groups[0] if G == 1 else jnp.concatenate(groups, axis=0)
    hcarry_ref[...] = carry
    hlast_ref[...] = carry

    yg = yg_ref[...].astype(F32)
    cdf = 0.5 * (1.0 + jnp.tanh(0.7978845608028654 * (yg + 0.044715 * (yg * yg * yg))))
    rnn_ref[...] = (h * (yg * cdf)).astype(rnn_ref.dtype)

    halo_ref[...] = xr[L - SUBLANES:L, :]
    cnew_ref[...] = xr[L - (CONV_W - 1):L, :]


def _rglru(xr_arr, xr_col, yg_arr, yg_col, B, S, h0, conv_buf, conv_w, conv_b, wa, ba, wx, bx, lam):
    C = h0.shape[-1]
    L = min(S, 256)
    assert S % L == 0 and L % SUBLANES == 0 and xr_col % C == 0 and yg_col % C == 0
    nc = S // L
    out_dtype = BF16 if L % 16 == 0 else F32
    row = lambda v: v.reshape(1, C)
    const2 = lambda b, c: (0, 0)
    const3 = lambda b, c: (0, 0, 0)
    rnn, hlast, cnew = pl.pallas_call(
        functools.partial(_rglru_kernel, L=L),
        grid=(B, nc),
        in_specs=[pl.BlockSpec((L, C), lambda b, c: (b * nc + c, xr_col // C)),
                  pl.BlockSpec((L, C), lambda b, c: (b * nc + c, yg_col // C)),
                  pl.BlockSpec((None, 1, C), lambda b, c: (b, 0, 0)),
                  pl.BlockSpec((None, CONV_W - 1, C), lambda b, c: (b, 0, 0)),
                  pl.BlockSpec((CONV_W, C), const2),
                  pl.BlockSpec((1, C), const2),
                  pl.BlockSpec((RNN_BLOCKS, C // RNN_BLOCKS, C // RNN_BLOCKS), const3),
                  pl.BlockSpec((1, C), const2),
                  pl.BlockSpec((RNN_BLOCKS, C // RNN_BLOCKS, C // RNN_BLOCKS), const3),
                  pl.BlockSpec((1, C), const2),
                  pl.BlockSpec((1, C), const2)],
        out_specs=[pl.BlockSpec((L, C), lambda b, c: (b * nc + c, 0)),
                   pl.BlockSpec((None, 1, C), lambda b, c: (b, 0, 0)),
                   pl.BlockSpec((None, CONV_W - 1, C), lambda b, c: (b, 0, 0))],
        out_shape=[jax.ShapeDtypeStruct((B * S, C), out_dtype),
                   jax.ShapeDtypeStruct((B, 1, C), F32),
                   jax.ShapeDtypeStruct((B, CONV_W - 1, C), F32)],
        scratch_shapes=[pltpu.VMEM((SUBLANES, C), F32),
                        pltpu.VMEM((1, C), F32)],
        compiler_params=_params("arbitrary", "arbitrary"),
        name="rglru",
    )(xr_arr, yg_arr, h0.reshape(B, 1, C), conv_buf, conv_w, row(conv_b), wa, row(ba), wx, row(bx), row(lam))
    return rnn, hlast.reshape(B, C), cnew


def _merge_kernel(rnn_ref, attn_ref, u_ref, wa_ref, wb_ref, wga_ref, wgb_ref, bga_ref, bgb_ref, o_ref):
    u = u_ref[...]
    a = jnp.dot(rnn_ref[...].astype(BF16), wa_ref[...], preferred_element_type=F32)
    b = jnp.dot(attn_ref[...].astype(BF16), wb_ref[...], preferred_element_type=F32)
    ga = jax.nn.sigmoid(jnp.dot(u, wga_ref[...], preferred_element_type=F32) + bga_ref[...])
    gb = jax.nn.sigmoid(jnp.dot(u, wgb_ref[...], preferred_element_type=F32) + bgb_ref[...])
    o_ref[...] = (ga * a + gb * b).astype(o_ref.dtype)


def _merge(rnn, attn, u, wa, wb, w_in, gate_col, b_gate):
    T, D = u.shape
    tm = _row_tile(T, 1024)
    tn = COL_CHUNK
    nb = D // tn
    assert gate_col % tn == 0
    act = lambda m, n: (m, 0)
    col = lambda m, n: (0, n)
    col_hi = lambda m, n: (0, n + nb)
    gcol = lambda m, n: (0, gate_col // tn + n)
    gcol_hi = lambda m, n: (0, gate_col // tn + nb + n)
    bg = b_gate.reshape(1, 2 * D)
    return pl.pallas_call(
        _merge_kernel,
        grid=(T // tm, nb),
        in_specs=[pl.BlockSpec((tm, D), act), pl.BlockSpec((tm, D), act), pl.BlockSpec((tm, D), act),
                  pl.BlockSpec((D, tn), col), pl.BlockSpec((D, tn), col),
                  pl.BlockSpec((D, tn), gcol), pl.BlockSpec((D, tn), gcol_hi),
                  pl.BlockSpec((1, tn), col), pl.BlockSpec((1, tn), col_hi)],
        out_specs=pl.BlockSpec((tm, tn), lambda m, n: (m, n)),
        out_shape=jax.ShapeDtypeStruct((T, D), BF16),
        compiler_params=_params("arbitrary", "arbitrary"),
        name="merge",
    )(rnn, attn, u, wa, wb, w_in, w_in, bg, bg)


def _outproj_kernel(mg_ref, x_ref, w_ref, g_ref, x2_ref, u2_ref):
    mg = mg_ref[...]
    D = x_ref.shape[1]
    ssq = jnp.zeros((x_ref.shape[0], 1), F32)
    for c in range(D // COL_CHUNK):
        sl = slice(c * COL_CHUNK, (c + 1) * COL_CHUNK)
        y = x_ref[:, sl] + jnp.dot(mg, w_ref[:, sl], preferred_element_type=F32)
        x2_ref[:, sl] = y
        ssq = ssq + jnp.sum(y * y, axis=-1, keepdims=True)
    inv = lax.rsqrt(ssq / D + RMS_EPS)
    u2_ref[...] = (x2_ref[...] * inv * g_ref[...]).astype(u2_ref.dtype)


def _outproj(merged, x, w_out, g):
    T, D = x.shape
    tm = _row_tile(T, 512)
    row = lambda m: (m, 0)
    return pl.pallas_call(
        _outproj_kernel,
        grid=(T // tm,),
        in_specs=[pl.BlockSpec((tm, D), row), pl.BlockSpec((tm, D), row),
                  pl.BlockSpec((D, D), lambda m: (0, 0)), pl.BlockSpec((1, D), lambda m: (0, 0))],
        out_specs=[pl.BlockSpec((tm, D), row), pl.BlockSpec((tm, D), row)],
        out_shape=[jax.ShapeDtypeStruct((T, D), F32), jax.ShapeDtypeStruct((T, D), BF16)],
        compiler_params=_params("arbitrary"),
        name="out_proj",
    )(merged, x, w_out, g.reshape(1, D))


def _ffn_kernel(u2_ref, x2_ref, w1_ref, w2_ref, gf_ref, y_ref, *, nf, final_norm):
    f = pl.program_id(1)

    @pl.when(f == 0)
    def _():
        y_ref[...] = x2_ref[...]

    h = jnp.dot(u2_ref[...], w1_ref[...], preferred_element_type=F32)
    h = jnp.square(jnp.maximum(h, 0.0)).astype(BF16)
    for c in range(y_ref.shape[1] // COL_CHUNK):
        sl = slice(c * COL_CHUNK, (c + 1) * COL_CHUNK)
        y_ref[:, sl] += jnp.dot(h, w2_ref[:, sl], preferred_element_type=F32)

    if final_norm:
        @pl.when(f == nf - 1)
        def _():
            x3 = y_ref[...]
            ms = jnp.mean(x3 * x3, axis=-1, keepdims=True)
            y_ref[...] = x3 * lax.rsqrt(ms + RMS_EPS) * gf_ref[...]


def _ffn(u2, x2, w1, w2, g_final, final_norm):
    T, D = x2.shape
    F = w1.shape[1]
    tm = _row_tile(T, 1024)
    tf = 512 if tm == 1024 else 1024
    nf = F // tf
    row = lambda m, f: (m, 0)
    return pl.pallas_call(
        functools.partial(_ffn_kernel, nf=nf, final_norm=final_norm),
        grid=(T // tm, nf),
        in_specs=[pl.BlockSpec((tm, D), row), pl.BlockSpec((tm, D), row),
                  pl.BlockSpec((D, tf), lambda m, f: (0, f)), pl.BlockSpec((tf, D), lambda m, f: (f, 0)),
                  pl.BlockSpec((1, D), lambda m, f: (0, 0))],
        out_specs=pl.BlockSpec((tm, D), row),
        out_shape=jax.ShapeDtypeStruct((T, D), F32),
        compiler_params=_params("arbitrary", "arbitrary"),
        name="ffn",
    )(u2, x2, w1, w2, g_final.reshape(1, D))


def _rope_tables(pos):
    half = HEAD_DIM // 2
    inv = ROPE_THETA ** (-jnp.arange(half, dtype=F32) * (2.0 / HEAD_DIM))
    ang = pos.astype(F32)[:, None] * inv[None, :]
    cos, sin = jnp.cos(ang), jnp.sin(ang)
    return jnp.concatenate([cos, cos], axis=-1), jnp.concatenate([-sin, sin], axis=-1)


def _decoder_layer(x, B, S, cos, sin, past, h0, conv_buf, w, final_norm):
    C = h0.shape[-1]
    attn_w = N_HEADS * HEAD_DIM
    act_dtype = BF16 if S % 16 == 0 else F32

    kv_w = N_KV_HEADS * HEAD_DIM
    q_col, k_col, v_col = 0, attn_w, attn_w + kv_w
    xr_col = v_col + kv_w
    yg_col = xr_col + C
    gate_col = yg_col + C
    w_in = w["w_in"]

    u, xr = _norm_proj(x, w["norm_mix"], w_in, xr_col, C, F32)
    ygq = _proj(u, w_in, [(yg_col, C), (q_col, attn_w)], cos, sin,
                (C, C + attn_w), act_dtype)
    k4, v4 = _kv_proj(u, w_in, k_col, v_col, cos, sin)
    if past is None:
        attn = _attn_prompt(ygq, C, k4, v4, B, S)
    else:
        attn = _attn_decode(ygq, C, k4, v4, past[0], past[1], past[2], past[3], B, S)
    rnn, h_last, conv_new = _rglru(xr, 0, ygq, 0, B, S, h0, conv_buf, w["conv_w"], w["conv_b"],
                                   w["w_rg_a"], w["b_rg_a"], w["w_rg_x"], w["b_rg_x"], w["lru_lambda"])
    merged = _merge(rnn, attn, u, w["w_proj_a"], w["w_proj_b"], w_in, gate_col, w["b_gate"])
    x2, u2 = _outproj(merged, x, w["w_out"], w["norm_mlp"])
    y = _ffn(u2, x2, w["w_ff1"], w["w_ff2"], w["norm_final"], final_norm)
    k = k4.reshape(B, S, N_KV_HEADS, HEAD_DIM)
    v = v4.reshape(B, S, N_KV_HEADS, HEAD_DIM)
    return y, k, v, h_last, conv_new


def kernel(x_prompt, x_sample, cache_k, cache_v, state_h, state_conv, page_table, norm_mix, w_in, b_gate, conv_w, conv_b, w_rg_a, b_rg_a, w_rg_x, b_rg_x, lru_lambda, w_proj_a, w_proj_b, w_out, norm_mlp, w_ff1, w_ff2, norm_final):
    B, S, D = x_prompt.shape
    DB, DS, _ = x_sample.shape
    depth = w_in.shape[0]
    C = state_h.shape[-1]
    past_len = page_table.shape[1] * cache_k.shape[2]

    cos_p, sin_p = _rope_tables(jnp.arange(S, dtype=jnp.int32))
    cos_s, sin_s = _rope_tables(past_len + jnp.arange(DS, dtype=jnp.int32))
    cos_s, sin_s = jnp.tile(cos_s, (DB, 1)), jnp.tile(sin_s, (DB, 1))

    hp = x_prompt.reshape(B * S, D)
    hs = x_sample.reshape(DB * DS, D)
    outs = [[] for _ in range(8)]
    for l in range(depth):
        wl = w_in[l]
        w = dict(
            norm_mix=norm_mix[l], norm_mlp=norm_mlp[l], norm_final=norm_final,
            w_in=wl.astype(BF16), b_gate=b_gate[l],
            conv_w=conv_w[l], conv_b=conv_b[l],
            w_rg_a=w_rg_a[l].astype(BF16), b_rg_a=b_rg_a[l],
            w_rg_x=w_rg_x[l].astype(BF16), b_rg_x=b_rg_x[l], lru_lambda=lru_lambda[l],
            w_proj_a=w_proj_a[l].astype(BF16), w_proj_b=w_proj_b[l].astype(BF16),
            w_out=w_out[l].astype(BF16), w_ff1=w_ff1[l].astype(BF16), w_ff2=w_ff2[l].astype(BF16))
        final = l == depth - 1
        h0_p = jnp.zeros((B, C), F32)
        buf_p = jnp.zeros((B, CONV_W - 1, C), F32)
        hp, k1, v1, r1, c1 = _decoder_layer(hp, B, S, cos_p, sin_p, None, h0_p, buf_p, w, final)
        past = (cache_k, cache_v, l, page_table)
        hs, k2, v2, r2, c2 = _decoder_layer(hs, DB, DS, cos_s, sin_s, past, state_h[l], state_conv[l], w, final)
        for lst, val in zip(outs, (k1, v1, r1, c1, k2, v2, r2, c2)):
            lst.append(val)
    return (hp.reshape(B, S, D), hs.reshape(DB, DS, D), *[jnp.stack(o) for o in outs])
```

```python
import functools

import jax
import jax.numpy as jnp
from jax import lax
from jax.experimental import pallas as pl
from jax.experimental.pallas import tpu as pltpu

F32 = jnp.float32
BF16 = jnp.bfloat16

N_HEADS = 16
N_KV_HEADS = 4
HEAD_DIM = 128
KV_GROUP = N_HEADS // N_KV_HEADS
ROPE_THETA = 10000.0
MOBA_BLOCK = 256
MOBA_TOPK = 3
RNN_BLOCKS = 16
CONV_W = 4
LRU_C = 8.0
RMS_EPS = 1e-6
NEG_INF = -1e30
LOG2_E = 1.4426950408889634

LANES = 128
SUBLANES = 8
VMEM_LIMIT_BYTES = 56 * 1024 * 1024
COL_CHUNK = 512

_NT = (((1,), (1,)), ((), ()))


def _params(*sem):
    return pltpu.CompilerParams(dimension_semantics=sem, vmem_limit_bytes=VMEM_LIMIT_BYTES)


def _row_tile(rows, pref):
    t = min(rows, pref)
    assert rows % t == 0, (rows, pref)
    return t


def _proj_kernel(u_ref, w_ref, cos_ref, sin_ref, o_ref, *, rope_lo, rope_hi):
    acc = jnp.dot(u_ref[...], w_ref[...], preferred_element_type=F32)
    n = pl.program_id(1)
    roped = jnp.logical_and(n >= rope_lo, n < rope_hi)

    @pl.when(roped)
    def _():
        cos = cos_ref[...]
        sin = sin_ref[...]
        for c in range(acc.shape[1] // HEAD_DIM):
            sl = slice(c * HEAD_DIM, (c + 1) * HEAD_DIM)
            o_ref[:, sl] = _rope(acc[:, sl], cos, sin).astype(o_ref.dtype)

    @pl.when(jnp.logical_not(roped))
    def _():
        o_ref[...] = acc.astype(o_ref.dtype)


def _rope(xh, cos, sin):
    return xh * cos + pltpu.roll(xh, HEAD_DIM // 2, 1) * sin


def _segment_block(segments, tn):
    assert all(s % tn == 0 and w % tn == 0 for s, w in segments)

    def block(n):
        idx, first = None, 0
        for start, width in segments:
            here = start // tn + n - first
            idx = here if idx is None else jnp.where(n >= first, here, idx)
            first += width // tn
        return idx

    return block


def _proj(u, w, segments, cos, sin, rope_cols, out_dtype):
    T, D = u.shape
    N = sum(width for _, width in segments)
    tm = _row_tile(T, 1024)
    tn = 2 * COL_CHUNK
    n_tab = cos.shape[0] // tm
    assert cos.shape[0] % tm == 0 and rope_cols[0] % tn == 0 and rope_cols[1] % tn == 0
    kern = functools.partial(_proj_kernel, rope_lo=rope_cols[0] // tn, rope_hi=rope_cols[1] // tn)
    wblock = _segment_block(segments, tn)
    return pl.pallas_call(
        kern,
        grid=(T // tm, N // tn),
        in_specs=[pl.BlockSpec((tm, D), lambda m, n: (m, 0)),
                  pl.BlockSpec((D, tn), lambda m, n: (0, wblock(n))),
                  pl.BlockSpec((tm, HEAD_DIM), lambda m, n: (m % n_tab, 0)),
                  pl.BlockSpec((tm, HEAD_DIM), lambda m, n: (m % n_tab, 0))],
        out_specs=pl.BlockSpec((tm, tn), lambda m, n: (m, n)),
        out_shape=jax.ShapeDtypeStruct((T, N), out_dtype),
        compiler_params=_params("arbitrary", "arbitrary"),
        name="in_proj",
    )(u, w, cos, sin)


def _norm_proj_kernel(x_ref, g_ref, w_ref, u_ref, o_ref):
    @pl.when(pl.program_id(1) == 0)
    def _():
        x = x_ref[...]
        ms = jnp.mean(x * x, axis=-1, keepdims=True)
        u_ref[...] = (x * lax.rsqrt(ms + RMS_EPS) * g_ref[...]).astype(u_ref.dtype)

    o_ref[...] = jnp.dot(u_ref[...], w_ref[...], preferred_element_type=F32).astype(o_ref.dtype)


def _norm_proj(x, g, w, col, width, out_dtype):
    T, D = x.shape
    tm = _row_tile(T, 1024)
    tn = 2 * COL_CHUNK
    assert col % tn == 0 and width % tn == 0
    return pl.pallas_call(
        _norm_proj_kernel,
        grid=(T // tm, width // tn),
        in_specs=[pl.BlockSpec((tm, D), lambda m, n: (m, 0)),
                  pl.BlockSpec((1, D), lambda m, n: (0, 0)),
                  pl.BlockSpec((D, tn), lambda m, n: (0, col // tn + n))],
        out_specs=[pl.BlockSpec((tm, D), lambda m, n: (m, 0)),
                   pl.BlockSpec((tm, tn), lambda m, n: (m, n))],
        out_shape=[jax.ShapeDtypeStruct((T, D), BF16), jax.ShapeDtypeStruct((T, width), out_dtype)],
        compiler_params=_params("arbitrary", "arbitrary"),
        name="norm_proj",
    )(x, g.reshape(1, D), w)


def _kv_proj_kernel(u_ref, wk_ref, wv_ref, cos_ref, sin_ref, k_ref, v_ref):
    u = u_ref[...]
    tm = u.shape[0]
    cos = cos_ref[...]
    sin = sin_ref[...]
    k = jnp.dot(u, wk_ref[...], preferred_element_type=F32)
    for g in range(N_KV_HEADS):
        k_ref[pl.ds(g, tm, stride=N_KV_HEADS), :] = _rope(k[:, g * HEAD_DIM:(g + 1) * HEAD_DIM], cos, sin)
    v = jnp.dot(u, wv_ref[...], preferred_element_type=F32)
    for g in range(N_KV_HEADS):
        v_ref[pl.ds(g, tm, stride=N_KV_HEADS), :] = v[:, g * HEAD_DIM:(g + 1) * HEAD_DIM]


def _kv_proj(u, w, k_col, v_col, cos, sin):
    T, D = u.shape
    kvw = N_KV_HEADS * HEAD_DIM
    assert k_col % kvw == 0 and v_col % kvw == 0
    tm = _row_tile(T, 1024)
    n_tab = cos.shape[0] // tm
    assert cos.shape[0] % tm == 0
    out = jax.ShapeDtypeStruct((T * N_KV_HEADS, HEAD_DIM), F32)
    out_spec = pl.BlockSpec((tm * N_KV_HEADS, HEAD_DIM), lambda m: (m, 0))
    return pl.pallas_call(
        _kv_proj_kernel,
        grid=(T // tm,),
        in_specs=[pl.BlockSpec((tm, D), lambda m: (m, 0)),
                  pl.BlockSpec((D, kvw), lambda m: (0, k_col // kvw)),
                  pl.BlockSpec((D, kvw), lambda m: (0, v_col // kvw)),
                  pl.BlockSpec((tm, HEAD_DIM), lambda m: (m % n_tab, 0)),
                  pl.BlockSpec((tm, HEAD_DIM), lambda m: (m % n_tab, 0))],
        out_specs=[out_spec, out_spec],
        out_shape=[out, out],
        compiler_params=_params("arbitrary"),
        name="kv_proj",
    )(u, w, w, cos, sin)


def _topk_select(gate, blk, eligible, axis):
    gm = jnp.where(eligible, gate, NEG_INF)
    rank = jnp.zeros(gate.shape, jnp.int32)
    for r in range(gate.shape[axis]):
        gr = gm[r:r + 1, :] if axis == 0 else gm[:, r:r + 1]
        beats = jnp.logical_or(gr > gm, jnp.logical_and(gr == gm, blk > r))
        rank = rank + beats.astype(jnp.int32)
    return jnp.logical_and(rank < MOBA_TOPK, eligible)


def _attn_prompt_kernel(q_ref, k_ref, v_ref, o_ref,
                        kb_ref, vt_ref, kmean_ref, kmh_ref, sel_ref, qt_ref, s_ref, m_ref, l_ref, acc_ref,
                        *, tq, nblk, scale):
    g = pl.program_id(1)
    i = pl.program_id(2)

    @pl.when(i == 0)
    def _():
        for j in range(nblk):
            rows = pl.ds(j * MOBA_BLOCK * N_KV_HEADS + g, MOBA_BLOCK, stride=N_KV_HEADS)
            kj = k_ref[rows, :]
            kb_ref[j] = kj.astype(BF16)
            kmean_ref[j:j + 1, :] = jnp.mean(kj, axis=0, keepdims=True)
            vt_ref[j] = v_ref[rows, :].T.astype(BF16)
        km = kmean_ref[...]
        km_hi = km.astype(BF16)
        kmh_ref[...] = jnp.concatenate([km_hi, (km - km_hi.astype(F32)).astype(BF16)], axis=0)

    q = q_ref[...].astype(F32)
    qs = jnp.concatenate([q[:, h * HEAD_DIM:(h + 1) * HEAD_DIM] for h in range(KV_GROUP)], axis=0)
    qt = (qs * (scale * LOG2_E)).T
    qt_hi = qt.astype(BF16)
    qt_lo = (qt - qt_hi.astype(F32)).astype(BF16)
    qt_ref[...] = qt_hi

    def scores(j):
        return jnp.dot(kb_ref[j], qt_ref[...], preferred_element_type=F32)

    ext = jnp.dot(jnp.concatenate([kb_ref[i], kmh_ref[...]], axis=0), qt_hi,
                  preferred_element_type=F32)
    gate = (ext[MOBA_BLOCK:MOBA_BLOCK + nblk] + ext[MOBA_BLOCK + nblk:]) + jnp.dot(
        kmh_ref[:nblk, :], qt_lo, preferred_element_type=F32)
    s_ref[0] = scores(0)
    blk = lax.broadcasted_iota(jnp.int32, gate.shape, 0)
    sel = _topk_select(gate, blk, blk < i, axis=0)
    sel_ref[...] = sel.astype(F32)

    s = ext[:MOBA_BLOCK]
    key_t = lax.broadcasted_iota(jnp.int32, s.shape, 0)
    qry_t = lax.broadcasted_iota(jnp.int32, s.shape, 1) % tq
    s = jnp.where(key_t <= qry_t, s, NEG_INF)
    m0 = jnp.max(s, axis=0, keepdims=True)
    p = jnp.exp2(s - m0)
    m_ref[...] = m0
    l_ref[...] = jnp.sum(p, axis=0, keepdims=True)
    acc_ref[...] = jnp.dot(vt_ref[i], p.astype(BF16), preferred_element_type=F32)

    def absorb(slot, j):
        s = s_ref[slot]
        picked = sel_ref[pl.ds(j, 1), :] > 0.5
        m_old = m_ref[...]
        m_new = jnp.maximum(m_old, jnp.where(picked, jnp.max(s, axis=0, keepdims=True), NEG_INF))
        alpha = jnp.exp2(m_old - m_new)
        p = jnp.exp2(s - jnp.where(picked, m_new, -NEG_INF))
        m_ref[...] = m_new
        l_ref[...] = alpha * l_ref[...] + jnp.sum(p, axis=0, keepdims=True)
        acc_ref[...] = alpha * acc_ref[...] + jnp.dot(vt_ref[j], p.astype(BF16),
                                                      preferred_element_type=F32)

    def pair(ja):
        s_ref[1] = scores(ja + 1)
        absorb(0, ja)
        s_ref[0] = scores(jnp.minimum(ja + 2, nblk - 1))
        absorb(1, ja + 1)

    def body(t, carry):
        pair(4 * t)
        pair(4 * t + 2)
        return carry

    quads = i // 4
    lax.fori_loop(0, quads, body, 0)

    @pl.when(i % 4 >= 2)
    def _():
        pair(4 * quads)

    @pl.when(i % 2 == 1)
    def _():
        absorb(0, i - 1)

    out = acc_ref[...] * (1.0 / l_ref[...])
    for h in range(KV_GROUP):
        o_ref[:, h * HEAD_DIM:(h + 1) * HEAD_DIM] = out[:, h * tq:(h + 1) * tq].T.astype(o_ref.dtype)


def _attn_prompt(qarr, q_col, k4, v4, B, S):
    tq = MOBA_BLOCK
    assert S % MOBA_BLOCK == 0
    nblk = S // MOBA_BLOCK
    nq = S // tq
    gw = KV_GROUP * HEAD_DIM
    n = KV_GROUP * tq
    kern = functools.partial(_attn_prompt_kernel, tq=tq, nblk=nblk, scale=HEAD_DIM ** -0.5)
    return pl.pallas_call(
        kern,
        grid=(B, N_KV_HEADS, nq),
        in_specs=[pl.BlockSpec((tq, gw), lambda b, g, i: (b * nq + i, q_col // gw + g)),
                  pl.BlockSpec((S * N_KV_HEADS, HEAD_DIM), lambda b, g, i: (b, 0)),
                  pl.BlockSpec((S * N_KV_HEADS, HEAD_DIM), lambda b, g, i: (b, 0))],
        out_specs=pl.BlockSpec((tq, gw), lambda b, g, i: (b * nq + i, g)),
        out_shape=jax.ShapeDtypeStruct((B * S, N_HEADS * HEAD_DIM), BF16),
        scratch_shapes=[pltpu.VMEM((nblk, MOBA_BLOCK, HEAD_DIM), BF16),
                        pltpu.VMEM((nblk, HEAD_DIM, MOBA_BLOCK), BF16),
                        pltpu.VMEM((nblk, HEAD_DIM), F32),
                        pltpu.VMEM((2 * nblk, HEAD_DIM), BF16),
                        pltpu.VMEM((nblk, n), F32),
                        pltpu.VMEM((HEAD_DIM, n), BF16),
                        pltpu.VMEM((2, MOBA_BLOCK, n), F32),
                        pltpu.VMEM((1, n), F32),
                        pltpu.VMEM((1, n), F32),
                        pltpu.VMEM((HEAD_DIM, n), F32)],
        compiler_params=_params("arbitrary", "arbitrary", "arbitrary"),
        name="moba_prompt",
    )(qarr, k4, v4)


def _attn_decode_kernel(pt_ref, q_ref, kn_ref, vn_ref, *refs, pps, ns, nblk, ds, nseq, scale):
    del pt_ref
    k_refs = refs[:pps]
    v_refs = refs[pps:2 * pps]
    o_ref = refs[2 * pps]
    wf_ref, wb_ref, s_ref, p_ref, kmean_ref, l_ref, acc_ref = refs[2 * pps + 1:]
    r = pl.program_id(0)
    step = pl.program_id(1)
    bps = pps // 2
    kvw = N_KV_HEADS * HEAD_DIM
    nrow = N_HEADS * ds
    has_values = r >= 1
    has_keys = r < nseq

    def by_head(ref):
        tokens = ref.shape[0] // N_KV_HEADS
        return jnp.concatenate([ref[pl.ds(g, tokens, stride=N_KV_HEADS), :]
                                for g in range(N_KV_HEADS)], axis=1)

    def block_of(page_refs, jj):
        return jnp.concatenate([by_head(page_refs[2 * jj]), by_head(page_refs[2 * jj + 1])], axis=0)

    @pl.when(has_values)
    def _():
        for jj in range(bps):
            j = step * bps + jj
            vblk = block_of(v_refs, jj).astype(BF16)
            acc_ref[...] += jnp.dot(p_ref[j], vblk, preferred_element_type=F32)

    @pl.when(jnp.logical_and(has_values, step == ns - 1))
    def _():
        acc = acc_ref[...] * (1.0 / l_ref[...])
        outs = []
        for h in range(N_HEADS):
            g = h // KV_GROUP
            outs.append(acc[h * ds:(h + 1) * ds, g * HEAD_DIM:(g + 1) * HEAD_DIM])
        o_ref[...] = jnp.concatenate(outs, axis=1)

    @pl.when(jnp.logical_and(has_keys, step == 0))
    def _():
        q = q_ref[...]
        zero = jnp.zeros((ds, HEAD_DIM), F32)
        rows = []
        for h in range(N_HEADS):
            pieces = [zero] * N_KV_HEADS
            pieces[h // KV_GROUP] = q[:, h * HEAD_DIM:(h + 1) * HEAD_DIM]
            rows.append(jnp.concatenate(pieces, axis=1))
        wf = jnp.concatenate(rows, axis=0)
        wf_ref[...] = wf
        wb_ref[...] = (wf * scale).astype(BF16)

    @pl.when(has_keys)
    def _():
        for jj in range(bps):
            j = step * bps + jj
            kblk = block_of(k_refs, jj)
            kmean_ref[pl.ds(j, 1), :] = jnp.mean(kblk, axis=0, keepdims=True)
            s_ref[j] = lax.dot_general(wb_ref[...], kblk.astype(BF16), _NT,
                                       preferred_element_type=F32)

    @pl.when(jnp.logical_and(has_keys, step == ns - 1))
    def _():
        gate = lax.dot_general(kmean_ref[...], wf_ref[...], _NT, precision=lax.Precision.HIGHEST,
                               preferred_element_type=F32)
        blk = lax.broadcasted_iota(jnp.int32, gate.shape, 0)
        sel = _topk_select(gate, blk, blk < nblk, axis=0)
        bias = jnp.concatenate([jnp.where(sel, 0.0, NEG_INF),
                                jnp.zeros((nrow - nblk, nrow), F32)], axis=0).T

        pad = jnp.zeros((nrow - ds, kvw), F32)
        knp = jnp.concatenate([by_head(kn_ref), pad], axis=0).astype(BF16)
        sn = lax.dot_general(wb_ref[...], knp, _NT, preferred_element_type=F32)
        qry_t = lax.broadcasted_iota(jnp.int32, sn.shape, 0) % ds
        key_t = lax.broadcasted_iota(jnp.int32, sn.shape, 1)
        sn = jnp.where(key_t <= qry_t, sn, NEG_INF)

        mx = jnp.full((nrow, MOBA_BLOCK), NEG_INF, F32)
        for j in range(nblk):
            sb = s_ref[j] + bias[:, j:j + 1]
            s_ref[j] = sb
            mx = jnp.maximum(mx, sb)
        m = jnp.maximum(jnp.max(mx, axis=1, keepdims=True), jnp.max(sn, axis=1, keepdims=True))
        lsum = jnp.zeros((nrow, MOBA_BLOCK), F32)
        for j in range(nblk):
            p = jnp.exp(s_ref[j] - m)
            lsum = lsum + p
            p_ref[j] = p.astype(BF16)
        pn = jnp.exp(sn - m)
        l_ref[...] = jnp.sum(lsum, axis=1, keepdims=True) + jnp.sum(pn, axis=1, keepdims=True)
        vnp = jnp.concatenate([by_head(vn_ref), pad], axis=0).astype(BF16)
        acc_ref[...] = jnp.dot(pn.astype(BF16), vnp, preferred_element_type=F32)


def _attn_decode(qarr, q_col, k4, v4, cache_k, cache_v, layer, page_table, DB, DS):
    depth, n_phys, page, nkv, hd = cache_k.shape
    n_pages = page_table.shape[1]
    past_len = n_pages * page
    kvw = nkv * hd
    assert (nkv, hd) == (N_KV_HEADS, HEAD_DIM) and MOBA_BLOCK == 2 * page
    assert past_len % MOBA_BLOCK == 0 and DS == SUBLANES
    nblk = past_len // MOBA_BLOCK
    pps = 32
    assert n_pages % pps == 0
    ns = n_pages // pps
    nrow = N_HEADS * DS
    ck = cache_k.reshape(depth * n_phys, page * nkv, hd)
    cv = cache_v.reshape(depth * n_phys, page * nkv, hd)
    base = layer * n_phys

    kseq = lambda r: jnp.minimum(r, DB - 1)
    vseq = lambda r: jnp.maximum(r - 1, 0)

    def k_map(i):
        return lambda r, s, pt: (base + pt[kseq(r), s * pps + i], 0, 0)

    def v_map(i):
        return lambda r, s, pt: (base + pt[vseq(r), s * pps + i], 0, 0)

    qw = N_HEADS * HEAD_DIM
    in_specs = [pl.BlockSpec((DS, qw), lambda r, s, pt: (kseq(r), q_col // qw)),
                pl.BlockSpec((DS * nkv, hd), lambda r, s, pt: (kseq(r), 0)),
                pl.BlockSpec((DS * nkv, hd), lambda r, s, pt: (kseq(r), 0))]
    in_specs += [pl.BlockSpec((None, page * nkv, hd), k_map(i)) for i in range(pps)]
    in_specs += [pl.BlockSpec((None, page * nkv, hd), v_map(i)) for i in range(pps)]
    kern = functools.partial(_attn_decode_kernel, pps=pps, ns=ns, nblk=nblk, ds=DS, nseq=DB,
                             scale=HEAD_DIM ** -0.5)
    grid_spec = pltpu.PrefetchScalarGridSpec(
        num_scalar_prefetch=1,
        grid=(DB + 1, ns),
        in_specs=in_specs,
        out_specs=pl.BlockSpec((DS, qw), lambda r, s, pt: (vseq(r), 0)),
        scratch_shapes=[pltpu.VMEM((nrow, kvw), F32),
                        pltpu.VMEM((nrow, kvw), BF16),
                        pltpu.VMEM((nblk, nrow, MOBA_BLOCK), F32),
                        pltpu.VMEM((nblk, nrow, MOBA_BLOCK), BF16),
                        pltpu.VMEM((nblk, kvw), F32),
                        pltpu.VMEM((nrow, 1), F32),
                        pltpu.VMEM((nrow, kvw), F32)])
    return pl.pallas_call(
        kern,
        grid_spec=grid_spec,
        out_shape=jax.ShapeDtypeStruct((DB * DS, qw), F32),
        compiler_params=_params("arbitrary", "arbitrary"),
        name="moba_decode",
    )(page_table, qarr, k4, v4, *([ck] * pps), *([cv] * pps))


def _rglru_kernel(xr_ref, yg_ref, h0_ref, cb_ref, cw_ref, cbias_ref, wa_ref, ba_ref, wx_ref, bx_ref,
                  lam_ref, rnn_ref, hlast_ref, cnew_ref, halo_ref, hcarry_ref, *, L):
    c = pl.program_id(1)
    C = xr_ref.shape[1]

    @pl.when(c == 0)
    def _():
        hcarry_ref[...] = h0_ref[...]
        halo_ref[...] = jnp.concatenate(
            [jnp.zeros((SUBLANES - (CONV_W - 1), C), F32), cb_ref[...]], axis=0)

    G = L // SUBLANES
    xr = xr_ref[...]
    xg = jnp.concatenate([halo_ref[...], xr], axis=0).reshape(G + 1, SUBLANES, C)
    rin = lax.broadcasted_iota(jnp.int32, (1, SUBLANES, 1), 1)
    cw = cw_ref[...]
    xc = cbias_ref[...].reshape(1, 1, C)
    for j in range(CONV_W):
        k = CONV_W - 1 - j
        if k == 0:
            xk = xg[1:]
        else:
            rolled = pltpu.roll(xg, k, 1)
            xk = jnp.where(rin >= k, rolled[1:], rolled[:-1])
        xc = xc + xk * cw[j:j + 1, :].reshape(1, 1, C)
    xc = xc.reshape(L, C)

    xcb = xc.astype(BF16)
    bw = C // RNN_BLOCKS
    r_parts, i_parts = [], []
    for n in range(RNN_BLOCKS):
        xs = xcb[:, n * bw:(n + 1) * bw]
        r_parts.append(jnp.dot(xs, wa_ref[n], preferred_element_type=F32))
        i_parts.append(jnp.dot(xs, wx_ref[n], preferred_element_type=F32))
    r = jax.nn.sigmoid(jnp.concatenate(r_parts, axis=1) + ba_ref[...])
    ig = jax.nn.sigmoid(jnp.concatenate(i_parts, axis=1) + bx_ref[...])

    nl = -lam_ref[...]
    softplus = jnp.maximum(nl, 0.0) + jnp.log1p(jnp.exp(-jnp.abs(nl)))
    log_a = -LRU_C * r * softplus
    a = jnp.exp(log_a)
    gain2 = -jnp.tanh(log_a) * (a * a + 1.0)
    gain = jnp.where(gain2 > 0.0, gain2 * lax.rsqrt(gain2), 0.0)
    bv = gain * (ig * xc)

    a = a.reshape(G, SUBLANES, C)
    bv = bv.reshape(G, SUBLANES, C)
    d = 1
    while d < SUBLANES:
        same_group = rin >= d
        bv = a * jnp.where(same_group, pltpu.roll(bv, d, 1), 0.0) + bv
        a = a * jnp.where(same_group, pltpu.roll(a, d, 1), 1.0)
        d *= 2
    carry = hcarry_ref[...]
    groups = []
    for g in range(G):
        hg = bv[g] + a[g] * carry
        groups.append(hg)
        carry = hg[SUBLANES - 1:SUBLANES, :]
    h = groups[0] if G == 1 else jnp.concatenate(groups, axis=0)
    hcarry_ref[...] = carry
    hlast_ref[...] = carry

    yg = yg_ref[...].astype(F32)
    cdf = 0.5 * (1.0 + jnp.tanh(0.7978845608028654 * (yg + 0.044715 * (yg * yg * yg))))
    rnn_ref[...] = (h * (yg * cdf)).astype(rnn_ref.dtype)

    halo_ref[...] = xr[L - SUBLANES:L, :]
    cnew_ref[...] = xr[L - (CONV_W - 1):L, :]


def _rglru(xr_arr, xr_col, yg_arr, yg_col, B, S, h0, conv_buf, conv_w, conv_b, wa, ba, wx, bx, lam):
    C = h0.shape[-1]
    L = min(S, 256)
    assert S % L == 0 and L % SUBLANES == 0 and xr_col % C == 0 and yg_col % C == 0
    nc = S // L
    out_dtype = BF16 if L % 16 == 0 else F32
    row = lambda v: v.reshape(1, C)
    const2 = lambda b, c: (0, 0)
    const3 = lambda b, c: (0, 0, 0)
    rnn, hlast, cnew = pl.pallas_call(
        functools.partial(_rglru_kernel, L=L),
        grid=(B, nc),
        in_specs=[pl.BlockSpec((L, C), lambda b, c: (b * nc + c, xr_col // C)),
                  pl.BlockSpec((L, C), lambda b, c: (b * nc + c, yg_col // C)),
                  pl.BlockSpec((None, 1, C), lambda b, c: (b, 0, 0)),
                  pl.BlockSpec((None, CONV_W - 1, C), lambda b, c: (b, 0, 0)),
                  pl.BlockSpec((CONV_W, C), const2),
                  pl.BlockSpec((1, C), const2),
                  pl.BlockSpec((RNN_BLOCKS, C // RNN_BLOCKS, C // RNN_BLOCKS), const3),
                  pl.BlockSpec((1, C), const2),
                  pl.BlockSpec((RNN_BLOCKS, C // RNN_BLOCKS, C // RNN_BLOCKS), const3),
                  pl.BlockSpec((1, C), const2),
                  pl.BlockSpec((1, C), const2)],
        out_specs=[pl.BlockSpec((L, C), lambda b, c: (b * nc + c, 0)),
                   pl.BlockSpec((None, 1, C), lambda b, c: (b, 0, 0)),
                   pl.BlockSpec((None, CONV_W - 1, C), lambda b, c: (b, 0, 0))],
        out_shape=[jax.ShapeDtypeStruct((B * S, C), out_dtype),
                   jax.ShapeDtypeStruct((B, 1, C), F32),
                   jax.ShapeDtypeStruct((B, CONV_W - 1, C), F32)],
        scratch_shapes=[pltpu.VMEM((SUBLANES, C), F32),
                        pltpu.VMEM((1, C), F32)],
        compiler_params=_params("arbitrary", "arbitrary"),
        name="rglru",
    )(xr_arr, yg_arr, h0.reshape(B, 1, C), conv_buf, conv_w, row(conv_b), wa, row(ba), wx, row(bx), row(lam))
    return rnn, hlast.reshape(B, C), cnew


def _merge_kernel(rnn_ref, attn_ref, u_ref, wa_ref, wb_ref, wga_ref, wgb_ref, bga_ref, bgb_ref, o_ref):
    u = u_ref[...]
    a = jnp.dot(rnn_ref[...].astype(BF16), wa_ref[...], preferred_element_type=F32)
    b = jnp.dot(attn_ref[...].astype(BF16), wb_ref[...], preferred_element_type=F32)
    ga = jax.nn.sigmoid(jnp.dot(u, wga_ref[...], preferred_element_type=F32) + bga_ref[...])
    gb = jax.nn.sigmoid(jnp.dot(u, wgb_ref[...], preferred_element_type=F32) + bgb_ref[...])
    o_ref[...] = (ga * a + gb * b).astype(o_ref.dtype)


def _merge(rnn, attn, u, wa, wb, w_in, gate_col, b_gate):
    T, D = u.shape
    tm = _row_tile(T, 1024)
    tn = COL_CHUNK
    nb = D // tn
    assert gate_col % tn == 0
    act = lambda m, n: (m, 0)
    col = lambda m, n: (0, n)
    col_hi = lambda m, n: (0, n + nb)
    gcol = lambda m, n: (0, gate_col // tn + n)
    gcol_hi = lambda m, n: (0, gate_col // tn + nb + n)
    bg = b_gate.reshape(1, 2 * D)
    return pl.pallas_call(
        _merge_kernel,
        grid=(T // tm, nb),
        in_specs=[pl.BlockSpec((tm, D), act), pl.BlockSpec((tm, D), act), pl.BlockSpec((tm, D), act),
                  pl.BlockSpec((D, tn), col), pl.BlockSpec((D, tn), col),
                  pl.BlockSpec((D, tn), gcol), pl.BlockSpec((D, tn), gcol_hi),
                  pl.BlockSpec((1, tn), col), pl.BlockSpec((1, tn), col_hi)],
        out_specs=pl.BlockSpec((tm, tn), lambda m, n: (m, n)),
        out_shape=jax.ShapeDtypeStruct((T, D), BF16),
        compiler_params=_params("arbitrary", "arbitrary"),
        name="merge",
    )(rnn, attn, u, wa, wb, w_in, w_in, bg, bg)


def _outproj_kernel(mg_ref, x_ref, w_ref, g_ref, x2_ref, u2_ref):
    mg = mg_ref[...]
    D = x_ref.shape[1]
    ssq = jnp.zeros((x_ref.shape[0], 1), F32)
    for c in range(D // COL_CHUNK):
        sl = slice(c * COL_CHUNK, (c + 1) * COL_CHUNK)
        y = x_ref[:, sl] + jnp.dot(mg, w_ref[:, sl], preferred_element_type=F32)
        x2_ref[:, sl] = y
        ssq = ssq + jnp.sum(y * y, axis=-1, keepdims=True)
    inv = lax.rsqrt(ssq / D + RMS_EPS)
    u2_ref[...] = (x2_ref[...] * inv * g_ref[...]).astype(u2_ref.dtype)


def _outproj(merged, x, w_out, g):
    T, D = x.shape
    tm = _row_tile(T, 512)
    row = lambda m: (m, 0)
    return pl.pallas_call(
        _outproj_kernel,
        grid=(T // tm,),
        in_specs=[pl.BlockSpec((tm, D), row), pl.BlockSpec((tm, D), row),
                  pl.BlockSpec((D, D), lambda m: (0, 0)), pl.BlockSpec((1, D), lambda m: (0, 0))],
        out_specs=[pl.BlockSpec((tm, D), row), pl.BlockSpec((tm, D), row)],
        out_shape=[jax.ShapeDtypeStruct((T, D), F32), jax.ShapeDtypeStruct((T, D), BF16)],
        compiler_params=_params("arbitrary"),
        name="out_proj",
    )(merged, x, w_out, g.reshape(1, D))


def _ffn_kernel(u2_ref, x2_ref, w1_ref, w2_ref, gf_ref, y_ref, *, nf, final_norm):
    f = pl.program_id(1)

    @pl.when(f == 0)
    def _():
        y_ref[...] = x2_ref[...]

    h = jnp.dot(u2_ref[...], w1_ref[...], preferred_element_type=F32)
    h = jnp.square(jnp.maximum(h, 0.0)).astype(BF16)
    for c in range(y_ref.shape[1] // COL_CHUNK):
        sl = slice(c * COL_CHUNK, (c + 1) * COL_CHUNK)
        y_ref[:, sl] += jnp.dot(h, w2_ref[:, sl], preferred_element_type=F32)

    if final_norm:
        @pl.when(f == nf - 1)
        def _():
            x3 = y_ref[...]
            ms = jnp.mean(x3 * x3, axis=-1, keepdims=True)
            y_ref[...] = x3 * lax.rsqrt(ms + RMS_EPS) * gf_ref[...]


def _ffn(u2, x2, w1, w2, g_final, final_norm):
    T, D = x2.shape
    F = w1.shape[1]
    tm = _row_tile(T, 1024)
    tf = 512 if tm == 1024 else 1024
    nf = F // tf
    row = lambda m, f: (m, 0)
    return pl.pallas_call(
        functools.partial(_ffn_kernel, nf=nf, final_norm=final_norm),
        grid=(T // tm, nf),
        in_specs=[pl.BlockSpec((tm, D), row), pl.BlockSpec((tm, D), row),
                  pl.BlockSpec((D, tf), lambda m, f: (0, f)), pl.BlockSpec((tf, D), lambda m, f: (f, 0)),
                  pl.BlockSpec((1, D), lambda m, f: (0, 0))],
        out_specs=pl.BlockSpec((tm, D), row),
        out_shape=jax.ShapeDtypeStruct((T, D), F32),
        compiler_params=_params("arbitrary", "arbitrary"),
        name="ffn",
    )(u2, x2, w1, w2, g_final.reshape(1, D))


def _rope_tables(pos):
    half = HEAD_DIM // 2
    inv = ROPE_THETA ** (-jnp.arange(half, dtype=F32) * (2.0 / HEAD_DIM))
    ang = pos.astype(F32)[:, None] * inv[None, :]
    cos, sin = jnp.cos(ang), jnp.sin(ang)
    return jnp.concatenate([cos, cos], axis=-1), jnp.concatenate([-sin, sin], axis=-1)


def _decoder_layer(x, B, S, cos, sin, past, h0, conv_buf, w, final_norm):
    C = h0.shape[-1]
    attn_w = N_HEADS * HEAD_DIM
    act_dtype = BF16 if S % 16 == 0 else F32

    kv_w = N_KV_HEADS * HEAD_DIM
    q_col, k_col, v_col = 0, attn_w, attn_w + kv_w
    xr_col = v_col + kv_w
    yg_col = xr_col + C
    gate_col = yg_col + C
    w_in = w["w_in"]

    u, xr = _norm_proj(x, w["norm_mix"], w_in, xr_col, C, F32)
    ygq = _proj(u, w_in, [(yg_col, C), (q_col, attn_w)], cos, sin,
                (C, C + attn_w), act_dtype)
    k4, v4 = _kv_proj(u, w_in, k_col, v_col, cos, sin)
    if past is None:
        attn = _attn_prompt(ygq, C, k4, v4, B, S)
    else:
        attn = _attn_decode(ygq, C, k4, v4, past[0], past[1], past[2], past[3], B, S)
    rnn, h_last, conv_new = _rglru(xr, 0, ygq, 0, B, S, h0, conv_buf, w["conv_w"], w["conv_b"],
                                   w["w_rg_a"], w["b_rg_a"], w["w_rg_x"], w["b_rg_x"], w["lru_lambda"])
    merged = _merge(rnn, attn, u, w["w_proj_a"], w["w_proj_b"], w_in, gate_col, w["b_gate"])
    x2, u2 = _outproj(merged, x, w["w_out"], w["norm_mlp"])
    y = _ffn(u2, x2, w["w_ff1"], w["w_ff2"], w["norm_final"], final_norm)
    k = k4.reshape(B, S, N_KV_HEADS, HEAD_DIM)
    v = v4.reshape(B, S, N_KV_HEADS, HEAD_DIM)
    return y, k, v, h_last, conv_new


def kernel(x_prompt, x_sample, cache_k, cache_v, state_h, state_conv, page_table, norm_mix, w_in, b_gate, conv_w, conv_b, w_rg_a, b_rg_a, w_rg_x, b_rg_x, lru_lambda, w_proj_a, w_proj_b, w_out, norm_mlp, w_ff1, w_ff2, norm_final):
    B, S, D = x_prompt.shape
    DB, DS, _ = x_sample.shape
    depth = w_in.shape[0]
    C = state_h.shape[-1]
    past_len = page_table.shape[1] * cache_k.shape[2]

    cos_p, sin_p = _rope_tables(jnp.arange(S, dtype=jnp.int32))
    cos_s, sin_s = _rope_tables(past_len + jnp.arange(DS, dtype=jnp.int32))
    cos_s, sin_s = jnp.tile(cos_s, (DB, 1)), jnp.tile(sin_s, (DB, 1))

    hp = x_prompt.reshape(B * S, D)
    hs = x_sample.reshape(DB * DS, D)
    outs = [[] for _ in range(8)]
    for l in range(depth):
        wl = w_in[l]
        w = dict(
            norm_mix=norm_mix[l], norm_mlp=norm_mlp[l], norm_final=norm_final,
            w_in=wl.astype(BF16), b_gate=b_gate[l],
            conv_w=conv_w[l], conv_b=conv_b[l],
            w_rg_a=w_rg_a[l].astype(BF16), b_rg_a=b_rg_a[l],
            w_rg_x=w_rg_x[l].astype(BF16), b_rg_x=b_rg_x[l], lru_lambda=lru_lambda[l],
            w_proj_a=w_proj_a[l].astype(BF16), w_proj_b=w_proj_b[l].astype(BF16),
            w_out=w_out[l].astype(BF16), w_ff1=w_ff1[l].astype(BF16), w_ff2=w_ff2[l].astype(BF16))
        final = l == depth - 1
        h0_p = jnp.zeros((B, C), F32)
        buf_p = jnp.zeros((B, CONV_W - 1, C), F32)
        hp, k1, v1, r1, c1 = _decoder_layer(hp, B, S, cos_p, sin_p, None, h0_p, buf_p, w, final)
        past = (cache_k, cache_v, l, page_table)
        hs, k2, v2, r2, c2 = _decoder_layer(hs, DB, DS, cos_s, sin_s, past, state_h[l], state_conv[l], w, final)
        for lst, val in zip(outs, (k1, v1, r1, c1, k2, v2, r2, c2)):
            lst.append(val)
    return (hp.reshape(B, S, D), hs.reshape(DB, DS, D), *[jnp.stack(o) for o in outs])
```

```python
import functools

import jax
import jax.numpy as jnp
from jax import lax
from jax.experimental import pallas as pl
from jax.experimental.pallas import tpu as pltpu

F32 = jnp.float32
BF16 = jnp.bfloat16

N_HEADS = 16
N_KV_HEADS = 4
HEAD_DIM = 128
KV_GROUP = N_HEADS // N_KV_HEADS
ROPE_THETA = 10000.0
MOBA_BLOCK = 256
MOBA_TOPK = 3
RNN_BLOCKS = 16
CONV_W = 4
LRU_C = 8.0
RMS_EPS = 1e-6
NEG_INF = -1e30
LOG2_E = 1.4426950408889634

LANES = 128
SUBLANES = 8
VMEM_LIMIT_BYTES = 56 * 1024 * 1024
COL_CHUNK = 512

_NT = (((1,), (1,)), ((), ()))


def _params(*sem):
    return pltpu.CompilerParams(dimension_semantics=sem, vmem_limit_bytes=VMEM_LIMIT_BYTES)


def _row_tile(rows, pref):
    t = min(rows, pref)
    assert rows % t == 0, (rows, pref)
    return t


def _proj_kernel(u_ref, w_ref, cos_ref, sin_ref, o_ref, *, rope_lo, rope_hi):
    acc = jnp.dot(u_ref[...], w_ref[...], preferred_element_type=F32)
    n = pl.program_id(1)
    roped = jnp.logical_and(n >= rope_lo, n < rope_hi)

    @pl.when(roped)
    def _():
        cos = cos_ref[...]
        sin = sin_ref[...]
        for c in range(acc.shape[1] // HEAD_DIM):
            sl = slice(c * HEAD_DIM, (c + 1) * HEAD_DIM)
            o_ref[:, sl] = _rope(acc[:, sl], cos, sin).astype(o_ref.dtype)

    @pl.when(jnp.logical_not(roped))
    def _():
        o_ref[...] = acc.astype(o_ref.dtype)


def _rope(xh, cos, sin):
    return xh * cos + pltpu.roll(xh, HEAD_DIM // 2, 1) * sin


def _segment_block(segments, tn):
    assert all(s % tn == 0 and w % tn == 0 for s, w in segments)

    def block(n):
        idx, first = None, 0
        for start, width in segments:
            here = start // tn + n - first
            idx = here if idx is None else jnp.where(n >= first, here, idx)
            first += width // tn
        return idx

    return block


def _proj(u, w, segments, cos, sin, rope_cols, out_dtype):
    T, D = u.shape
    N = sum(width for _, width in segments)
    tm = _row_tile(T, 2048)
    tn = 2 * COL_CHUNK
    n_tab = cos.shape[0] // tm
    assert cos.shape[0] % tm == 0 and rope_cols[0] % tn == 0 and rope_cols[1] % tn == 0
    kern = functools.partial(_proj_kernel, rope_lo=rope_cols[0] // tn, rope_hi=rope_cols[1] // tn)
    wblock = _segment_block(segments, tn)
    return pl.pallas_call(
        kern,
        grid=(T // tm, N // tn),
        in_specs=[pl.BlockSpec((tm, D), lambda m, n: (m, 0)),
                  pl.BlockSpec((D, tn), lambda m, n: (0, wblock(n))),
                  pl.BlockSpec((tm, HEAD_DIM), lambda m, n: (m % n_tab, 0)),
                  pl.BlockSpec((tm, HEAD_DIM), lambda m, n: (m % n_tab, 0))],
        out_specs=pl.BlockSpec((tm, tn), lambda m, n: (m, n)),
        out_shape=jax.ShapeDtypeStruct((T, N), out_dtype),
        compiler_params=_params("arbitrary", "arbitrary"),
        name="in_proj",
    )(u, w, cos, sin)


def _norm_proj_kernel(x_ref, g_ref, w_ref, u_ref, o_ref):
    @pl.when(pl.program_id(1) == 0)
    def _():
        x = x_ref[...]
        ms = jnp.mean(x * x, axis=-1, keepdims=True)
        u_ref[...] = (x * lax.rsqrt(ms + RMS_EPS) * g_ref[...]).astype(u_ref.dtype)

    o_ref[...] = jnp.dot(u_ref[...], w_ref[...], preferred_element_type=F32).astype(o_ref.dtype)


def _norm_proj(x, g, w, col, width, out_dtype):
    T, D = x.shape
    tm = _row_tile(T, 1024)
    tn = 2 * COL_CHUNK
    assert col % tn == 0 and width % tn == 0
    return pl.pallas_call(
        _norm_proj_kernel,
        grid=(T // tm, width // tn),
        in_specs=[pl.BlockSpec((tm, D), lambda m, n: (m, 0)),
                  pl.BlockSpec((1, D), lambda m, n: (0, 0)),
                  pl.BlockSpec((D, tn), lambda m, n: (0, col // tn + n))],
        out_specs=[pl.BlockSpec((tm, D), lambda m, n: (m, 0)),
                   pl.BlockSpec((tm, tn), lambda m, n: (m, n))],
        out_shape=[jax.ShapeDtypeStruct((T, D), BF16), jax.ShapeDtypeStruct((T, width), out_dtype)],
        compiler_params=_params("arbitrary", "arbitrary"),
        name="norm_proj",
    )(x, g.reshape(1, D), w)


def _kv_proj_kernel(u_ref, wk_ref, wv_ref, cos_ref, sin_ref, k_ref, v_ref):
    u = u_ref[...]
    tm = u.shape[0]
    cos = cos_ref[...]
    sin = sin_ref[...]
    k = jnp.dot(u, wk_ref[...], preferred_element_type=F32)
    for g in range(N_KV_HEADS):
        k_ref[pl.ds(g, tm, stride=N_KV_HEADS), :] = _rope(k[:, g * HEAD_DIM:(g + 1) * HEAD_DIM], cos, sin)
    v = jnp.dot(u, wv_ref[...], preferred_element_type=F32)
    for g in range(N_KV_HEADS):
        v_ref[pl.ds(g, tm, stride=N_KV_HEADS), :] = v[:, g * HEAD_DIM:(g + 1) * HEAD_DIM]


def _kv_proj(u, w, k_col, v_col, cos, sin):
    T, D = u.shape
    kvw = N_KV_HEADS * HEAD_DIM
    assert k_col % kvw == 0 and v_col % kvw == 0
    tm = _row_tile(T, 2048)
    n_tab = cos.shape[0] // tm
    assert cos.shape[0] % tm == 0
    out = jax.ShapeDtypeStruct((T * N_KV_HEADS, HEAD_DIM), F32)
    out_spec = pl.BlockSpec((tm * N_KV_HEADS, HEAD_DIM), lambda m: (m, 0))
    return pl.pallas_call(
        _kv_proj_kernel,
        grid=(T // tm,),
        in_specs=[pl.BlockSpec((tm, D), lambda m: (m, 0)),
                  pl.BlockSpec((D, kvw), lambda m: (0, k_col // kvw)),
                  pl.BlockSpec((D, kvw), lambda m: (0, v_col // kvw)),
                  pl.BlockSpec((tm, HEAD_DIM), lambda m: (m % n_tab, 0)),
                  pl.BlockSpec((tm, HEAD_DIM), lambda m: (m % n_tab, 0))],
        out_specs=[out_spec, out_spec],
        out_shape=[out, out],
        compiler_params=_params("arbitrary"),
        name="kv_proj",
    )(u, w, w, cos, sin)


def _topk_select(gate, blk, eligible, axis):
    gm = jnp.where(eligible, gate, NEG_INF)
    rank = jnp.zeros(gate.shape, jnp.int32)
    for r in range(gate.shape[axis]):
        gr = gm[r:r + 1, :] if axis == 0 else gm[:, r:r + 1]
        beats = jnp.logical_or(gr > gm, jnp.logical_and(gr == gm, blk > r))
        rank = rank + beats.astype(jnp.int32)
    return jnp.logical_and(rank < MOBA_TOPK, eligible)


def _attn_prompt_kernel(q_ref, k_ref, v_ref, o_ref,
                        kb_ref, vt_ref, kmean_ref, kmh_ref, sel_ref, qt_ref, s_ref, m_ref, l_ref, acc_ref,
                        *, tq, nblk, scale):
    g = pl.program_id(1)
    i = pl.program_id(2)

    @pl.when(i == 0)
    def _():
        for j in range(nblk):
            rows = pl.ds(j * MOBA_BLOCK * N_KV_HEADS + g, MOBA_BLOCK, stride=N_KV_HEADS)
            kj = k_ref[rows, :]
            kb_ref[j] = kj.astype(BF16)
            kmean_ref[j:j + 1, :] = jnp.mean(kj, axis=0, keepdims=True)
            vt_ref[j] = v_ref[rows, :].T.astype(BF16)
        km = kmean_ref[...]
        km_hi = km.astype(BF16)
        kmh_ref[...] = jnp.concatenate([km_hi, (km - km_hi.astype(F32)).astype(BF16)], axis=0)

    q = q_ref[...].astype(F32)
    qs = jnp.concatenate([q[:, h * HEAD_DIM:(h + 1) * HEAD_DIM] for h in range(KV_GROUP)], axis=0)
    qt = (qs * (scale * LOG2_E)).T
    qt_hi = qt.astype(BF16)
    qt_lo = (qt - qt_hi.astype(F32)).astype(BF16)
    qt_ref[...] = qt_hi

    def scores(j):
        return jnp.dot(kb_ref[j], qt_ref[...], preferred_element_type=F32)

    ext = jnp.dot(jnp.concatenate([kb_ref[i], kmh_ref[...]], axis=0), qt_hi,
                  preferred_element_type=F32)
    gate = (ext[MOBA_BLOCK:MOBA_BLOCK + nblk] + ext[MOBA_BLOCK + nblk:]) + jnp.dot(
        kmh_ref[:nblk, :], qt_lo, preferred_element_type=F32)
    s_ref[0] = scores(0)
    blk = lax.broadcasted_iota(jnp.int32, gate.shape, 0)
    sel = _topk_select(gate, blk, blk < i, axis=0)
    sel_ref[...] = sel.astype(F32)

    s = ext[:MOBA_BLOCK]
    key_t = lax.broadcasted_iota(jnp.int32, s.shape, 0)
    qry_t = lax.broadcasted_iota(jnp.int32, s.shape, 1) % tq
    s = jnp.where(key_t <= qry_t, s, NEG_INF)
    m0 = jnp.max(s, axis=0, keepdims=True)
    p = jnp.exp2(s - m0)
    m_ref[...] = m0
    l_ref[...] = jnp.sum(p, axis=0, keepdims=True)
    acc_ref[...] = jnp.dot(vt_ref[i], p.astype(BF16), preferred_element_type=F32)

    def absorb(slot, j):
        s = s_ref[slot]
        picked = sel_ref[pl.ds(j, 1), :] > 0.5
        m_old = m_ref[...]
        m_new = jnp.maximum(m_old, jnp.where(picked, jnp.max(s, axis=0, keepdims=True), NEG_INF))
        alpha = jnp.exp2(m_old - m_new)
        p = jnp.exp2(s - jnp.where(picked, m_new, -NEG_INF))
        m_ref[...] = m_new
        l_ref[...] = alpha * l_ref[...] + jnp.sum(p, axis=0, keepdims=True)
        acc_ref[...] = alpha * acc_ref[...] + jnp.dot(vt_ref[j], p.astype(BF16),
                                                      preferred_element_type=F32)

    def pair(ja):
        s_ref[1] = scores(ja + 1)
        absorb(0, ja)
        s_ref[0] = scores(jnp.minimum(ja + 2, nblk - 1))
        absorb(1, ja + 1)

    def body(t, carry):
        pair(4 * t)
        pair(4 * t + 2)
        return carry

    quads = i // 4
    lax.fori_loop(0, quads, body, 0)

    @pl.when(i % 4 >= 2)
    def _():
        pair(4 * quads)

    @pl.when(i % 2 == 1)
    def _():
        absorb(0, i - 1)

    out = acc_ref[...] * (1.0 / l_ref[...])
    for h in range(KV_GROUP):
        o_ref[:, h * HEAD_DIM:(h + 1) * HEAD_DIM] = out[:, h * tq:(h + 1) * tq].T.astype(o_ref.dtype)


def _attn_prompt(qarr, q_col, k4, v4, B, S):
    tq = MOBA_BLOCK
    assert S % MOBA_BLOCK == 0
    nblk = S // MOBA_BLOCK
    nq = S // tq
    gw = KV_GROUP * HEAD_DIM
    n = KV_GROUP * tq
    kern = functools.partial(_attn_prompt_kernel, tq=tq, nblk=nblk, scale=HEAD_DIM ** -0.5)
    return pl.pallas_call(
        kern,
        grid=(B, N_KV_HEADS, nq),
        in_specs=[pl.BlockSpec((tq, gw), lambda b, g, i: (b * nq + i, q_col // gw + g)),
                  pl.BlockSpec((S * N_KV_HEADS, HEAD_DIM), lambda b, g, i: (b, 0)),
                  pl.BlockSpec((S * N_KV_HEADS, HEAD_DIM), lambda b, g, i: (b, 0))],
        out_specs=pl.BlockSpec((tq, gw), lambda b, g, i: (b * nq + i, g)),
        out_shape=jax.ShapeDtypeStruct((B * S, N_HEADS * HEAD_DIM), BF16),
        scratch_shapes=[pltpu.VMEM((nblk, MOBA_BLOCK, HEAD_DIM), BF16),
                        pltpu.VMEM((nblk, HEAD_DIM, MOBA_BLOCK), BF16),
                        pltpu.VMEM((nblk, HEAD_DIM), F32),
                        pltpu.VMEM((2 * nblk, HEAD_DIM), BF16),
                        pltpu.VMEM((nblk, n), F32),
                        pltpu.VMEM((HEAD_DIM, n), BF16),
                        pltpu.VMEM((2, MOBA_BLOCK, n), F32),
                        pltpu.VMEM((1, n), F32),
                        pltpu.VMEM((1, n), F32),
                        pltpu.VMEM((HEAD_DIM, n), F32)],
        compiler_params=_params("arbitrary", "arbitrary", "arbitrary"),
        name="moba_prompt",
    )(qarr, k4, v4)


def _attn_decode_kernel(pt_ref, q_ref, kn_ref, vn_ref, *refs, pps, ns, nblk, ds, nseq, scale):
    del pt_ref
    k_refs = refs[:pps]
    v_refs = refs[pps:2 * pps]
    o_ref = refs[2 * pps]
    wf_ref, wb_ref, s_ref, p_ref, kmean_ref, l_ref, acc_ref = refs[2 * pps + 1:]
    r = pl.program_id(0)
    step = pl.program_id(1)
    bps = pps // 2
    kvw = N_KV_HEADS * HEAD_DIM
    nrow = N_HEADS * ds
    has_values = r >= 1
    has_keys = r < nseq

    def by_head(ref):
        tokens = ref.shape[0] // N_KV_HEADS
        return jnp.concatenate([ref[pl.ds(g, tokens, stride=N_KV_HEADS), :]
                                for g in range(N_KV_HEADS)], axis=1)

    def block_of(page_refs, jj):
        return jnp.concatenate([by_head(page_refs[2 * jj]), by_head(page_refs[2 * jj + 1])], axis=0)

    @pl.when(has_values)
    def _():
        for jj in range(bps):
            j = step * bps + jj
            vblk = block_of(v_refs, jj).astype(BF16)
            acc_ref[...] += jnp.dot(p_ref[j], vblk, preferred_element_type=F32)

    @pl.when(jnp.logical_and(has_values, step == ns - 1))
    def _():
        acc = acc_ref[...] * (1.0 / l_ref[...])
        outs = []
        for h in range(N_HEADS):
            g = h // KV_GROUP
            outs.append(acc[h * ds:(h + 1) * ds, g * HEAD_DIM:(g + 1) * HEAD_DIM])
        o_ref[...] = jnp.concatenate(outs, axis=1)

    @pl.when(jnp.logical_and(has_keys, step == 0))
    def _():
        q = q_ref[...]
        zero = jnp.zeros((ds, HEAD_DIM), F32)
        rows = []
        for h in range(N_HEADS):
            pieces = [zero] * N_KV_HEADS
            pieces[h // KV_GROUP] = q[:, h * HEAD_DIM:(h + 1) * HEAD_DIM]
            rows.append(jnp.concatenate(pieces, axis=1))
        wf = jnp.concatenate(rows, axis=0)
        wf_ref[...] = wf
        wb_ref[...] = (wf * scale).astype(BF16)

    @pl.when(has_keys)
    def _():
        for jj in range(bps):
            j = step * bps + jj
            kblk = block_of(k_refs, jj)
            kmean_ref[pl.ds(j, 1), :] = jnp.mean(kblk, axis=0, keepdims=True)
            s_ref[j] = lax.dot_general(wb_ref[...], kblk.astype(BF16), _NT,
                                       preferred_element_type=F32)

    @pl.when(jnp.logical_and(has_keys, step == ns - 1))
    def _():
        gate = lax.dot_general(kmean_ref[...], wf_ref[...], _NT, precision=lax.Precision.HIGHEST,
                               preferred_element_type=F32)
        blk = lax.broadcasted_iota(jnp.int32, gate.shape, 0)
        sel = _topk_select(gate, blk, blk < nblk, axis=0)
        bias = jnp.concatenate([jnp.where(sel, 0.0, NEG_INF),
                                jnp.zeros((nrow - nblk, nrow), F32)], axis=0).T

        pad = jnp.zeros((nrow - ds, kvw), F32)
        knp = jnp.concatenate([by_head(kn_ref), pad], axis=0).astype(BF16)
        sn = lax.dot_general(wb_ref[...], knp, _NT, preferred_element_type=F32)
        qry_t = lax.broadcasted_iota(jnp.int32, sn.shape, 0) % ds
        key_t = lax.broadcasted_iota(jnp.int32, sn.shape, 1)
        sn = jnp.where(key_t <= qry_t, sn, NEG_INF)

        mx = jnp.full((nrow, MOBA_BLOCK), NEG_INF, F32)
        for j in range(nblk):
            sb = s_ref[j] + bias[:, j:j + 1]
            s_ref[j] = sb
            mx = jnp.maximum(mx, sb)
        m = jnp.maximum(jnp.max(mx, axis=1, keepdims=True), jnp.max(sn, axis=1, keepdims=True))
        lsum = jnp.zeros((nrow, MOBA_BLOCK), F32)
        for j in range(nblk):
            p = jnp.exp(s_ref[j] - m)
            lsum = lsum + p
            p_ref[j] = p.astype(BF16)
        pn = jnp.exp(sn - m)
        l_ref[...] = jnp.sum(lsum, axis=1, keepdims=True) + jnp.sum(pn, axis=1, keepdims=True)
        vnp = jnp.concatenate([by_head(vn_ref), pad], axis=0).astype(BF16)
        acc_ref[...] = jnp.dot(pn.astype(BF16), vnp, preferred_element_type=F32)


def _attn_decode(qarr, q_col, k4, v4, cache_k, cache_v, layer, page_table, DB, DS):
    depth, n_phys, page, nkv, hd = cache_k.shape
    n_pages = page_table.shape[1]
    past_len = n_pages * page
    kvw = nkv * hd
    assert (nkv, hd) == (N_KV_HEADS, HEAD_DIM) and MOBA_BLOCK == 2 * page
    assert past_len % MOBA_BLOCK == 0 and DS == SUBLANES
    nblk = past_len // MOBA_BLOCK
    pps = 32
    assert n_pages % pps == 0
    ns = n_pages // pps
    nrow = N_HEADS * DS
    ck = cache_k.reshape(depth * n_phys, page * nkv, hd)
    cv = cache_v.reshape(depth * n_phys, page * nkv, hd)
    base = layer * n_phys

    kseq = lambda r: jnp.minimum(r, DB - 1)
    vseq = lambda r: jnp.maximum(r - 1, 0)

    kstep = lambda r, s: jnp.where(r < DB, s, ns - 1)
    vstep = lambda r, s: jnp.where(r >= 1, s, 0)

    def k_map(i):
        return lambda r, s, pt: (base + pt[kseq(r), kstep(r, s) * pps + i], 0, 0)

    def v_map(i):
        return lambda r, s, pt: (base + pt[vseq(r), vstep(r, s) * pps + i], 0, 0)

    qw = N_HEADS * HEAD_DIM
    in_specs = [pl.BlockSpec((DS, qw), lambda r, s, pt: (kseq(r), q_col // qw)),
                pl.BlockSpec((DS * nkv, hd), lambda r, s, pt: (kseq(r), 0)),
                pl.BlockSpec((DS * nkv, hd), lambda r, s, pt: (kseq(r), 0))]
    in_specs += [pl.BlockSpec((None, page * nkv, hd), k_map(i)) for i in range(pps)]
    in_specs += [pl.BlockSpec((None, page * nkv, hd), v_map(i)) for i in range(pps)]
    kern = functools.partial(_attn_decode_kernel, pps=pps, ns=ns, nblk=nblk, ds=DS, nseq=DB,
                             scale=HEAD_DIM ** -0.5)
    grid_spec = pltpu.PrefetchScalarGridSpec(
        num_scalar_prefetch=1,
        grid=(DB + 1, ns),
        in_specs=in_specs,
        out_specs=pl.BlockSpec((DS, qw), lambda r, s, pt: (vseq(r), 0)),
        scratch_shapes=[pltpu.VMEM((nrow, kvw), F32),
                        pltpu.VMEM((nrow, kvw), BF16),
                        pltpu.VMEM((nblk, nrow, MOBA_BLOCK), F32),
                        pltpu.VMEM((nblk, nrow, MOBA_BLOCK), BF16),
                        pltpu.VMEM((nblk, kvw), F32),
                        pltpu.VMEM((nrow, 1), F32),
                        pltpu.VMEM((nrow, kvw), F32)])
    return pl.pallas_call(
        kern,
        grid_spec=grid_spec,
        out_shape=jax.ShapeDtypeStruct((DB * DS, qw), F32),
        compiler_params=_params("arbitrary", "arbitrary"),
        name="moba_decode",
    )(page_table, qarr, k4, v4, *([ck] * pps), *([cv] * pps))


def _rglru_kernel(xr_ref, yg_ref, h0_ref, cb_ref, cw_ref, cbias_ref, wa_ref, ba_ref, wx_ref, bx_ref,
                  lam_ref, rnn_ref, hlast_ref, cnew_ref, halo_ref, hcarry_ref, *, L):
    c = pl.program_id(1)
    C = xr_ref.shape[1]

    @pl.when(c == 0)
    def _():
        hcarry_ref[...] = h0_ref[...]
        halo_ref[...] = jnp.concatenate(
            [jnp.zeros((SUBLANES - (CONV_W - 1), C), F32), cb_ref[...]], axis=0)

    G = L // SUBLANES
    xr = xr_ref[...]
    xg = jnp.concatenate([halo_ref[...], xr], axis=0).reshape(G + 1, SUBLANES, C)
    rin = lax.broadcasted_iota(jnp.int32, (1, SUBLANES, 1), 1)
    cw = cw_ref[...]
    xc = cbias_ref[...].reshape(1, 1, C)
    for j in range(CONV_W):
        k = CONV_W - 1 - j
        if k == 0:
            xk = xg[1:]
        else:
            rolled = pltpu.roll(xg, k, 1)
            xk = jnp.where(rin >= k, rolled[1:], rolled[:-1])
        xc = xc + xk * cw[j:j + 1, :].reshape(1, 1, C)
    xc = xc.reshape(L, C)

    xcb = xc.astype(BF16)
    bw = C // RNN_BLOCKS
    r_parts, i_parts = [], []
    for n in range(RNN_BLOCKS):
        xs = xcb[:, n * bw:(n + 1) * bw]
        r_parts.append(jnp.dot(xs, wa_ref[n], preferred_element_type=F32))
        i_parts.append(jnp.dot(xs, wx_ref[n], preferred_element_type=F32))
    r = jax.nn.sigmoid(jnp.concatenate(r_parts, axis=1) + ba_ref[...])
    ig = jax.nn.sigmoid(jnp.concatenate(i_parts, axis=1) + bx_ref[...])

    nl = -lam_ref[...]
    softplus = jnp.maximum(nl, 0.0) + jnp.log1p(jnp.exp(-jnp.abs(nl)))
    log_a = -LRU_C * r * softplus
    a = jnp.exp(log_a)
    gain2 = -jnp.tanh(log_a) * (a * a + 1.0)
    gain = jnp.where(gain2 > 0.0, gain2 * lax.rsqrt(gain2), 0.0)
    bv = gain * (ig * xc)

    a = a.reshape(G, SUBLANES, C)
    bv = bv.reshape(G, SUBLANES, C)
    d = 1
    while d < SUBLANES:
        same_group = rin >= d
        bv = a * jnp.where(same_group, pltpu.roll(bv, d, 1), 0.0) + bv
        a = a * jnp.where(same_group, pltpu.roll(a, d, 1), 1.0)
        d *= 2
    carry = hcarry_ref[...]
    groups = []
    for g in range(G):
        hg = bv[g] + a[g] * carry
        groups.append(hg)
        carry = hg[SUBLANES - 1:SUBLANES, :]
    h = groups[0] if G == 1 else jnp.concatenate(groups, axis=0)
    hcarry_ref[...] = carry
    hlast_ref[...] = carry

    yg = yg_ref[...].astype(F32)
    cdf = 0.5 * (1.0 + jnp.tanh(0.7978845608028654 * (yg + 0.044715 * (yg * yg * yg))))
    rnn_ref[...] = (h * (yg * cdf)).astype(rnn_ref.dtype)

    halo_ref[...] = xr[L - SUBLANES:L, :]
    cnew_ref[...] = xr[L - (CONV_W - 1):L, :]


def _rglru(xr_arr, xr_col, yg_arr, yg_col, B, S, h0, conv_buf, conv_w, conv_b, wa, ba, wx, bx, lam):
    C = h0.shape[-1]
    L = min(S, 256)
    assert S % L == 0 and L % SUBLANES == 0 and xr_col % C == 0 and yg_col % C == 0
    nc = S // L
    out_dtype = BF16 if L % 16 == 0 else F32
    row = lambda v: v.reshape(1, C)
    const2 = lambda b, c: (0, 0)
    const3 = lambda b, c: (0, 0, 0)
    rnn, hlast, cnew = pl.pallas_call(
        functools.partial(_rglru_kernel, L=L),
        grid=(B, nc),
        in_specs=[pl.BlockSpec((L, C), lambda b, c: (b * nc + c, xr_col // C)),
                  pl.BlockSpec((L, C), lambda b, c: (b * nc + c, yg_col // C)),
                  pl.BlockSpec((None, 1, C), lambda b, c: (b, 0, 0)),
                  pl.BlockSpec((None, CONV_W - 1, C), lambda b, c: (b, 0, 0)),
                  pl.BlockSpec((CONV_W, C), const2),
                  pl.BlockSpec((1, C), const2),
                  pl.BlockSpec((RNN_BLOCKS, C // RNN_BLOCKS, C // RNN_BLOCKS), const3),
                  pl.BlockSpec((1, C), const2),
                  pl.BlockSpec((RNN_BLOCKS, C // RNN_BLOCKS, C // RNN_BLOCKS), const3),
                  pl.BlockSpec((1, C), const2),
                  pl.BlockSpec((1, C), const2)],
        out_specs=[pl.BlockSpec((L, C), lambda b, c: (b * nc + c, 0)),
                   pl.BlockSpec((None, 1, C), lambda b, c: (b, 0, 0)),
                   pl.BlockSpec((None, CONV_W - 1, C), lambda b, c: (b, 0, 0))],
        out_shape=[jax.ShapeDtypeStruct((B * S, C), out_dtype),
                   jax.ShapeDtypeStruct((B, 1, C), F32),
                   jax.ShapeDtypeStruct((B, CONV_W - 1, C), F32)],
        scratch_shapes=[pltpu.VMEM((SUBLANES, C), F32),
                        pltpu.VMEM((1, C), F32)],
        compiler_params=_params("arbitrary", "arbitrary"),
        name="rglru",
    )(xr_arr, yg_arr, h0.reshape(B, 1, C), conv_buf, conv_w, row(conv_b), wa, row(ba), wx, row(bx), row(lam))
    return rnn, hlast.reshape(B, C), cnew


def _merge_kernel(rnn_ref, attn_ref, u_ref, wa_ref, wb_ref, wga_ref, wgb_ref, bga_ref, bgb_ref, o_ref):
    u = u_ref[...]
    a = jnp.dot(rnn_ref[...].astype(BF16), wa_ref[...], preferred_element_type=F32)
    b = jnp.dot(attn_ref[...].astype(BF16), wb_ref[...], preferred_element_type=F32)
    ga = jax.nn.sigmoid(jnp.dot(u, wga_ref[...], preferred_element_type=F32) + bga_ref[...])
    gb = jax.nn.sigmoid(jnp.dot(u, wgb_ref[...], preferred_element_type=F32) + bgb_ref[...])
    o_ref[...] = (ga * a + gb * b).astype(o_ref.dtype)


def _merge(rnn, attn, u, wa, wb, w_in, gate_col, b_gate):
    T, D = u.shape
    tm = _row_tile(T, 1024)
    tn = COL_CHUNK
    nb = D // tn
    assert gate_col % tn == 0
    act = lambda m, n: (m, 0)
    col = lambda m, n: (0, n)
    col_hi = lambda m, n: (0, n + nb)
    gcol = lambda m, n: (0, gate_col // tn + n)
    gcol_hi = lambda m, n: (0, gate_col // tn + nb + n)
    bg = b_gate.reshape(1, 2 * D)
    return pl.pallas_call(
        _merge_kernel,
        grid=(T // tm, nb),
        in_specs=[pl.BlockSpec((tm, D), act), pl.BlockSpec((tm, D), act), pl.BlockSpec((tm, D), act),
                  pl.BlockSpec((D, tn), col), pl.BlockSpec((D, tn), col),
                  pl.BlockSpec((D, tn), gcol), pl.BlockSpec((D, tn), gcol_hi),
                  pl.BlockSpec((1, tn), col), pl.BlockSpec((1, tn), col_hi)],
        out_specs=pl.BlockSpec((tm, tn), lambda m, n: (m, n)),
        out_shape=jax.ShapeDtypeStruct((T, D), BF16),
        compiler_params=_params("arbitrary", "arbitrary"),
        name="merge",
    )(rnn, attn, u, wa, wb, w_in, w_in, bg, bg)


def _outproj_kernel(mg_ref, x_ref, w_ref, g_ref, x2_ref, u2_ref):
    mg = mg_ref[...]
    D = x_ref.shape[1]
    ssq = jnp.zeros((x_ref.shape[0], 1), F32)
    for c in range(D // COL_CHUNK):
        sl = slice(c * COL_CHUNK, (c + 1) * COL_CHUNK)
        y = x_ref[:, sl] + jnp.dot(mg, w_ref[:, sl], preferred_element_type=F32)
        x2_ref[:, sl] = y
        ssq = ssq + jnp.sum(y * y, axis=-1, keepdims=True)
    inv = lax.rsqrt(ssq / D + RMS_EPS)
    u2_ref[...] = (x2_ref[...] * inv * g_ref[...]).astype(u2_ref.dtype)


def _outproj(merged, x, w_out, g):
    T, D = x.shape
    tm = _row_tile(T, 512)
    row = lambda m: (m, 0)
    return pl.pallas_call(
        _outproj_kernel,
        grid=(T // tm,),
        in_specs=[pl.BlockSpec((tm, D), row), pl.BlockSpec((tm, D), row),
                  pl.BlockSpec((D, D), lambda m: (0, 0)), pl.BlockSpec((1, D), lambda m: (0, 0))],
        out_specs=[pl.BlockSpec((tm, D), row), pl.BlockSpec((tm, D), row)],
        out_shape=[jax.ShapeDtypeStruct((T, D), F32), jax.ShapeDtypeStruct((T, D), BF16)],
        compiler_params=_params("arbitrary"),
        name="out_proj",
    )(merged, x, w_out, g.reshape(1, D))


def _ffn_kernel(u2_ref, x2_ref, w1_ref, w2_ref, gf_ref, y_ref, *, nf, final_norm):
    f = pl.program_id(1)

    @pl.when(f == 0)
    def _():
        y_ref[...] = x2_ref[...]

    h = jnp.dot(u2_ref[...], w1_ref[...], preferred_element_type=F32)
    h = jnp.square(jnp.maximum(h, 0.0)).astype(BF16)
    for c in range(y_ref.shape[1] // COL_CHUNK):
        sl = slice(c * COL_CHUNK, (c + 1) * COL_CHUNK)
        y_ref[:, sl] += jnp.dot(h, w2_ref[:, sl], preferred_element_type=F32)

    if final_norm:
        @pl.when(f == nf - 1)
        def _():
            x3 = y_ref[...]
            ms = jnp.mean(x3 * x3, axis=-1, keepdims=True)
            y_ref[...] = x3 * lax.rsqrt(ms + RMS_EPS) * gf_ref[...]


def _ffn(u2, x2, w1, w2, g_final, final_norm):
    T, D = x2.shape
    F = w1.shape[1]
    tm = _row_tile(T, 1024)
    tf = 512 if tm == 1024 else 1024
    nf = F // tf
    row = lambda m, f: (m, 0)
    return pl.pallas_call(
        functools.partial(_ffn_kernel, nf=nf, final_norm=final_norm),
        grid=(T // tm, nf),
        in_specs=[pl.BlockSpec((tm, D), row), pl.BlockSpec((tm, D), row),
                  pl.BlockSpec((D, tf), lambda m, f: (0, f)), pl.BlockSpec((tf, D), lambda m, f: (f, 0)),
                  pl.BlockSpec((1, D), lambda m, f: (0, 0))],
        out_specs=pl.BlockSpec((tm, D), row),
        out_shape=jax.ShapeDtypeStruct((T, D), F32),
        compiler_params=_params("arbitrary", "arbitrary"),
        name="ffn",
    )(u2, x2, w1, w2, g_final.reshape(1, D))


def _rope_tables(pos):
    half = HEAD_DIM // 2
    inv = ROPE_THETA ** (-jnp.arange(half, dtype=F32) * (2.0 / HEAD_DIM))
    ang = pos.astype(F32)[:, None] * inv[None, :]
    cos, sin = jnp.cos(ang), jnp.sin(ang)
    return jnp.concatenate([cos, cos], axis=-1), jnp.concatenate([-sin, sin], axis=-1)


def _decoder_layer(x, B, S, cos, sin, past, h0, conv_buf, w, final_norm):
    C = h0.shape[-1]
    attn_w = N_HEADS * HEAD_DIM
    act_dtype = BF16 if S % 16 == 0 else F32

    kv_w = N_KV_HEADS * HEAD_DIM
    q_col, k_col, v_col = 0, attn_w, attn_w + kv_w
    xr_col = v_col + kv_w
    yg_col = xr_col + C
    gate_col = yg_col + C
    w_in = w["w_in"]

    u, xr = _norm_proj(x, w["norm_mix"], w_in, xr_col, C, F32)
    ygq = _proj(u, w_in, [(yg_col, C), (q_col, attn_w)], cos, sin,
                (C, C + attn_w), act_dtype)
    k4, v4 = _kv_proj(u, w_in, k_col, v_col, cos, sin)
    if past is None:
        attn = _attn_prompt(ygq, C, k4, v4, B, S)
    else:
        attn = _attn_decode(ygq, C, k4, v4, past[0], past[1], past[2], past[3], B, S)
    rnn, h_last, conv_new = _rglru(xr, 0, ygq, 0, B, S, h0, conv_buf, w["conv_w"], w["conv_b"],
                                   w["w_rg_a"], w["b_rg_a"], w["w_rg_x"], w["b_rg_x"], w["lru_lambda"])
    merged = _merge(rnn, attn, u, w["w_proj_a"], w["w_proj_b"], w_in, gate_col, w["b_gate"])
    x2, u2 = _outproj(merged, x, w["w_out"], w["norm_mlp"])
    y = _ffn(u2, x2, w["w_ff1"], w["w_ff2"], w["norm_final"], final_norm)
    k = k4.reshape(B, S, N_KV_HEADS, HEAD_DIM)
    v = v4.reshape(B, S, N_KV_HEADS, HEAD_DIM)
    return y, k, v, h_last, conv_new


def kernel(x_prompt, x_sample, cache_k, cache_v, state_h, state_conv, page_table, norm_mix, w_in, b_gate, conv_w, conv_b, w_rg_a, b_rg_a, w_rg_x, b_rg_x, lru_lambda, w_proj_a, w_proj_b, w_out, norm_mlp, w_ff1, w_ff2, norm_final):
    B, S, D = x_prompt.shape
    DB, DS, _ = x_sample.shape
    depth = w_in.shape[0]
    C = state_h.shape[-1]
    past_len = page_table.shape[1] * cache_k.shape[2]

    cos_p, sin_p = _rope_tables(jnp.arange(S, dtype=jnp.int32))
    cos_s, sin_s = _rope_tables(past_len + jnp.arange(DS, dtype=jnp.int32))
    cos_s, sin_s = jnp.tile(cos_s, (DB, 1)), jnp.tile(sin_s, (DB, 1))

    hp = x_prompt.reshape(B * S, D)
    hs = x_sample.reshape(DB * DS, D)
    outs = [[] for _ in range(8)]
    for l in range(depth):
        wl = w_in[l]
        w = dict(
            norm_mix=norm_mix[l], norm_mlp=norm_mlp[l], norm_final=norm_final,
            w_in=wl.astype(BF16), b_gate=b_gate[l],
            conv_w=conv_w[l], conv_b=conv_b[l],
            w_rg_a=w_rg_a[l].astype(BF16), b_rg_a=b_rg_a[l],
            w_rg_x=w_rg_x[l].astype(BF16), b_rg_x=b_rg_x[l], lru_lambda=lru_lambda[l],
            w_proj_a=w_proj_a[l].astype(BF16), w_proj_b=w_proj_b[l].astype(BF16),
            w_out=w_out[l].astype(BF16), w_ff1=w_ff1[l].astype(BF16), w_ff2=w_ff2[l].astype(BF16))
        final = l == depth - 1
        h0_p = jnp.zeros((B, C), F32)
        buf_p = jnp.zeros((B, CONV_W - 1, C), F32)
        hp, k1, v1, r1, c1 = _decoder_layer(hp, B, S, cos_p, sin_p, None, h0_p, buf_p, w, final)
        past = (cache_k, cache_v, l, page_table)
        hs, k2, v2, r2, c2 = _decoder_layer(hs, DB, DS, cos_s, sin_s, past, state_h[l], state_conv[l], w, final)
        for lst, val in zip(outs, (k1, v1, r1, c1, k2, v2, r2, c2)):
            lst.append(val)
    return (hp.reshape(B, S, D), hs.reshape(DB, DS, D), *[jnp.stack(o) for o in outs])
```

```python
import functools

import jax
import jax.numpy as jnp
from jax import lax
from jax.experimental import pallas as pl
from jax.experimental.pallas import tpu as pltpu

F32 = jnp.float32
BF16 = jnp.bfloat16

N_HEADS = 16
N_KV_HEADS = 4
HEAD_DIM = 128
KV_GROUP = N_HEADS // N_KV_HEADS
ROPE_THETA = 10000.0
MOBA_BLOCK = 256
MOBA_TOPK = 3
RNN_BLOCKS = 16
CONV_W = 4
LRU_C = 8.0
RMS_EPS = 1e-6
NEG_INF = -1e30
LOG2_E = 1.4426950408889634

LANES = 128
SUBLANES = 8
VMEM_LIMIT_BYTES = 56 * 1024 * 1024
COL_CHUNK = 512

_NT = (((1,), (1,)), ((), ()))


def _params(*sem):
    return pltpu.CompilerParams(dimension_semantics=sem, vmem_limit_bytes=VMEM_LIMIT_BYTES)


def _row_tile(rows, pref):
    t = min(rows, pref)
    assert rows % t == 0, (rows, pref)
    return t


def _proj_kernel(u_ref, w_ref, cos_ref, sin_ref, o_ref, *, rope_lo, rope_hi):
    acc = jnp.dot(u_ref[...], w_ref[...], preferred_element_type=F32)
    n = pl.program_id(1)
    roped = jnp.logical_and(n >= rope_lo, n < rope_hi)

    @pl.when(roped)
    def _():
        cos = cos_ref[...]
        sin = sin_ref[...]
        for c in range(acc.shape[1] // HEAD_DIM):
            sl = slice(c * HEAD_DIM, (c + 1) * HEAD_DIM)
            o_ref[:, sl] = _rope(acc[:, sl], cos, sin).astype(o_ref.dtype)

    @pl.when(jnp.logical_not(roped))
    def _():
        o_ref[...] = acc.astype(o_ref.dtype)


def _rope(xh, cos, sin):
    return xh * cos + pltpu.roll(xh, HEAD_DIM // 2, 1) * sin


def _segment_block(segments, tn):
    assert all(s % tn == 0 and w % tn == 0 for s, w in segments)

    def block(n):
        idx, first = None, 0
        for start, width in segments:
            here = start // tn + n - first
            idx = here if idx is None else jnp.where(n >= first, here, idx)
            first += width // tn
        return idx

    return block


def _proj(u, w, segments, cos, sin, rope_cols, out_dtype):
    T, D = u.shape
    N = sum(width for _, width in segments)
    tm = _row_tile(T, 1024)
    tn = 2 * COL_CHUNK
    n_tab = cos.shape[0] // tm
    assert cos.shape[0] % tm == 0 and rope_cols[0] % tn == 0 and rope_cols[1] % tn == 0
    kern = functools.partial(_proj_kernel, rope_lo=rope_cols[0] // tn, rope_hi=rope_cols[1] // tn)
    wblock = _segment_block(segments, tn)
    return pl.pallas_call(
        kern,
        grid=(T // tm, N // tn),
        in_specs=[pl.BlockSpec((tm, D), lambda m, n: (m, 0)),
                  pl.BlockSpec((D, tn), lambda m, n: (0, wblock(n))),
                  pl.BlockSpec((tm, HEAD_DIM), lambda m, n: (m % n_tab, 0)),
                  pl.BlockSpec((tm, HEAD_DIM), lambda m, n: (m % n_tab, 0))],
        out_specs=pl.BlockSpec((tm, tn), lambda m, n: (m, n)),
        out_shape=jax.ShapeDtypeStruct((T, N), out_dtype),
        compiler_params=_params("arbitrary", "arbitrary"),
        name="in_proj",
    )(u, w, cos, sin)


def _norm_proj_kernel(x_ref, g_ref, w_ref, u_ref, o_ref):
    @pl.when(pl.program_id(1) == 0)
    def _():
        x = x_ref[...]
        ms = jnp.mean(x * x, axis=-1, keepdims=True)
        u_ref[...] = (x * lax.rsqrt(ms + RMS_EPS) * g_ref[...]).astype(u_ref.dtype)

    o_ref[...] = jnp.dot(u_ref[...], w_ref[...], preferred_element_type=F32).astype(o_ref.dtype)


def _norm_proj(x, g, w, col, width, out_dtype):
    T, D = x.shape
    tm = _row_tile(T, 1024)
    tn = 2 * COL_CHUNK
    assert col % tn == 0 and width % tn == 0
    return pl.pallas_call(
        _norm_proj_kernel,
        grid=(T // tm, width // tn),
        in_specs=[pl.BlockSpec((tm, D), lambda m, n: (m, 0)),
                  pl.BlockSpec((1, D), lambda m, n: (0, 0)),
                  pl.BlockSpec((D, tn), lambda m, n: (0, col // tn + n))],
        out_specs=[pl.BlockSpec((tm, D), lambda m, n: (m, 0)),
                   pl.BlockSpec((tm, tn), lambda m, n: (m, n))],
        out_shape=[jax.ShapeDtypeStruct((T, D), BF16), jax.ShapeDtypeStruct((T, width), out_dtype)],
        compiler_params=_params("arbitrary", "arbitrary"),
        name="norm_proj",
    )(x, g.reshape(1, D), w)


def _kv_proj_kernel(u_ref, wk_ref, wv_ref, cos_ref, sin_ref, k_ref, v_ref):
    u = u_ref[...]
    tm = u.shape[0]
    cos = cos_ref[...]
    sin = sin_ref[...]
    k = jnp.dot(u, wk_ref[...], preferred_element_type=F32)
    for g in range(N_KV_HEADS):
        k_ref[pl.ds(g, tm, stride=N_KV_HEADS), :] = _rope(k[:, g * HEAD_DIM:(g + 1) * HEAD_DIM], cos, sin)
    v = jnp.dot(u, wv_ref[...], preferred_element_type=F32)
    for g in range(N_KV_HEADS):
        v_ref[pl.ds(g, tm, stride=N_KV_HEADS), :] = v[:, g * HEAD_DIM:(g + 1) * HEAD_DIM]


def _kv_proj(u, w, k_col, v_col, cos, sin):
    T, D = u.shape
    kvw = N_KV_HEADS * HEAD_DIM
    assert k_col % kvw == 0 and v_col % kvw == 0
    tm = _row_tile(T, 1024)
    n_tab = cos.shape[0] // tm
    assert cos.shape[0] % tm == 0
    out = jax.ShapeDtypeStruct((T * N_KV_HEADS, HEAD_DIM), F32)
    out_spec = pl.BlockSpec((tm * N_KV_HEADS, HEAD_DIM), lambda m: (m, 0))
    return pl.pallas_call(
        _kv_proj_kernel,
        grid=(T // tm,),
        in_specs=[pl.BlockSpec((tm, D), lambda m: (m, 0)),
                  pl.BlockSpec((D, kvw), lambda m: (0, k_col // kvw)),
                  pl.BlockSpec((D, kvw), lambda m: (0, v_col // kvw)),
                  pl.BlockSpec((tm, HEAD_DIM), lambda m: (m % n_tab, 0)),
                  pl.BlockSpec((tm, HEAD_DIM), lambda m: (m % n_tab, 0))],
        out_specs=[out_spec, out_spec],
        out_shape=[out, out],
        compiler_params=_params("arbitrary"),
        name="kv_proj",
    )(u, w, w, cos, sin)


def _topk_select(gate, blk, eligible, axis):
    gm = jnp.where(eligible, gate, NEG_INF)
    rank = jnp.zeros(gate.shape, jnp.int32)
    for r in range(gate.shape[axis]):
        gr = gm[r:r + 1, :] if axis == 0 else gm[:, r:r + 1]
        beats = jnp.logical_or(gr > gm, jnp.logical_and(gr == gm, blk > r))
        rank = rank + beats.astype(jnp.int32)
    return jnp.logical_and(rank < MOBA_TOPK, eligible)


def _attn_prompt_kernel(q_ref, k_ref, v_ref, o_ref,
                        kb_ref, vt_ref, kmean_ref, kmh_ref, sel_ref, qt_ref, s_ref, m_ref, l_ref, acc_ref,
                        *, tq, nblk, scale):
    g = pl.program_id(1)
    i = pl.program_id(2)

    @pl.when(i == 0)
    def _():
        for j in range(nblk):
            rows = pl.ds(j * MOBA_BLOCK * N_KV_HEADS + g, MOBA_BLOCK, stride=N_KV_HEADS)
            kj = k_ref[rows, :]
            kb_ref[j] = kj.astype(BF16)
            kmean_ref[j:j + 1, :] = jnp.mean(kj, axis=0, keepdims=True)
            vt_ref[j] = v_ref[rows, :].T.astype(BF16)
        km = kmean_ref[...]
        km_hi = km.astype(BF16)
        kmh_ref[...] = jnp.concatenate([km_hi, (km - km_hi.astype(F32)).astype(BF16)], axis=0)

    q = q_ref[...].astype(F32)
    qs = jnp.concatenate([q[:, h * HEAD_DIM:(h + 1) * HEAD_DIM] for h in range(KV_GROUP)], axis=0)
    qt = (qs * (scale * LOG2_E)).T
    qt_hi = qt.astype(BF16)
    qt_lo = (qt - qt_hi.astype(F32)).astype(BF16)
    qt_ref[...] = qt_hi

    def scores(j):
        return jnp.dot(kb_ref[j], qt_ref[...], preferred_element_type=F32)

    ext = jnp.dot(jnp.concatenate([kb_ref[i], kmh_ref[...]], axis=0), qt_hi,
                  preferred_element_type=F32)
    gate = (ext[MOBA_BLOCK:MOBA_BLOCK + nblk] + ext[MOBA_BLOCK + nblk:]) + jnp.dot(
        kmh_ref[:nblk, :], qt_lo, preferred_element_type=F32)
    s_ref[0] = scores(0)
    blk = lax.broadcasted_iota(jnp.int32, gate.shape, 0)
    sel = _topk_select(gate, blk, blk < i, axis=0)
    sel_ref[...] = sel.astype(F32)

    s = ext[:MOBA_BLOCK]
    key_t = lax.broadcasted_iota(jnp.int32, s.shape, 0)
    qry_t = lax.broadcasted_iota(jnp.int32, s.shape, 1) % tq
    s = jnp.where(key_t <= qry_t, s, NEG_INF)
    m0 = jnp.max(s, axis=0, keepdims=True)
    p = jnp.exp2(s - m0)
    m_ref[...] = m0
    l_ref[...] = jnp.sum(p, axis=0, keepdims=True)
    acc_ref[...] = jnp.dot(vt_ref[i], p.astype(BF16), preferred_element_type=F32)

    def absorb(slot, j):
        s = s_ref[slot]
        picked = sel_ref[pl.ds(j, 1), :] > 0.5
        m_old = m_ref[...]
        m_new = jnp.maximum(m_old, jnp.where(picked, jnp.max(s, axis=0, keepdims=True), NEG_INF))
        alpha = jnp.exp2(m_old - m_new)
        p = jnp.exp2(s - jnp.where(picked, m_new, -NEG_INF))
        m_ref[...] = m_new
        l_ref[...] = alpha * l_ref[...] + jnp.sum(p, axis=0, keepdims=True)
        acc_ref[...] = alpha * acc_ref[...] + jnp.dot(vt_ref[j], p.astype(BF16),
                                                      preferred_element_type=F32)

    def pair(ja):
        s_ref[1] = scores(ja + 1)
        absorb(0, ja)
        s_ref[0] = scores(jnp.minimum(ja + 2, nblk - 1))
        absorb(1, ja + 1)

    def body(t, carry):
        pair(4 * t)
        pair(4 * t + 2)
        return carry

    quads = i // 4
    lax.fori_loop(0, quads, body, 0)

    @pl.when(i % 4 >= 2)
    def _():
        pair(4 * quads)

    @pl.when(i % 2 == 1)
    def _():
        absorb(0, i - 1)

    out = acc_ref[...] * (1.0 / l_ref[...])
    for h in range(KV_GROUP):
        o_ref[:, h * HEAD_DIM:(h + 1) * HEAD_DIM] = out[:, h * tq:(h + 1) * tq].T.astype(o_ref.dtype)


def _attn_prompt(qarr, q_col, k4, v4, B, S):
    tq = MOBA_BLOCK
    assert S % MOBA_BLOCK == 0
    nblk = S // MOBA_BLOCK
    nq = S // tq
    gw = KV_GROUP * HEAD_DIM
    n = KV_GROUP * tq
    kern = functools.partial(_attn_prompt_kernel, tq=tq, nblk=nblk, scale=HEAD_DIM ** -0.5)
    return pl.pallas_call(
        kern,
        grid=(B, N_KV_HEADS, nq),
        in_specs=[pl.BlockSpec((tq, gw), lambda b, g, i: (b * nq + i, q_col // gw + g)),
                  pl.BlockSpec((S * N_KV_HEADS, HEAD_DIM), lambda b, g, i: (b, 0)),
                  pl.BlockSpec((S * N_KV_HEADS, HEAD_DIM), lambda b, g, i: (b, 0))],
        out_specs=pl.BlockSpec((tq, gw), lambda b, g, i: (b * nq + i, g)),
        out_shape=jax.ShapeDtypeStruct((B * S, N_HEADS * HEAD_DIM), BF16),
        scratch_shapes=[pltpu.VMEM((nblk, MOBA_BLOCK, HEAD_DIM), BF16),
                        pltpu.VMEM((nblk, HEAD_DIM, MOBA_BLOCK), BF16),
                        pltpu.VMEM((nblk, HEAD_DIM), F32),
                        pltpu.VMEM((2 * nblk, HEAD_DIM), BF16),
                        pltpu.VMEM((nblk, n), F32),
                        pltpu.VMEM((HEAD_DIM, n), BF16),
                        pltpu.VMEM((2, MOBA_BLOCK, n), F32),
                        pltpu.VMEM((1, n), F32),
                        pltpu.VMEM((1, n), F32),
                        pltpu.VMEM((HEAD_DIM, n), F32)],
        compiler_params=_params("arbitrary", "arbitrary", "arbitrary"),
        name="moba_prompt",
    )(qarr, k4, v4)


def _attn_decode_kernel(pt_ref, q_ref, kn_ref, vn_ref, *refs, pps, ns, nblk, ds, nseq, scale):
    del pt_ref
    k_refs = refs[:pps]
    v_refs = refs[pps:2 * pps]
    o_ref = refs[2 * pps]
    wf_ref, wb_ref, s_ref, p_ref, kmean_ref, l_ref, acc_ref = refs[2 * pps + 1:]
    r = pl.program_id(0)
    step = pl.program_id(1)
    bps = pps // 2
    kvw = N_KV_HEADS * HEAD_DIM
    nrow = N_HEADS * ds
    has_values = r >= 1
    has_keys = r < nseq

    def by_head(ref):
        tokens = ref.shape[0] // N_KV_HEADS
        return jnp.concatenate([ref[pl.ds(g, tokens, stride=N_KV_HEADS), :]
                                for g in range(N_KV_HEADS)], axis=1)

    def block_of(page_refs, jj):
        return jnp.concatenate([by_head(page_refs[2 * jj]), by_head(page_refs[2 * jj + 1])], axis=0)

    @pl.when(has_values)
    def _():
        for jj in range(bps):
            j = step * bps + jj
            vblk = block_of(v_refs, jj).astype(BF16)
            acc_ref[...] += jnp.dot(p_ref[j], vblk, preferred_element_type=F32)

    @pl.when(jnp.logical_and(has_values, step == ns - 1))
    def _():
        acc = acc_ref[...] * (1.0 / l_ref[...])
        outs = []
        for h in range(N_HEADS):
            g = h // KV_GROUP
            outs.append(acc[h * ds:(h + 1) * ds, g * HEAD_DIM:(g + 1) * HEAD_DIM])
        o_ref[...] = jnp.concatenate(outs, axis=1)

    @pl.when(jnp.logical_and(has_keys, step == 0))
    def _():
        q = q_ref[...]
        zero = jnp.zeros((ds, HEAD_DIM), F32)
        rows = []
        for h in range(N_HEADS):
            pieces = [zero] * N_KV_HEADS
            pieces[h // KV_GROUP] = q[:, h * HEAD_DIM:(h + 1) * HEAD_DIM]
            rows.append(jnp.concatenate(pieces, axis=1))
        wf = jnp.concatenate(rows, axis=0)
        wf_ref[...] = wf
        wb_ref[...] = (wf * scale).astype(BF16)

    @pl.when(has_keys)
    def _():
        for jj in range(bps):
            j = step * bps + jj
            kblk = block_of(k_refs, jj)
            kmean_ref[pl.ds(j, 1), :] = jnp.mean(kblk, axis=0, keepdims=True)
            s_ref[j] = lax.dot_general(wb_ref[...], kblk.astype(BF16), _NT,
                                       preferred_element_type=F32)

    @pl.when(jnp.logical_and(has_keys, step == ns - 1))
    def _():
        gate = lax.dot_general(kmean_ref[...], wf_ref[...], _NT, precision=lax.Precision.HIGHEST,
                               preferred_element_type=F32)
        blk = lax.broadcasted_iota(jnp.int32, gate.shape, 0)
        sel = _topk_select(gate, blk, blk < nblk, axis=0)
        bias = jnp.concatenate([jnp.where(sel, 0.0, NEG_INF),
                                jnp.zeros((nrow - nblk, nrow), F32)], axis=0).T

        pad = jnp.zeros((nrow - ds, kvw), F32)
        knp = jnp.concatenate([by_head(kn_ref), pad], axis=0).astype(BF16)
        sn = lax.dot_general(wb_ref[...], knp, _NT, preferred_element_type=F32)
        qry_t = lax.broadcasted_iota(jnp.int32, sn.shape, 0) % ds
        key_t = lax.broadcasted_iota(jnp.int32, sn.shape, 1)
        sn = jnp.where(key_t <= qry_t, sn, NEG_INF)

        mx = jnp.full((nrow, MOBA_BLOCK), NEG_INF, F32)
        for j in range(nblk):
            sb = s_ref[j] + bias[:, j:j + 1]
            s_ref[j] = sb
            mx = jnp.maximum(mx, sb)
        m = jnp.maximum(jnp.max(mx, axis=1, keepdims=True), jnp.max(sn, axis=1, keepdims=True))
        lsum = jnp.zeros((nrow, MOBA_BLOCK), F32)
        for j in range(nblk):
            p = jnp.exp(s_ref[j] - m)
            lsum = lsum + p
            p_ref[j] = p.astype(BF16)
        pn = jnp.exp(sn - m)
        l_ref[...] = jnp.sum(lsum, axis=1, keepdims=True) + jnp.sum(pn, axis=1, keepdims=True)
        vnp = jnp.concatenate([by_head(vn_ref), pad], axis=0).astype(BF16)
        acc_ref[...] = jnp.dot(pn.astype(BF16), vnp, preferred_element_type=F32)


def _attn_decode(qarr, q_col, k4, v4, cache_k, cache_v, layer, page_table, DB, DS):
    depth, n_phys, page, nkv, hd = cache_k.shape
    n_pages = page_table.shape[1]
    past_len = n_pages * page
    kvw = nkv * hd
    assert (nkv, hd) == (N_KV_HEADS, HEAD_DIM) and MOBA_BLOCK == 2 * page
    assert past_len % MOBA_BLOCK == 0 and DS == SUBLANES
    nblk = past_len // MOBA_BLOCK
    pps = 32
    assert n_pages % pps == 0
    ns = n_pages // pps
    nrow = N_HEADS * DS
    ck = cache_k.reshape(depth * n_phys, page * nkv, hd)
    cv = cache_v.reshape(depth * n_phys, page * nkv, hd)
    base = layer * n_phys

    kseq = lambda r: jnp.minimum(r, DB - 1)
    vseq = lambda r: jnp.maximum(r - 1, 0)

    kstep = lambda r, s: jnp.where(r < DB, s, ns - 1)
    vstep = lambda r, s: jnp.where(r >= 1, s, 0)

    def k_map(i):
        return lambda r, s, pt: (base + pt[kseq(r), kstep(r, s) * pps + i], 0, 0)

    def v_map(i):
        return lambda r, s, pt: (base + pt[vseq(r), vstep(r, s) * pps + i], 0, 0)

    qw = N_HEADS * HEAD_DIM
    in_specs = [pl.BlockSpec((DS, qw), lambda r, s, pt: (kseq(r), q_col // qw)),
                pl.BlockSpec((DS * nkv, hd), lambda r, s, pt: (kseq(r), 0)),
                pl.BlockSpec((DS * nkv, hd), lambda r, s, pt: (kseq(r), 0))]
    in_specs += [pl.BlockSpec((None, page * nkv, hd), k_map(i)) for i in range(pps)]
    in_specs += [pl.BlockSpec((None, page * nkv, hd), v_map(i)) for i in range(pps)]
    kern = functools.partial(_attn_decode_kernel, pps=pps, ns=ns, nblk=nblk, ds=DS, nseq=DB,
                             scale=HEAD_DIM ** -0.5)
    grid_spec = pltpu.PrefetchScalarGridSpec(
        num_scalar_prefetch=1,
        grid=(DB + 1, ns),
        in_specs=in_specs,
        out_specs=pl.BlockSpec((DS, qw), lambda r, s, pt: (vseq(r), 0)),
        scratch_shapes=[pltpu.VMEM((nrow, kvw), F32),
                        pltpu.VMEM((nrow, kvw), BF16),
                        pltpu.VMEM((nblk, nrow, MOBA_BLOCK), F32),
                        pltpu.VMEM((nblk, nrow, MOBA_BLOCK), BF16),
                        pltpu.VMEM((nblk, kvw), F32),
                        pltpu.VMEM((nrow, 1), F32),
                        pltpu.VMEM((nrow, kvw), F32)])
    return pl.pallas_call(
        kern,
        grid_spec=grid_spec,
        out_shape=jax.ShapeDtypeStruct((DB * DS, qw), F32),
        compiler_params=_params("arbitrary", "arbitrary"),
        name="moba_decode",
    )(page_table, qarr, k4, v4, *([ck] * pps), *([cv] * pps))


def _rglru_kernel(xr_ref, yg_ref, h0_ref, cb_ref, cw_ref, cbias_ref, wa_ref, ba_ref, wx_ref, bx_ref,
                  lam_ref, rnn_ref, hlast_ref, cnew_ref, halo_ref, hcarry_ref, *, L):
    c = pl.program_id(1)
    C = xr_ref.shape[1]

    @pl.when(c == 0)
    def _():
        hcarry_ref[...] = h0_ref[...]
        halo_ref[...] = jnp.concatenate(
            [jnp.zeros((SUBLANES - (CONV_W - 1), C), F32), cb_ref[...]], axis=0)

    G = L // SUBLANES
    xr = xr_ref[...]
    xg = jnp.concatenate([halo_ref[...], xr], axis=0).reshape(G + 1, SUBLANES, C)
    rin = lax.broadcasted_iota(jnp.int32, (1, SUBLANES, 1), 1)
    cw = cw_ref[...]
    xc = cbias_ref[...].reshape(1, 1, C)
    for j in range(CONV_W):
        k = CONV_W - 1 - j
        if k == 0:
            xk = xg[1:]
        else:
            rolled = pltpu.roll(xg, k, 1)
            xk = jnp.where(rin >= k, rolled[1:], rolled[:-1])
        xc = xc + xk * cw[j:j + 1, :].reshape(1, 1, C)
    xc = xc.reshape(L, C)

    xcb = xc.astype(BF16)
    bw = C // RNN_BLOCKS
    r_parts, i_parts = [], []
    for n in range(RNN_BLOCKS):
        xs = xcb[:, n * bw:(n + 1) * bw]
        r_parts.append(jnp.dot(xs, wa_ref[n], preferred_element_type=F32))
        i_parts.append(jnp.dot(xs, wx_ref[n], preferred_element_type=F32))
    r = jax.nn.sigmoid(jnp.concatenate(r_parts, axis=1) + ba_ref[...])
    ig = jax.nn.sigmoid(jnp.concatenate(i_parts, axis=1) + bx_ref[...])

    nl = -lam_ref[...]
    softplus = jnp.maximum(nl, 0.0) + jnp.log1p(jnp.exp(-jnp.abs(nl)))
    log_a = -LRU_C * r * softplus
    a = jnp.exp(log_a)
    gain2 = -jnp.tanh(log_a) * (a * a + 1.0)
    gain = jnp.where(gain2 > 0.0, gain2 * lax.rsqrt(gain2), 0.0)
    bv = gain * (ig * xc)

    a = a.reshape(G, SUBLANES, C)
    bv = bv.reshape(G, SUBLANES, C)
    d = 1
    while d < SUBLANES:
        same_group = rin >= d
        bv = a * jnp.where(same_group, pltpu.roll(bv, d, 1), 0.0) + bv
        a = a * jnp.where(same_group, pltpu.roll(a, d, 1), 1.0)
        d *= 2
    carry = hcarry_ref[...]
    groups = []
    for g in range(G):
        hg = bv[g] + a[g] * carry
        groups.append(hg)
        carry = hg[SUBLANES - 1:SUBLANES, :]
    h = groups[0] if G == 1 else jnp.concatenate(groups, axis=0)
    hcarry_ref[...] = carry
    hlast_ref[...] = carry

    yg = yg_ref[...].astype(F32)
    cdf = 0.5 * (1.0 + jnp.tanh(0.7978845608028654 * (yg + 0.044715 * (yg * yg * yg))))
    rnn_ref[...] = (h * (yg * cdf)).astype(rnn_ref.dtype)

    halo_ref[...] = xr[L - SUBLANES:L, :]
    cnew_ref[...] = xr[L - (CONV_W - 1):L, :]


def _rglru(xr_arr, xr_col, yg_arr, yg_col, B, S, h0, conv_buf, conv_w, conv_b, wa, ba, wx, bx, lam):
    C = h0.shape[-1]
    L = min(S, 256)
    assert S % L == 0 and L % SUBLANES == 0 and xr_col % C == 0 and yg_col % C == 0
    nc = S // L
    out_dtype = BF16 if L % 16 == 0 else F32
    row = lambda v: v.reshape(1, C)
    const2 = lambda b, c: (0, 0)
    const3 = lambda b, c: (0, 0, 0)
    rnn, hlast, cnew = pl.pallas_call(
        functools.partial(_rglru_kernel, L=L),
        grid=(B, nc),
        in_specs=[pl.BlockSpec((L, C), lambda b, c: (b * nc + c, xr_col // C)),
                  pl.BlockSpec((L, C), lambda b, c: (b * nc + c, yg_col // C)),
                  pl.BlockSpec((None, 1, C), lambda b, c: (b, 0, 0)),
                  pl.BlockSpec((None, CONV_W - 1, C), lambda b, c: (b, 0, 0)),
                  pl.BlockSpec((CONV_W, C), const2),
                  pl.BlockSpec((1, C), const2),
                  pl.BlockSpec((RNN_BLOCKS, C // RNN_BLOCKS, C // RNN_BLOCKS), const3),
                  pl.BlockSpec((1, C), const2),
                  pl.BlockSpec((RNN_BLOCKS, C // RNN_BLOCKS, C // RNN_BLOCKS), const3),
                  pl.BlockSpec((1, C), const2),
                  pl.BlockSpec((1, C), const2)],
        out_specs=[pl.BlockSpec((L, C), lambda b, c: (b * nc + c, 0)),
                   pl.BlockSpec((None, 1, C), lambda b, c: (b, 0, 0)),
                   pl.BlockSpec((None, CONV_W - 1, C), lambda b, c: (b, 0, 0))],
        out_shape=[jax.ShapeDtypeStruct((B * S, C), out_dtype),
                   jax.ShapeDtypeStruct((B, 1, C), F32),
                   jax.ShapeDtypeStruct((B, CONV_W - 1, C), F32)],
        scratch_shapes=[pltpu.VMEM((SUBLANES, C), F32),
                        pltpu.VMEM((1, C), F32)],
        compiler_params=_params("arbitrary", "arbitrary"),
        name="rglru",
    )(xr_arr, yg_arr, h0.reshape(B, 1, C), conv_buf, conv_w, row(conv_b), wa, row(ba), wx, row(bx), row(lam))
    return rnn, hlast.reshape(B, C), cnew


def _merge_kernel(rnn_ref, attn_ref, u_ref, wa_ref, wb_ref, wga_ref, wgb_ref, bga_ref, bgb_ref, o_ref):
    u = u_ref[...]
    a = jnp.dot(rnn_ref[...].astype(BF16), wa_ref[...], preferred_element_type=F32)
    b = jnp.dot(attn_ref[...].astype(BF16), wb_ref[...], preferred_element_type=F32)
    ga = jax.nn.sigmoid(jnp.dot(u, wga_ref[...], preferred_element_type=F32) + bga_ref[...])
    gb = jax.nn.sigmoid(jnp.dot(u, wgb_ref[...], preferred_element_type=F32) + bgb_ref[...])
    o_ref[...] = (ga * a + gb * b).astype(o_ref.dtype)


def _merge(rnn, attn, u, wa, wb, w_in, gate_col, b_gate):
    T, D = u.shape
    tm = _row_tile(T, 1024)
    tn = COL_CHUNK
    nb = D // tn
    assert gate_col % tn == 0
    act = lambda m, n: (m, 0)
    col = lambda m, n: (0, n)
    col_hi = lambda m, n: (0, n + nb)
    gcol = lambda m, n: (0, gate_col // tn + n)
    gcol_hi = lambda m, n: (0, gate_col // tn + nb + n)
    bg = b_gate.reshape(1, 2 * D)
    return pl.pallas_call(
        _merge_kernel,
        grid=(T // tm, nb),
        in_specs=[pl.BlockSpec((tm, D), act), pl.BlockSpec((tm, D), act), pl.BlockSpec((tm, D), act),
                  pl.BlockSpec((D, tn), col), pl.BlockSpec((D, tn), col),
                  pl.BlockSpec((D, tn), gcol), pl.BlockSpec((D, tn), gcol_hi),
                  pl.BlockSpec((1, tn), col), pl.BlockSpec((1, tn), col_hi)],
        out_specs=pl.BlockSpec((tm, tn), lambda m, n: (m, n)),
        out_shape=jax.ShapeDtypeStruct((T, D), BF16),
        compiler_params=_params("arbitrary", "arbitrary"),
        name="merge",
    )(rnn, attn, u, wa, wb, w_in, w_in, bg, bg)


def _outproj_kernel(mg_ref, x_ref, w_ref, g_ref, x2_ref, u2_ref):
    mg = mg_ref[...]
    D = x_ref.shape[1]
    ssq = jnp.zeros((x_ref.shape[0], 1), F32)
    for c in range(D // COL_CHUNK):
        sl = slice(c * COL_CHUNK, (c + 1) * COL_CHUNK)
        y = x_ref[:, sl] + jnp.dot(mg, w_ref[:, sl], preferred_element_type=F32)
        x2_ref[:, sl] = y
        ssq = ssq + jnp.sum(y * y, axis=-1, keepdims=True)
    inv = lax.rsqrt(ssq / D + RMS_EPS)
    u2_ref[...] = (x2_ref[...] * inv * g_ref[...]).astype(u2_ref.dtype)


def _outproj(merged, x, w_out, g):
    T, D = x.shape
    tm = _row_tile(T, 512)
    row = lambda m: (m, 0)
    return pl.pallas_call(
        _outproj_kernel,
        grid=(T // tm,),
        in_specs=[pl.BlockSpec((tm, D), row), pl.BlockSpec((tm, D), row),
                  pl.BlockSpec((D, D), lambda m: (0, 0)), pl.BlockSpec((1, D), lambda m: (0, 0))],
        out_specs=[pl.BlockSpec((tm, D), row), pl.BlockSpec((tm, D), row)],
        out_shape=[jax.ShapeDtypeStruct((T, D), F32), jax.ShapeDtypeStruct((T, D), BF16)],
        compiler_params=_params("arbitrary"),
        name="out_proj",
    )(merged, x, w_out, g.reshape(1, D))


def _ffn_kernel(u2_ref, x2_ref, w1_ref, w2_ref, gf_ref, y_ref, *, nf, final_norm):
    f = pl.program_id(1)

    @pl.when(f == 0)
    def _():
        y_ref[...] = x2_ref[...]

    h = jnp.dot(u2_ref[...], w1_ref[...], preferred_element_type=F32)
    h = jnp.square(jnp.maximum(h, 0.0)).astype(BF16)
    for c in range(y_ref.shape[1] // COL_CHUNK):
        sl = slice(c * COL_CHUNK, (c + 1) * COL_CHUNK)
        y_ref[:, sl] += jnp.dot(h, w2_ref[:, sl], preferred_element_type=F32)

    if final_norm:
        @pl.when(f == nf - 1)
        def _():
            x3 = y_ref[...]
            ms = jnp.mean(x3 * x3, axis=-1, keepdims=True)
            y_ref[...] = x3 * lax.rsqrt(ms + RMS_EPS) * gf_ref[...]


def _ffn(u2, x2, w1, w2, g_final, final_norm):
    T, D = x2.shape
    F = w1.shape[1]
    tm = _row_tile(T, 1024)
    tf = 512 if tm == 1024 else 1024
    nf = F // tf
    row = lambda m, f: (m, 0)
    return pl.pallas_call(
        functools.partial(_ffn_kernel, nf=nf, final_norm=final_norm),
        grid=(T // tm, nf),
        in_specs=[pl.BlockSpec((tm, D), row), pl.BlockSpec((tm, D), row),
                  pl.BlockSpec((D, tf), lambda m, f: (0, f)), pl.BlockSpec((tf, D), lambda m, f: (f, 0)),
                  pl.BlockSpec((1, D), lambda m, f: (0, 0))],
        out_specs=pl.BlockSpec((tm, D), row),
        out_shape=jax.ShapeDtypeStruct((T, D), F32),
        compiler_params=_params("arbitrary", "arbitrary"),
        name="ffn",
    )(u2, x2, w1, w2, g_final.reshape(1, D))


def _rope_tables(pos):
    half = HEAD_DIM // 2
    inv = ROPE_THETA ** (-jnp.arange(half, dtype=F32) * (2.0 / HEAD_DIM))
    ang = pos.astype(F32)[:, None] * inv[None, :]
    cos, sin = jnp.cos(ang), jnp.sin(ang)
    return jnp.concatenate([cos, cos], axis=-1), jnp.concatenate([-sin, sin], axis=-1)


def _decoder_layer(x, B, S, cos, sin, past, h0, conv_buf, w, final_norm):
    C = h0.shape[-1]
    attn_w = N_HEADS * HEAD_DIM
    act_dtype = BF16 if S % 16 == 0 else F32

    kv_w = N_KV_HEADS * HEAD_DIM
    q_col, k_col, v_col = 0, attn_w, attn_w + kv_w
    xr_col = v_col + kv_w
    yg_col = xr_col + C
    gate_col = yg_col + C
    w_in = w["w_in"]

    u, xr = _norm_proj(x, w["norm_mix"], w_in, xr_col, C, F32)
    ygq = _proj(u, w_in, [(yg_col, C), (q_col, attn_w)], cos, sin,
                (C, C + attn_w), act_dtype)
    k4, v4 = _kv_proj(u, w_in, k_col, v_col, cos, sin)
    if past is None:
        attn = _attn_prompt(ygq, C, k4, v4, B, S)
    else:
        attn = _attn_decode(ygq, C, k4, v4, past[0], past[1], past[2], past[3], B, S)
    rnn, h_last, conv_new = _rglru(xr, 0, ygq, 0, B, S, h0, conv_buf, w["conv_w"], w["conv_b"],
                                   w["w_rg_a"], w["b_rg_a"], w["w_rg_x"], w["b_rg_x"], w["lru_lambda"])
    merged = _merge(rnn, attn, u, w["w_proj_a"], w["w_proj_b"], w_in, gate_col, w["b_gate"])
    x2, u2 = _outproj(merged, x, w["w_out"], w["norm_mlp"])
    y = _ffn(u2, x2, w["w_ff1"], w["w_ff2"], w["norm_final"], final_norm)
    k = k4.reshape(B, S, N_KV_HEADS, HEAD_DIM)
    v = v4.reshape(B, S, N_KV_HEADS, HEAD_DIM)
    return y, k, v, h_last, conv_new


def kernel(x_prompt, x_sample, cache_k, cache_v, state_h, state_conv, page_table, norm_mix, w_in, b_gate, conv_w, conv_b, w_rg_a, b_rg_a, w_rg_x, b_rg_x, lru_lambda, w_proj_a, w_proj_b, w_out, norm_mlp, w_ff1, w_ff2, norm_final):
    B, S, D = x_prompt.shape
    DB, DS, _ = x_sample.shape
    depth = w_in.shape[0]
    C = state_h.shape[-1]
    past_len = page_table.shape[1] * cache_k.shape[2]

    cos_p, sin_p = _rope_tables(jnp.arange(S, dtype=jnp.int32))
    cos_s, sin_s = _rope_tables(past_len + jnp.arange(DS, dtype=jnp.int32))
    cos_s, sin_s = jnp.tile(cos_s, (DB, 1)), jnp.tile(sin_s, (DB, 1))

    hp = x_prompt.reshape(B * S, D)
    hs = x_sample.reshape(DB * DS, D)
    outs = [[] for _ in range(8)]
    for l in range(depth):
        wl = w_in[l]
        w = dict(
            norm_mix=norm_mix[l], norm_mlp=norm_mlp[l], norm_final=norm_final,
            w_in=wl.astype(BF16), b_gate=b_gate[l],
            conv_w=conv_w[l], conv_b=conv_b[l],
            w_rg_a=w_rg_a[l].astype(BF16), b_rg_a=b_rg_a[l],
            w_rg_x=w_rg_x[l].astype(BF16), b_rg_x=b_rg_x[l], lru_lambda=lru_lambda[l],
            w_proj_a=w_proj_a[l].astype(BF16), w_proj_b=w_proj_b[l].astype(BF16),
            w_out=w_out[l].astype(BF16), w_ff1=w_ff1[l].astype(BF16), w_ff2=w_ff2[l].astype(BF16))
        final = l == depth - 1
        h0_p = jnp.zeros((B, C), F32)
        buf_p = jnp.zeros((B, CONV_W - 1, C), F32)
        hp, k1, v1, r1, c1 = _decoder_layer(hp, B, S, cos_p, sin_p, None, h0_p, buf_p, w, final)
        past = (cache_k, cache_v, l, page_table)
        hs, k2, v2, r2, c2 = _decoder_layer(hs, DB, DS, cos_s, sin_s, past, state_h[l], state_conv[l], w, final)
        for lst, val in zip(outs, (k1, v1, r1, c1, k2, v2, r2, c2)):
            lst.append(val)
    return (hp.reshape(B, S, D), hs.reshape(DB, DS, D), *[jnp.stack(o) for o in outs])
```

```python
import functools

import jax
import jax.numpy as jnp
from jax import lax
from jax.experimental import pallas as pl
from jax.experimental.pallas import tpu as pltpu

F32 = jnp.float32
BF16 = jnp.bfloat16

N_HEADS = 16
N_KV_HEADS = 4
HEAD_DIM = 128
KV_GROUP = N_HEADS // N_KV_HEADS
ROPE_THETA = 10000.0
MOBA_BLOCK = 256
MOBA_TOPK = 3
RNN_BLOCKS = 16
CONV_W = 4
LRU_C = 8.0
RMS_EPS = 1e-6
NEG_INF = -1e30
LOG2_E = 1.4426950408889634

LANES = 128
SUBLANES = 8
VMEM_LIMIT_BYTES = 56 * 1024 * 1024
COL_CHUNK = 512

_NT = (((1,), (1,)), ((), ()))


def _params(*sem, flags=None):
    return pltpu.CompilerParams(dimension_semantics=sem, vmem_limit_bytes=VMEM_LIMIT_BYTES, flags=flags)


def _row_tile(rows, pref):
    t = min(rows, pref)
    assert rows % t == 0, (rows, pref)
    return t


def _proj_kernel(u_ref, w_ref, cos_ref, sin_ref, o_ref, *, rope_lo, rope_hi):
    acc = jnp.dot(u_ref[...], w_ref[...], preferred_element_type=F32)
    n = pl.program_id(1)
    roped = jnp.logical_and(n >= rope_lo, n < rope_hi)

    @pl.when(roped)
    def _():
        cos = cos_ref[...]
        sin = sin_ref[...]
        for c in range(acc.shape[1] // HEAD_DIM):
            sl = slice(c * HEAD_DIM, (c + 1) * HEAD_DIM)
            o_ref[:, sl] = _rope(acc[:, sl], cos, sin).astype(o_ref.dtype)

    @pl.when(jnp.logical_not(roped))
    def _():
        o_ref[...] = acc.astype(o_ref.dtype)


def _rope(xh, cos, sin):
    return xh * cos + pltpu.roll(xh, HEAD_DIM // 2, 1) * sin


def _segment_block(segments, tn):
    assert all(s % tn == 0 and w % tn == 0 for s, w in segments)

    def block(n):
        idx, first = None, 0
        for start, width in segments:
            here = start // tn + n - first
            idx = here if idx is None else jnp.where(n >= first, here, idx)
            first += width // tn
        return idx

    return block


def _proj(u, w, segments, cos, sin, rope_cols, out_dtype):
    T, D = u.shape
    N = sum(width for _, width in segments)
    tm = _row_tile(T, 1024)
    tn = 2 * COL_CHUNK
    n_tab = cos.shape[0] // tm
    assert cos.shape[0] % tm == 0 and rope_cols[0] % tn == 0 and rope_cols[1] % tn == 0
    kern = functools.partial(_proj_kernel, rope_lo=rope_cols[0] // tn, rope_hi=rope_cols[1] // tn)
    wblock = _segment_block(segments, tn)
    return pl.pallas_call(
        kern,
        grid=(T // tm, N // tn),
        in_specs=[pl.BlockSpec((tm, D), lambda m, n: (m, 0)),
                  pl.BlockSpec((D, tn), lambda m, n: (0, wblock(n))),
                  pl.BlockSpec((tm, HEAD_DIM), lambda m, n: (m % n_tab, 0)),
                  pl.BlockSpec((tm, HEAD_DIM), lambda m, n: (m % n_tab, 0))],
        out_specs=pl.BlockSpec((tm, tn), lambda m, n: (m, n)),
        out_shape=jax.ShapeDtypeStruct((T, N), out_dtype),
        compiler_params=_params("arbitrary", "arbitrary"),
        name="in_proj",
    )(u, w, cos, sin)


def _norm_proj_kernel(x_ref, g_ref, w_ref, u_ref, o_ref):
    @pl.when(pl.program_id(1) == 0)
    def _():
        x = x_ref[...]
        ms = jnp.mean(x * x, axis=-1, keepdims=True)
        u_ref[...] = (x * lax.rsqrt(ms + RMS_EPS) * g_ref[...]).astype(u_ref.dtype)

    o_ref[...] = jnp.dot(u_ref[...], w_ref[...], preferred_element_type=F32).astype(o_ref.dtype)


def _norm_proj(x, g, w, col, width, out_dtype):
    T, D = x.shape
    tm = _row_tile(T, 1024)
    tn = 2 * COL_CHUNK
    assert col % tn == 0 and width % tn == 0
    return pl.pallas_call(
        _norm_proj_kernel,
        grid=(T // tm, width // tn),
        in_specs=[pl.BlockSpec((tm, D), lambda m, n: (m, 0)),
                  pl.BlockSpec((1, D), lambda m, n: (0, 0)),
                  pl.BlockSpec((D, tn), lambda m, n: (0, col // tn + n))],
        out_specs=[pl.BlockSpec((tm, D), lambda m, n: (m, 0)),
                   pl.BlockSpec((tm, tn), lambda m, n: (m, n))],
        out_shape=[jax.ShapeDtypeStruct((T, D), BF16), jax.ShapeDtypeStruct((T, width), out_dtype)],
        compiler_params=_params("arbitrary", "arbitrary"),
        name="norm_proj",
    )(x, g.reshape(1, D), w)


def _kv_proj_kernel(u_ref, wk_ref, wv_ref, cos_ref, sin_ref, k_ref, v_ref):
    u = u_ref[...]
    tm = u.shape[0]
    cos = cos_ref[...]
    sin = sin_ref[...]
    k = jnp.dot(u, wk_ref[...], preferred_element_type=F32)
    for g in range(N_KV_HEADS):
        k_ref[pl.ds(g, tm, stride=N_KV_HEADS), :] = _rope(k[:, g * HEAD_DIM:(g + 1) * HEAD_DIM], cos, sin)
    v = jnp.dot(u, wv_ref[...], preferred_element_type=F32)
    for g in range(N_KV_HEADS):
        v_ref[pl.ds(g, tm, stride=N_KV_HEADS), :] = v[:, g * HEAD_DIM:(g + 1) * HEAD_DIM]


def _kv_proj(u, w, k_col, v_col, cos, sin):
    T, D = u.shape
    kvw = N_KV_HEADS * HEAD_DIM
    assert k_col % kvw == 0 and v_col % kvw == 0
    tm = _row_tile(T, 1024)
    n_tab = cos.shape[0] // tm
    assert cos.shape[0] % tm == 0
    out = jax.ShapeDtypeStruct((T * N_KV_HEADS, HEAD_DIM), F32)
    out_spec = pl.BlockSpec((tm * N_KV_HEADS, HEAD_DIM), lambda m: (m, 0))
    return pl.pallas_call(
        _kv_proj_kernel,
        grid=(T // tm,),
        in_specs=[pl.BlockSpec((tm, D), lambda m: (m, 0)),
                  pl.BlockSpec((D, kvw), lambda m: (0, k_col // kvw)),
                  pl.BlockSpec((D, kvw), lambda m: (0, v_col // kvw)),
                  pl.BlockSpec((tm, HEAD_DIM), lambda m: (m % n_tab, 0)),
                  pl.BlockSpec((tm, HEAD_DIM), lambda m: (m % n_tab, 0))],
        out_specs=[out_spec, out_spec],
        out_shape=[out, out],
        compiler_params=_params("arbitrary"),
        name="kv_proj",
    )(u, w, w, cos, sin)


def _topk_select(gate, blk, eligible, axis):
    gm = jnp.where(eligible, gate, NEG_INF)
    rank = jnp.zeros(gate.shape, jnp.int32)
    for r in range(gate.shape[axis]):
        gr = gm[r:r + 1, :] if axis == 0 else gm[:, r:r + 1]
        beats = jnp.logical_or(gr > gm, jnp.logical_and(gr == gm, blk > r))
        rank = rank + beats.astype(jnp.int32)
    return jnp.logical_and(rank < MOBA_TOPK, eligible)


def _attn_prompt_kernel(q_ref, k_ref, v_ref, o_ref,
                        kb_ref, vt_ref, kmean_ref, kmh_ref, sel_ref, qt_ref, s_ref, m_ref, l_ref, acc_ref,
                        *, tq, nblk, scale):
    g = pl.program_id(1)
    i = pl.program_id(2)

    @pl.when(i == 0)
    def _():
        for j in range(nblk):
            rows = pl.ds(j * MOBA_BLOCK * N_KV_HEADS + g, MOBA_BLOCK, stride=N_KV_HEADS)
            kj = k_ref[rows, :]
            kb_ref[j] = kj.astype(BF16)
            kmean_ref[j:j + 1, :] = jnp.mean(kj, axis=0, keepdims=True)
            vt_ref[j] = v_ref[rows, :].T.astype(BF16)
        km = kmean_ref[...]
        km_hi = km.astype(BF16)
        kmh_ref[...] = jnp.concatenate([km_hi, (km - km_hi.astype(F32)).astype(BF16)], axis=0)

    q = q_ref[...].astype(F32)
    qs = jnp.concatenate([q[:, h * HEAD_DIM:(h + 1) * HEAD_DIM] for h in range(KV_GROUP)], axis=0)
    qt = (qs * (scale * LOG2_E)).T
    qt_hi = qt.astype(BF16)
    qt_lo = (qt - qt_hi.astype(F32)).astype(BF16)
    qt_ref[...] = qt_hi

    def scores(j):
        return jnp.dot(kb_ref[j], qt_ref[...], preferred_element_type=F32)

    ext = jnp.dot(jnp.concatenate([kb_ref[i], kmh_ref[...]], axis=0), qt_hi,
                  preferred_element_type=F32)
    gate = (ext[MOBA_BLOCK:MOBA_BLOCK + nblk] + ext[MOBA_BLOCK + nblk:]) + jnp.dot(
        kmh_ref[:nblk, :], qt_lo, preferred_element_type=F32)
    s_ref[0] = scores(0)
    blk = lax.broadcasted_iota(jnp.int32, gate.shape, 0)
    sel = _topk_select(gate, blk, blk < i, axis=0)
    sel_ref[...] = sel.astype(F32)

    s = ext[:MOBA_BLOCK]
    key_t = lax.broadcasted_iota(jnp.int32, s.shape, 0)
    qry_t = lax.broadcasted_iota(jnp.int32, s.shape, 1) % tq
    s = jnp.where(key_t <= qry_t, s, NEG_INF)
    m0 = jnp.max(s, axis=0, keepdims=True)
    p = jnp.exp2(s - m0)
    m_ref[...] = m0
    l_ref[...] = jnp.sum(p, axis=0, keepdims=True)
    acc_ref[...] = jnp.dot(vt_ref[i], p.astype(BF16), preferred_element_type=F32)

    def absorb(slot, j):
        half = s_ref.shape[2] // 2
        for c in range(2):
            cols = slice(c * half, (c + 1) * half)
            s = s_ref[slot, :, cols]
            picked = sel_ref[pl.ds(j, 1), cols] > 0.5
            m_old = m_ref[:, cols]
            m_new = jnp.maximum(m_old, jnp.where(picked, jnp.max(s, axis=0, keepdims=True), NEG_INF))
            alpha = jnp.exp2(m_old - m_new)
            p = jnp.exp2(s - jnp.where(picked, m_new, -NEG_INF))
            m_ref[:, cols] = m_new
            l_ref[:, cols] = alpha * l_ref[:, cols] + jnp.sum(p, axis=0, keepdims=True)
            acc_ref[:, cols] = alpha * acc_ref[:, cols] + jnp.dot(vt_ref[j], p.astype(BF16),
                                                                  preferred_element_type=F32)

    def pair(ja):
        s_ref[1] = scores(ja + 1)
        absorb(0, ja)
        s_ref[0] = scores(jnp.minimum(ja + 2, nblk - 1))
        absorb(1, ja + 1)

    def body(t, carry):
        pair(4 * t)
        pair(4 * t + 2)
        return carry

    quads = i // 4
    lax.fori_loop(0, quads, body, 0)

    @pl.when(i % 4 >= 2)
    def _():
        pair(4 * quads)

    @pl.when(i % 2 == 1)
    def _():
        absorb(0, i - 1)

    out = acc_ref[...] * (1.0 / l_ref[...])
    for h in range(KV_GROUP):
        o_ref[:, h * HEAD_DIM:(h + 1) * HEAD_DIM] = out[:, h * tq:(h + 1) * tq].T.astype(o_ref.dtype)


def _attn_prompt(qarr, q_col, k4, v4, B, S):
    tq = MOBA_BLOCK
    assert S % MOBA_BLOCK == 0
    nblk = S // MOBA_BLOCK
    nq = S // tq
    gw = KV_GROUP * HEAD_DIM
    n = KV_GROUP * tq
    kern = functools.partial(_attn_prompt_kernel, tq=tq, nblk=nblk, scale=HEAD_DIM ** -0.5)
    return pl.pallas_call(
        kern,
        grid=(B, N_KV_HEADS, nq),
        in_specs=[pl.BlockSpec((tq, gw), lambda b, g, i: (b * nq + i, q_col // gw + g)),
                  pl.BlockSpec((S * N_KV_HEADS, HEAD_DIM), lambda b, g, i: (b, 0)),
                  pl.BlockSpec((S * N_KV_HEADS, HEAD_DIM), lambda b, g, i: (b, 0))],
        out_specs=pl.BlockSpec((tq, gw), lambda b, g, i: (b * nq + i, g)),
        out_shape=jax.ShapeDtypeStruct((B * S, N_HEADS * HEAD_DIM), BF16),
        scratch_shapes=[pltpu.VMEM((nblk, MOBA_BLOCK, HEAD_DIM), BF16),
                        pltpu.VMEM((nblk, HEAD_DIM, MOBA_BLOCK), BF16),
                        pltpu.VMEM((nblk, HEAD_DIM), F32),
                        pltpu.VMEM((2 * nblk, HEAD_DIM), BF16),
                        pltpu.VMEM((nblk, n), F32),
                        pltpu.VMEM((HEAD_DIM, n), BF16),
                        pltpu.VMEM((2, MOBA_BLOCK, n), F32),
                        pltpu.VMEM((1, n), F32),
                        pltpu.VMEM((1, n), F32),
                        pltpu.VMEM((HEAD_DIM, n), F32)],
        compiler_params=_params("arbitrary", "arbitrary", "arbitrary"),
        name="moba_prompt",
    )(qarr, k4, v4)


def _attn_decode_kernel(pt_ref, q_ref, kn_ref, vn_ref, *refs, pps, ns, nblk, ds, nseq, scale):
    del pt_ref
    k_refs = refs[:pps]
    v_refs = refs[pps:2 * pps]
    o_ref = refs[2 * pps]
    wf_ref, wb_ref, s_ref, p_ref, kmean_ref, l_ref, acc_ref = refs[2 * pps + 1:]
    r = pl.program_id(0)
    step = pl.program_id(1)
    bps = pps // 2
    kvw = N_KV_HEADS * HEAD_DIM
    nrow = N_HEADS * ds
    has_values = r >= 1
    has_keys = r < nseq

    def by_head(ref):
        tokens = ref.shape[0] // N_KV_HEADS
        return jnp.concatenate([ref[pl.ds(g, tokens, stride=N_KV_HEADS), :]
                                for g in range(N_KV_HEADS)], axis=1)

    def block_of(page_refs, jj):
        return jnp.concatenate([by_head(page_refs[2 * jj]), by_head(page_refs[2 * jj + 1])], axis=0)

    @pl.when(has_values)
    def _():
        for jj in range(bps):
            j = step * bps + jj
            vblk = block_of(v_refs, jj).astype(BF16)
            acc_ref[...] += jnp.dot(p_ref[j], vblk, preferred_element_type=F32)

    @pl.when(jnp.logical_and(has_values, step == ns - 1))
    def _():
        acc = acc_ref[...] * (1.0 / l_ref[...])
        outs = []
        for h in range(N_HEADS):
            g = h // KV_GROUP
            outs.append(acc[h * ds:(h + 1) * ds, g * HEAD_DIM:(g + 1) * HEAD_DIM])
        o_ref[...] = jnp.concatenate(outs, axis=1)

    @pl.when(jnp.logical_and(has_keys, step == 0))
    def _():
        q = q_ref[...]
        zero = jnp.zeros((ds, HEAD_DIM), F32)
        rows = []
        for h in range(N_HEADS):
            pieces = [zero] * N_KV_HEADS
            pieces[h // KV_GROUP] = q[:, h * HEAD_DIM:(h + 1) * HEAD_DIM]
            rows.append(jnp.concatenate(pieces, axis=1))
        wf = jnp.concatenate(rows, axis=0)
        wf_ref[...] = wf
        wb_ref[...] = (wf * scale).astype(BF16)

    @pl.when(has_keys)
    def _():
        for jj in range(bps):
            j = step * bps + jj
            kblk = block_of(k_refs, jj)
            kmean_ref[pl.ds(j, 1), :] = jnp.mean(kblk, axis=0, keepdims=True)
            s_ref[j] = lax.dot_general(wb_ref[...], kblk.astype(BF16), _NT,
                                       preferred_element_type=F32)

    @pl.when(jnp.logical_and(has_keys, step == ns - 1))
    def _():
        gate = lax.dot_general(kmean_ref[...], wf_ref[...], _NT, precision=lax.Precision.HIGHEST,
                               preferred_element_type=F32)
        blk = lax.broadcasted_iota(jnp.int32, gate.shape, 0)
        sel = _topk_select(gate, blk, blk < nblk, axis=0)
        bias = jnp.concatenate([jnp.where(sel, 0.0, NEG_INF),
                                jnp.zeros((nrow - nblk, nrow), F32)], axis=0).T

        pad = jnp.zeros((nrow - ds, kvw), F32)
        knp = jnp.concatenate([by_head(kn_ref), pad], axis=0).astype(BF16)
        sn = lax.dot_general(wb_ref[...], knp, _NT, preferred_element_type=F32)
        qry_t = lax.broadcasted_iota(jnp.int32, sn.shape, 0) % ds
        key_t = lax.broadcasted_iota(jnp.int32, sn.shape, 1)
        sn = jnp.where(key_t <= qry_t, sn, NEG_INF)

        mx = jnp.full((nrow, MOBA_BLOCK), NEG_INF, F32)
        for j in range(nblk):
            sb = s_ref[j] + bias[:, j:j + 1]
            s_ref[j] = sb
            mx = jnp.maximum(mx, sb)
        m = jnp.maximum(jnp.max(mx, axis=1, keepdims=True), jnp.max(sn, axis=1, keepdims=True))
        lsum = jnp.zeros((nrow, MOBA_BLOCK), F32)
        for j in range(nblk):
            p = jnp.exp(s_ref[j] - m)
            lsum = lsum + p
            p_ref[j] = p.astype(BF16)
        pn = jnp.exp(sn - m)
        l_ref[...] = jnp.sum(lsum, axis=1, keepdims=True) + jnp.sum(pn, axis=1, keepdims=True)
        vnp = jnp.concatenate([by_head(vn_ref), pad], axis=0).astype(BF16)
        acc_ref[...] = jnp.dot(pn.astype(BF16), vnp, preferred_element_type=F32)


def _attn_decode(qarr, q_col, k4, v4, cache_k, cache_v, layer, page_table, DB, DS):
    depth, n_phys, page, nkv, hd = cache_k.shape
    n_pages = page_table.shape[1]
    past_len = n_pages * page
    kvw = nkv * hd
    assert (nkv, hd) == (N_KV_HEADS, HEAD_DIM) and MOBA_BLOCK == 2 * page
    assert past_len % MOBA_BLOCK == 0 and DS == SUBLANES
    nblk = past_len // MOBA_BLOCK
    pps = 32
    assert n_pages % pps == 0
    ns = n_pages // pps
    nrow = N_HEADS * DS
    ck = cache_k.reshape(depth * n_phys, page * nkv, hd)
    cv = cache_v.reshape(depth * n_phys, page * nkv, hd)
    base = layer * n_phys

    kseq = lambda r: jnp.minimum(r, DB - 1)
    vseq = lambda r: jnp.maximum(r - 1, 0)

    kstep = lambda r, s: jnp.where(r < DB, s, ns - 1)
    vstep = lambda r, s: jnp.where(r >= 1, s, 0)

    def k_map(i):
        return lambda r, s, pt: (base + pt[kseq(r), kstep(r, s) * pps + i], 0, 0)

    def v_map(i):
        return lambda r, s, pt: (base + pt[vseq(r), vstep(r, s) * pps + i], 0, 0)

    qw = N_HEADS * HEAD_DIM
    in_specs = [pl.BlockSpec((DS, qw), lambda r, s, pt: (kseq(r), q_col // qw)),
                pl.BlockSpec((DS * nkv, hd), lambda r, s, pt: (kseq(r), 0)),
                pl.BlockSpec((DS * nkv, hd), lambda r, s, pt: (kseq(r), 0))]
    in_specs += [pl.BlockSpec((None, page * nkv, hd), k_map(i)) for i in range(pps)]
    in_specs += [pl.BlockSpec((None, page * nkv, hd), v_map(i)) for i in range(pps)]
    kern = functools.partial(_attn_decode_kernel, pps=pps, ns=ns, nblk=nblk, ds=DS, nseq=DB,
                             scale=HEAD_DIM ** -0.5)
    grid_spec = pltpu.PrefetchScalarGridSpec(
        num_scalar_prefetch=1,
        grid=(DB + 1, ns),
        in_specs=in_specs,
        out_specs=pl.BlockSpec((DS, qw), lambda r, s, pt: (vseq(r), 0)),
        scratch_shapes=[pltpu.VMEM((nrow, kvw), F32),
                        pltpu.VMEM((nrow, kvw), BF16),
                        pltpu.VMEM((nblk, nrow, MOBA_BLOCK), F32),
                        pltpu.VMEM((nblk, nrow, MOBA_BLOCK), BF16),
                        pltpu.VMEM((nblk, kvw), F32),
                        pltpu.VMEM((nrow, 1), F32),
                        pltpu.VMEM((nrow, kvw), F32)])
    return pl.pallas_call(
        kern,
        grid_spec=grid_spec,
        out_shape=jax.ShapeDtypeStruct((DB * DS, qw), F32),
        compiler_params=_params("arbitrary", "arbitrary"),
        name="moba_decode",
    )(page_table, qarr, k4, v4, *([ck] * pps), *([cv] * pps))


def _rglru_kernel(xr_ref, yg_ref, h0_ref, cb_ref, cw_ref, cbias_ref, wa_ref, ba_ref, wx_ref, bx_ref,
                  lam_ref, rnn_ref, hlast_ref, cnew_ref, halo_ref, hcarry_ref, *, L):
    c = pl.program_id(1)
    C = xr_ref.shape[1]

    @pl.when(c == 0)
    def _():
        hcarry_ref[...] = h0_ref[...]
        halo_ref[...] = jnp.concatenate(
            [jnp.zeros((SUBLANES - (CONV_W - 1), C), F32), cb_ref[...]], axis=0)

    G = L // SUBLANES
    xr = xr_ref[...]
    xg = jnp.concatenate([halo_ref[...], xr], axis=0).reshape(G + 1, SUBLANES, C)
    rin = lax.broadcasted_iota(jnp.int32, (1, SUBLANES, 1), 1)
    cw = cw_ref[...]
    xc = cbias_ref[...].reshape(1, 1, C)
    for j in range(CONV_W):
        k = CONV_W - 1 - j
        if k == 0:
            xk = xg[1:]
        else:
            rolled = pltpu.roll(xg, k, 1)
            xk = jnp.where(rin >= k, rolled[1:], rolled[:-1])
        xc = xc + xk * cw[j:j + 1, :].reshape(1, 1, C)
    xc = xc.reshape(L, C)

    xcb = xc.astype(BF16)
    bw = C // RNN_BLOCKS
    r_parts, i_parts = [], []
    for n in range(RNN_BLOCKS):
        xs = xcb[:, n * bw:(n + 1) * bw]
        r_parts.append(jnp.dot(xs, wa_ref[n], preferred_element_type=F32))
        i_parts.append(jnp.dot(xs, wx_ref[n], preferred_element_type=F32))
    r = jax.nn.sigmoid(jnp.concatenate(r_parts, axis=1) + ba_ref[...])
    ig = jax.nn.sigmoid(jnp.concatenate(i_parts, axis=1) + bx_ref[...])

    nl = -lam_ref[...]
    softplus = jnp.maximum(nl, 0.0) + jnp.log1p(jnp.exp(-jnp.abs(nl)))
    log_a = -LRU_C * r * softplus
    a = jnp.exp(log_a)
    gain2 = -jnp.tanh(log_a) * (a * a + 1.0)
    gain = jnp.where(gain2 > 0.0, gain2 * lax.rsqrt(gain2), 0.0)
    bv = gain * (ig * xc)

    a = a.reshape(G, SUBLANES, C)
    bv = bv.reshape(G, SUBLANES, C)
    d = 1
    while d < SUBLANES:
        same_group = rin >= d
        bv = a * jnp.where(same_group, pltpu.roll(bv, d, 1), 0.0) + bv
        a = a * jnp.where(same_group, pltpu.roll(a, d, 1), 1.0)
        d *= 2
    carry = hcarry_ref[...]
    groups = []
    for g in range(G):
        hg = bv[g] + a[g] * carry
        groups.append(hg)
        carry = hg[SUBLANES - 1:SUBLANES, :]
    h = groups[0] if G == 1 else jnp.concatenate(groups, axis=0)
    hcarry_ref[...] = carry
    hlast_ref[...] = carry

    yg = yg_ref[...].astype(F32)
    cdf = 0.5 * (1.0 + jnp.tanh(0.7978845608028654 * (yg + 0.044715 * (yg * yg * yg))))
    rnn_ref[...] = (h * (yg * cdf)).astype(rnn_ref.dtype)

    halo_ref[...] = xr[L - SUBLANES:L, :]
    cnew_ref[...] = xr[L - (CONV_W - 1):L, :]


def _rglru(xr_arr, xr_col, yg_arr, yg_col, B, S, h0, conv_buf, conv_w, conv_b, wa, ba, wx, bx, lam):
    C = h0.shape[-1]
    L = min(S, 256)
    assert S % L == 0 and L % SUBLANES == 0 and xr_col % C == 0 and yg_col % C == 0
    nc = S // L
    out_dtype = BF16 if L % 16 == 0 else F32
    row = lambda v: v.reshape(1, C)
    const2 = lambda b, c: (0, 0)
    const3 = lambda b, c: (0, 0, 0)
    rnn, hlast, cnew = pl.pallas_call(
        functools.partial(_rglru_kernel, L=L),
        grid=(B, nc),
        in_specs=[pl.BlockSpec((L, C), lambda b, c: (b * nc + c, xr_col // C)),
                  pl.BlockSpec((L, C), lambda b, c: (b * nc + c, yg_col // C)),
                  pl.BlockSpec((None, 1, C), lambda b, c: (b, 0, 0)),
                  pl.BlockSpec((None, CONV_W - 1, C), lambda b, c: (b, 0, 0)),
                  pl.BlockSpec((CONV_W, C), const2),
                  pl.BlockSpec((1, C), const2),
                  pl.BlockSpec((RNN_BLOCKS, C // RNN_BLOCKS, C // RNN_BLOCKS), const3),
                  pl.BlockSpec((1, C), const2),
                  pl.BlockSpec((RNN_BLOCKS, C // RNN_BLOCKS, C // RNN_BLOCKS), const3),
                  pl.BlockSpec((1, C), const2),
                  pl.BlockSpec((1, C), const2)],
        out_specs=[pl.BlockSpec((L, C), lambda b, c: (b * nc + c, 0)),
                   pl.BlockSpec((None, 1, C), lambda b, c: (b, 0, 0)),
                   pl.BlockSpec((None, CONV_W - 1, C), lambda b, c: (b, 0, 0))],
        out_shape=[jax.ShapeDtypeStruct((B * S, C), out_dtype),
                   jax.ShapeDtypeStruct((B, 1, C), F32),
                   jax.ShapeDtypeStruct((B, CONV_W - 1, C), F32)],
        scratch_shapes=[pltpu.VMEM((SUBLANES, C), F32),
                        pltpu.VMEM((1, C), F32)],
        compiler_params=_params("arbitrary", "arbitrary"),
        name="rglru",
    )(xr_arr, yg_arr, h0.reshape(B, 1, C), conv_buf, conv_w, row(conv_b), wa, row(ba), wx, row(bx), row(lam))
    return rnn, hlast.reshape(B, C), cnew


def _merge_kernel(rnn_ref, attn_ref, u_ref, wa_ref, wb_ref, wga_ref, wgb_ref, bga_ref, bgb_ref, o_ref):
    u = u_ref[...]
    a = jnp.dot(rnn_ref[...].astype(BF16), wa_ref[...], preferred_element_type=F32)
    b = jnp.dot(attn_ref[...].astype(BF16), wb_ref[...], preferred_element_type=F32)
    ga = jax.nn.sigmoid(jnp.dot(u, wga_ref[...], preferred_element_type=F32) + bga_ref[...])
    gb = jax.nn.sigmoid(jnp.dot(u, wgb_ref[...], preferred_element_type=F32) + bgb_ref[...])
    o_ref[...] = (ga * a + gb * b).astype(o_ref.dtype)


def _merge(rnn, attn, u, wa, wb, w_in, gate_col, b_gate):
    T, D = u.shape
    tm = _row_tile(T, 1024)
    tn = COL_CHUNK
    nb = D // tn
    assert gate_col % tn == 0
    act = lambda m, n: (m, 0)
    col = lambda m, n: (0, n)
    col_hi = lambda m, n: (0, n + nb)
    gcol = lambda m, n: (0, gate_col // tn + n)
    gcol_hi = lambda m, n: (0, gate_col // tn + nb + n)
    bg = b_gate.reshape(1, 2 * D)
    return pl.pallas_call(
        _merge_kernel,
        grid=(T // tm, nb),
        in_specs=[pl.BlockSpec((tm, D), act), pl.BlockSpec((tm, D), act), pl.BlockSpec((tm, D), act),
                  pl.BlockSpec((D, tn), col), pl.BlockSpec((D, tn), col),
                  pl.BlockSpec((D, tn), gcol), pl.BlockSpec((D, tn), gcol_hi),
                  pl.BlockSpec((1, tn), col), pl.BlockSpec((1, tn), col_hi)],
        out_specs=pl.BlockSpec((tm, tn), lambda m, n: (m, n)),
        out_shape=jax.ShapeDtypeStruct((T, D), BF16),
        compiler_params=_params("arbitrary", "arbitrary"),
        name="merge",
    )(rnn, attn, u, wa, wb, w_in, w_in, bg, bg)


def _outproj_kernel(mg_ref, x_ref, w_ref, g_ref, x2_ref, u2_ref):
    mg = mg_ref[...]
    D = x_ref.shape[1]
    ssq = jnp.zeros((x_ref.shape[0], 1), F32)
    for c in range(D // COL_CHUNK):
        sl = slice(c * COL_CHUNK, (c + 1) * COL_CHUNK)
        y = x_ref[:, sl] + jnp.dot(mg, w_ref[:, sl], preferred_element_type=F32)
        x2_ref[:, sl] = y
        ssq = ssq + jnp.sum(y * y, axis=-1, keepdims=True)
    inv = lax.rsqrt(ssq / D + RMS_EPS)
    u2_ref[...] = (x2_ref[...] * inv * g_ref[...]).astype(u2_ref.dtype)


def _outproj(merged, x, w_out, g):
    T, D = x.shape
    tm = _row_tile(T, 512)
    row = lambda m: (m, 0)
    return pl.pallas_call(
        _outproj_kernel,
        grid=(T // tm,),
        in_specs=[pl.BlockSpec((tm, D), row), pl.BlockSpec((tm, D), row),
                  pl.BlockSpec((D, D), lambda m: (0, 0)), pl.BlockSpec((1, D), lambda m: (0, 0))],
        out_specs=[pl.BlockSpec((tm, D), row), pl.BlockSpec((tm, D), row)],
        out_shape=[jax.ShapeDtypeStruct((T, D), F32), jax.ShapeDtypeStruct((T, D), BF16)],
        compiler_params=_params("arbitrary"),
        name="out_proj",
    )(merged, x, w_out, g.reshape(1, D))


def _ffn_kernel(u2_ref, x2_ref, w1_ref, w2_ref, gf_ref, y_ref, *, nf, final_norm):
    f = pl.program_id(1)

    @pl.when(f == 0)
    def _():
        y_ref[...] = x2_ref[...]

    h = jnp.dot(u2_ref[...], w1_ref[...], preferred_element_type=F32)
    h = jnp.square(jnp.maximum(h, 0.0)).astype(BF16)
    for c in range(y_ref.shape[1] // COL_CHUNK):
        sl = slice(c * COL_CHUNK, (c + 1) * COL_CHUNK)
        y_ref[:, sl] += jnp.dot(h, w2_ref[:, sl], preferred_element_type=F32)

    if final_norm:
        @pl.when(f == nf - 1)
        def _():
            x3 = y_ref[...]
            ms = jnp.mean(x3 * x3, axis=-1, keepdims=True)
            y_ref[...] = x3 * lax.rsqrt(ms + RMS_EPS) * gf_ref[...]


def _ffn(u2, x2, w1, w2, g_final, final_norm):
    T, D = x2.shape
    F = w1.shape[1]
    tm = _row_tile(T, 1024)
    tf = 512 if tm == 1024 else 1024
    nf = F // tf
    row = lambda m, f: (m, 0)
    return pl.pallas_call(
        functools.partial(_ffn_kernel, nf=nf, final_norm=final_norm),
        grid=(T // tm, nf),
        in_specs=[pl.BlockSpec((tm, D), row), pl.BlockSpec((tm, D), row),
                  pl.BlockSpec((D, tf), lambda m, f: (0, f)), pl.BlockSpec((tf, D), lambda m, f: (f, 0)),
                  pl.BlockSpec((1, D), lambda m, f: (0, 0))],
        out_specs=pl.BlockSpec((tm, D), row),
        out_shape=jax.ShapeDtypeStruct((T, D), F32),
        compiler_params=_params("arbitrary", "arbitrary"),
        name="ffn",
    )(u2, x2, w1, w2, g_final.reshape(1, D))


def _rope_tables(pos):
    half = HEAD_DIM // 2
    inv = ROPE_THETA ** (-jnp.arange(half, dtype=F32) * (2.0 / HEAD_DIM))
    ang = pos.astype(F32)[:, None] * inv[None, :]
    cos, sin = jnp.cos(ang), jnp.sin(ang)
    return jnp.concatenate([cos, cos], axis=-1), jnp.concatenate([-sin, sin], axis=-1)


def _decoder_layer(x, B, S, cos, sin, past, h0, conv_buf, w, final_norm):
    C = h0.shape[-1]
    attn_w = N_HEADS * HEAD_DIM
    act_dtype = BF16 if S % 16 == 0 else F32

    kv_w = N_KV_HEADS * HEAD_DIM
    q_col, k_col, v_col = 0, attn_w, attn_w + kv_w
    xr_col = v_col + kv_w
    yg_col = xr_col + C
    gate_col = yg_col + C
    w_in = w["w_in"]

    u, xr = _norm_proj(x, w["norm_mix"], w_in, xr_col, C, F32)
    ygq = _proj(u, w_in, [(yg_col, C), (q_col, attn_w)], cos, sin,
                (C, C + attn_w), act_dtype)
    k4, v4 = _kv_proj(u, w_in, k_col, v_col, cos, sin)
    if past is None:
        attn = _attn_prompt(ygq, C, k4, v4, B, S)
    else:
        attn = _attn_decode(ygq, C, k4, v4, past[0], past[1], past[2], past[3], B, S)
    rnn, h_last, conv_new = _rglru(xr, 0, ygq, 0, B, S, h0, conv_buf, w["conv_w"], w["conv_b"],
                                   w["w_rg_a"], w["b_rg_a"], w["w_rg_x"], w["b_rg_x"], w["lru_lambda"])
    merged = _merge(rnn, attn, u, w["w_proj_a"], w["w_proj_b"], w_in, gate_col, w["b_gate"])
    x2, u2 = _outproj(merged, x, w["w_out"], w["norm_mlp"])
    y = _ffn(u2, x2, w["w_ff1"], w["w_ff2"], w["norm_final"], final_norm)
    k = k4.reshape(B, S, N_KV_HEADS, HEAD_DIM)
    v = v4.reshape(B, S, N_KV_HEADS, HEAD_DIM)
    return y, k, v, h_last, conv_new


def kernel(x_prompt, x_sample, cache_k, cache_v, state_h, state_conv, page_table, norm_mix, w_in, b_gate, conv_w, conv_b, w_rg_a, b_rg_a, w_rg_x, b_rg_x, lru_lambda, w_proj_a, w_proj_b, w_out, norm_mlp, w_ff1, w_ff2, norm_final):
    B, S, D = x_prompt.shape
    DB, DS, _ = x_sample.shape
    depth = w_in.shape[0]
    C = state_h.shape[-1]
    past_len = page_table.shape[1] * cache_k.shape[2]

    cos_p, sin_p = _rope_tables(jnp.arange(S, dtype=jnp.int32))
    cos_s, sin_s = _rope_tables(past_len + jnp.arange(DS, dtype=jnp.int32))
    cos_s, sin_s = jnp.tile(cos_s, (DB, 1)), jnp.tile(sin_s, (DB, 1))

    hp = x_prompt.reshape(B * S, D)
    hs = x_sample.reshape(DB * DS, D)
    outs = [[] for _ in range(8)]
    for l in range(depth):
        wl = w_in[l]
        w = dict(
            norm_mix=norm_mix[l], norm_mlp=norm_mlp[l], norm_final=norm_final,
            w_in=wl.astype(BF16), b_gate=b_gate[l],
            conv_w=conv_w[l], conv_b=conv_b[l],
            w_rg_a=w_rg_a[l].astype(BF16), b_rg_a=b_rg_a[l],
            w_rg_x=w_rg_x[l].astype(BF16), b_rg_x=b_rg_x[l], lru_lambda=lru_lambda[l],
            w_proj_a=w_proj_a[l].astype(BF16), w_proj_b=w_proj_b[l].astype(BF16),
            w_out=w_out[l].astype(BF16), w_ff1=w_ff1[l].astype(BF16), w_ff2=w_ff2[l].astype(BF16))
        final = l == depth - 1
        h0_p = jnp.zeros((B, C), F32)
        buf_p = jnp.zeros((B, CONV_W - 1, C), F32)
        hp, k1, v1, r1, c1 = _decoder_layer(hp, B, S, cos_p, sin_p, None, h0_p, buf_p, w, final)
        past = (cache_k, cache_v, l, page_table)
        hs, k2, v2, r2, c2 = _decoder_layer(hs, DB, DS, cos_s, sin_s, past, state_h[l], state_conv[l], w, final)
        for lst, val in zip(outs, (k1, v1, r1, c1, k2, v2, r2, c2)):
            lst.append(val)
    return (hp.reshape(B, S, D), hs.reshape(DB, DS, D), *[jnp.stack(o) for o in outs])
```

```python
import functools

import jax
import jax.numpy as jnp
from jax import lax
from jax.experimental import pallas as pl
from jax.experimental.pallas import tpu as pltpu

F32 = jnp.float32
BF16 = jnp.bfloat16

N_HEADS = 16
N_KV_HEADS = 4
HEAD_DIM = 128
KV_GROUP = N_HEADS // N_KV_HEADS
ROPE_THETA = 10000.0
MOBA_BLOCK = 256
MOBA_TOPK = 3
RNN_BLOCKS = 16
CONV_W = 4
LRU_C = 8.0
RMS_EPS = 1e-6
NEG_INF = -1e30
LOG2_E = 1.4426950408889634

LANES = 128
SUBLANES = 8
VMEM_LIMIT_BYTES = 56 * 1024 * 1024
COL_CHUNK = 512

_NT = (((1,), (1,)), ((), ()))


def _params(*sem, flags=None):
    return pltpu.CompilerParams(dimension_semantics=sem, vmem_limit_bytes=VMEM_LIMIT_BYTES, flags=flags)


def _row_tile(rows, pref):
    t = min(rows, pref)
    assert rows % t == 0, (rows, pref)
    return t


def _proj_kernel(u_ref, w_ref, cos_ref, sin_ref, o_ref, *, rope_lo, rope_hi):
    acc = jnp.dot(u_ref[...], w_ref[...], preferred_element_type=F32)
    n = pl.program_id(1)
    roped = jnp.logical_and(n >= rope_lo, n < rope_hi)

    @pl.when(roped)
    def _():
        cos = cos_ref[...]
        sin = sin_ref[...]
        for c in range(acc.shape[1] // HEAD_DIM):
            sl = slice(c * HEAD_DIM, (c + 1) * HEAD_DIM)
            o_ref[:, sl] = _rope(acc[:, sl], cos, sin).astype(o_ref.dtype)

    @pl.when(jnp.logical_not(roped))
    def _():
        o_ref[...] = acc.astype(o_ref.dtype)


def _rope(xh, cos, sin):
    return xh * cos + pltpu.roll(xh, HEAD_DIM // 2, 1) * sin


def _segment_block(segments, tn):
    assert all(s % tn == 0 and w % tn == 0 for s, w in segments)

    def block(n):
        idx, first = None, 0
        for start, width in segments:
            here = start // tn + n - first
            idx = here if idx is None else jnp.where(n >= first, here, idx)
            first += width // tn
        return idx

    return block


def _proj(u, w, segments, cos, sin, rope_cols, out_dtype):
    T, D = u.shape
    N = sum(width for _, width in segments)
    tm = _row_tile(T, 1024)
    tn = 2 * COL_CHUNK
    n_tab = cos.shape[0] // tm
    assert cos.shape[0] % tm == 0 and rope_cols[0] % tn == 0 and rope_cols[1] % tn == 0
    kern = functools.partial(_proj_kernel, rope_lo=rope_cols[0] // tn, rope_hi=rope_cols[1] // tn)
    wblock = _segment_block(segments, tn)
    return pl.pallas_call(
        kern,
        grid=(T // tm, N // tn),
        in_specs=[pl.BlockSpec((tm, D), lambda m, n: (m, 0)),
                  pl.BlockSpec((D, tn), lambda m, n: (0, wblock(n))),
                  pl.BlockSpec((tm, HEAD_DIM), lambda m, n: (m % n_tab, 0)),
                  pl.BlockSpec((tm, HEAD_DIM), lambda m, n: (m % n_tab, 0))],
        out_specs=pl.BlockSpec((tm, tn), lambda m, n: (m, n)),
        out_shape=jax.ShapeDtypeStruct((T, N), out_dtype),
        compiler_params=_params("arbitrary", "arbitrary"),
        name="in_proj",
    )(u, w, cos, sin)


def _norm_proj_kernel(x_ref, g_ref, w_ref, u_ref, o_ref):
    @pl.when(pl.program_id(1) == 0)
    def _():
        x = x_ref[...]
        ms = jnp.mean(x * x, axis=-1, keepdims=True)
        u_ref[...] = (x * lax.rsqrt(ms + RMS_EPS) * g_ref[...]).astype(u_ref.dtype)

    o_ref[...] = jnp.dot(u_ref[...], w_ref[...], preferred_element_type=F32).astype(o_ref.dtype)


def _norm_proj(x, g, w, col, width, out_dtype):
    T, D = x.shape
    tm = _row_tile(T, 1024)
    tn = 2 * COL_CHUNK
    assert col % tn == 0 and width % tn == 0
    return pl.pallas_call(
        _norm_proj_kernel,
        grid=(T // tm, width // tn),
        in_specs=[pl.BlockSpec((tm, D), lambda m, n: (m, 0)),
                  pl.BlockSpec((1, D), lambda m, n: (0, 0)),
                  pl.BlockSpec((D, tn), lambda m, n: (0, col // tn + n))],
        out_specs=[pl.BlockSpec((tm, D), lambda m, n: (m, 0)),
                   pl.BlockSpec((tm, tn), lambda m, n: (m, n))],
        out_shape=[jax.ShapeDtypeStruct((T, D), BF16), jax.ShapeDtypeStruct((T, width), out_dtype)],
        compiler_params=_params("arbitrary", "arbitrary"),
        name="norm_proj",
    )(x, g.reshape(1, D), w)


def _kv_proj_kernel(u_ref, wk_ref, wv_ref, cos_ref, sin_ref, k_ref, v_ref):
    u = u_ref[...]
    tm = u.shape[0]
    cos = cos_ref[...]
    sin = sin_ref[...]
    k = jnp.dot(u, wk_ref[...], preferred_element_type=F32)
    for g in range(N_KV_HEADS):
        k_ref[pl.ds(g, tm, stride=N_KV_HEADS), :] = _rope(k[:, g * HEAD_DIM:(g + 1) * HEAD_DIM], cos, sin)
    v = jnp.dot(u, wv_ref[...], preferred_element_type=F32)
    for g in range(N_KV_HEADS):
        v_ref[pl.ds(g, tm, stride=N_KV_HEADS), :] = v[:, g * HEAD_DIM:(g + 1) * HEAD_DIM]


def _kv_proj(u, w, k_col, v_col, cos, sin):
    T, D = u.shape
    kvw = N_KV_HEADS * HEAD_DIM
    assert k_col % kvw == 0 and v_col % kvw == 0
    tm = _row_tile(T, 1024)
    n_tab = cos.shape[0] // tm
    assert cos.shape[0] % tm == 0
    out = jax.ShapeDtypeStruct((T * N_KV_HEADS, HEAD_DIM), F32)
    out_spec = pl.BlockSpec((tm * N_KV_HEADS, HEAD_DIM), lambda m: (m, 0))
    return pl.pallas_call(
        _kv_proj_kernel,
        grid=(T // tm,),
        in_specs=[pl.BlockSpec((tm, D), lambda m: (m, 0)),
                  pl.BlockSpec((D, kvw), lambda m: (0, k_col // kvw)),
                  pl.BlockSpec((D, kvw), lambda m: (0, v_col // kvw)),
                  pl.BlockSpec((tm, HEAD_DIM), lambda m: (m % n_tab, 0)),
                  pl.BlockSpec((tm, HEAD_DIM), lambda m: (m % n_tab, 0))],
        out_specs=[out_spec, out_spec],
        out_shape=[out, out],
        compiler_params=_params("arbitrary"),
        name="kv_proj",
    )(u, w, w, cos, sin)


def _topk_select(gate, blk, eligible, axis):
    gm = jnp.where(eligible, gate, NEG_INF)
    rank = jnp.zeros(gate.shape, jnp.int32)
    for r in range(gate.shape[axis]):
        gr = gm[r:r + 1, :] if axis == 0 else gm[:, r:r + 1]
        beats = jnp.logical_or(gr > gm, jnp.logical_and(gr == gm, blk > r))
        rank = rank + beats.astype(jnp.int32)
    return jnp.logical_and(rank < MOBA_TOPK, eligible)


def _attn_prompt_kernel(q_ref, k_ref, v_ref, o_ref,
                        kb_ref, vt_ref, kmean_ref, kmh_ref, sel_ref, qt_ref, s_ref, m_ref, l_ref, acc_ref,
                        *, tq, nblk, scale):
    g = pl.program_id(1)
    i = pl.program_id(2)

    @pl.when(i == 0)
    def _():
        for j in range(nblk):
            rows = pl.ds(j * MOBA_BLOCK * N_KV_HEADS + g, MOBA_BLOCK, stride=N_KV_HEADS)
            kj = k_ref[rows, :]
            kb_ref[j] = kj.astype(BF16)
            kmean_ref[j:j + 1, :] = jnp.mean(kj, axis=0, keepdims=True)
            vt_ref[j] = v_ref[rows, :].T.astype(BF16)
        km = kmean_ref[...]
        km_hi = km.astype(BF16)
        kmh_ref[...] = jnp.concatenate([km_hi, (km - km_hi.astype(F32)).astype(BF16)], axis=0)

    q = q_ref[...].astype(F32)
    qs = jnp.concatenate([q[:, h * HEAD_DIM:(h + 1) * HEAD_DIM] for h in range(KV_GROUP)], axis=0)
    qt = (qs * (scale * LOG2_E)).T
    qt_hi = qt.astype(BF16)
    qt_lo = (qt - qt_hi.astype(F32)).astype(BF16)
    qt_ref[...] = qt_hi

    def scores(j):
        return jnp.dot(kb_ref[j], qt_ref[...], preferred_element_type=F32)

    ext = jnp.dot(jnp.concatenate([kb_ref[i], kmh_ref[...]], axis=0), qt_hi,
                  preferred_element_type=F32)
    gate = (ext[MOBA_BLOCK:MOBA_BLOCK + nblk] + ext[MOBA_BLOCK + nblk:]) + jnp.dot(
        kmh_ref[:nblk, :], qt_lo, preferred_element_type=F32)
    s_ref[0] = scores(0)
    blk = lax.broadcasted_iota(jnp.int32, gate.shape, 0)
    sel = _topk_select(gate, blk, blk < i, axis=0)
    sel_ref[...] = sel.astype(F32)

    s = ext[:MOBA_BLOCK]
    key_t = lax.broadcasted_iota(jnp.int32, s.shape, 0)
    qry_t = lax.broadcasted_iota(jnp.int32, s.shape, 1) % tq
    s = jnp.where(key_t <= qry_t, s, NEG_INF)
    half = s.shape[1] // 2
    for c in range(2):
        cols = slice(c * half, (c + 1) * half)
        sc = s[:, cols]
        m0 = jnp.max(sc, axis=0, keepdims=True)
        p = jnp.exp2(sc - m0)
        m_ref[:, cols] = m0
        l_ref[:, cols] = jnp.sum(p, axis=0, keepdims=True)
        acc_ref[:, cols] = jnp.dot(vt_ref[i], p.astype(BF16), preferred_element_type=F32)

    def absorb(slot, j):
        half = s_ref.shape[2] // 2
        for c in range(2):
            cols = slice(c * half, (c + 1) * half)
            s = s_ref[slot, :, cols]
            picked = sel_ref[pl.ds(j, 1), cols] > 0.5
            m_old = m_ref[:, cols]
            m_new = jnp.maximum(m_old, jnp.where(picked, jnp.max(s, axis=0, keepdims=True), NEG_INF))
            alpha = jnp.exp2(m_old - m_new)
            p = jnp.exp2(s - jnp.where(picked, m_new, -NEG_INF))
            m_ref[:, cols] = m_new
            l_ref[:, cols] = alpha * l_ref[:, cols] + jnp.sum(p, axis=0, keepdims=True)
            acc_ref[:, cols] = alpha * acc_ref[:, cols] + jnp.dot(vt_ref[j], p.astype(BF16),
                                                                  preferred_element_type=F32)

    def pair(ja):
        s_ref[1] = scores(ja + 1)
        absorb(0, ja)
        s_ref[0] = scores(jnp.minimum(ja + 2, nblk - 1))
        absorb(1, ja + 1)

    def body(t, carry):
        pair(4 * t)
        pair(4 * t + 2)
        return carry

    quads = i // 4
    lax.fori_loop(0, quads, body, 0)

    @pl.when(i % 4 >= 2)
    def _():
        pair(4 * quads)

    @pl.when(i % 2 == 1)
    def _():
        absorb(0, i - 1)

    out = acc_ref[...] * (1.0 / l_ref[...])
    for h in range(KV_GROUP):
        o_ref[:, h * HEAD_DIM:(h + 1) * HEAD_DIM] = out[:, h * tq:(h + 1) * tq].T.astype(o_ref.dtype)


def _attn_prompt(qarr, q_col, k4, v4, B, S):
    tq = MOBA_BLOCK
    assert S % MOBA_BLOCK == 0
    nblk = S // MOBA_BLOCK
    nq = S // tq
    gw = KV_GROUP * HEAD_DIM
    n = KV_GROUP * tq
    kern = functools.partial(_attn_prompt_kernel, tq=tq, nblk=nblk, scale=HEAD_DIM ** -0.5)
    return pl.pallas_call(
        kern,
        grid=(B, N_KV_HEADS, nq),
        in_specs=[pl.BlockSpec((tq, gw), lambda b, g, i: (b * nq + i, q_col // gw + g)),
                  pl.BlockSpec((S * N_KV_HEADS, HEAD_DIM), lambda b, g, i: (b, 0)),
                  pl.BlockSpec((S * N_KV_HEADS, HEAD_DIM), lambda b, g, i: (b, 0))],
        out_specs=pl.BlockSpec((tq, gw), lambda b, g, i: (b * nq + i, g)),
        out_shape=jax.ShapeDtypeStruct((B * S, N_HEADS * HEAD_DIM), BF16),
        scratch_shapes=[pltpu.VMEM((nblk, MOBA_BLOCK, HEAD_DIM), BF16),
                        pltpu.VMEM((nblk, HEAD_DIM, MOBA_BLOCK), BF16),
                        pltpu.VMEM((nblk, HEAD_DIM), F32),
                        pltpu.VMEM((2 * nblk, HEAD_DIM), BF16),
                        pltpu.VMEM((nblk, n), F32),
                        pltpu.VMEM((HEAD_DIM, n), BF16),
                        pltpu.VMEM((2, MOBA_BLOCK, n), F32),
                        pltpu.VMEM((1, n), F32),
                        pltpu.VMEM((1, n), F32),
                        pltpu.VMEM((HEAD_DIM, n), F32)],
        compiler_params=_params("arbitrary", "arbitrary", "arbitrary"),
        name="moba_prompt",
    )(qarr, k4, v4)


def _attn_decode_kernel(pt_ref, q_ref, kn_ref, vn_ref, *refs, pps, ns, nblk, ds, nseq, scale):
    del pt_ref
    k_refs = refs[:pps]
    v_refs = refs[pps:2 * pps]
    o_ref = refs[2 * pps]
    wf_ref, wb_ref, s_ref, p_ref, kmean_ref, l_ref, acc_ref = refs[2 * pps + 1:]
    r = pl.program_id(0)
    step = pl.program_id(1)
    bps = pps // 2
    kvw = N_KV_HEADS * HEAD_DIM
    nrow = N_HEADS * ds
    has_values = r >= 1
    has_keys = r < nseq

    def by_head(ref):
        tokens = ref.shape[0] // N_KV_HEADS
        return jnp.concatenate([ref[pl.ds(g, tokens, stride=N_KV_HEADS), :]
                                for g in range(N_KV_HEADS)], axis=1)

    def block_of(page_refs, jj):
        return jnp.concatenate([by_head(page_refs[2 * jj]), by_head(page_refs[2 * jj + 1])], axis=0)

    @pl.when(has_values)
    def _():
        for jj in range(bps):
            j = step * bps + jj
            vblk = block_of(v_refs, jj).astype(BF16)
            acc_ref[...] += jnp.dot(p_ref[j], vblk, preferred_element_type=F32)

    @pl.when(jnp.logical_and(has_values, step == ns - 1))
    def _():
        acc = acc_ref[...] * (1.0 / l_ref[...])
        outs = []
        for h in range(N_HEADS):
            g = h // KV_GROUP
            outs.append(acc[h * ds:(h + 1) * ds, g * HEAD_DIM:(g + 1) * HEAD_DIM])
        o_ref[...] = jnp.concatenate(outs, axis=1)

    @pl.when(jnp.logical_and(has_keys, step == 0))
    def _():
        q = q_ref[...]
        zero = jnp.zeros((ds, HEAD_DIM), F32)
        rows = []
        for h in range(N_HEADS):
            pieces = [zero] * N_KV_HEADS
            pieces[h // KV_GROUP] = q[:, h * HEAD_DIM:(h + 1) * HEAD_DIM]
            rows.append(jnp.concatenate(pieces, axis=1))
        wf = jnp.concatenate(rows, axis=0)
        wf_ref[...] = wf
        wb_ref[...] = (wf * scale).astype(BF16)

    @pl.when(has_keys)
    def _():
        for jj in range(bps):
            j = step * bps + jj
            kblk = block_of(k_refs, jj)
            kmean_ref[pl.ds(j, 1), :] = jnp.mean(kblk, axis=0, keepdims=True)
            s_ref[j] = lax.dot_general(wb_ref[...], kblk.astype(BF16), _NT,
                                       preferred_element_type=F32)

    @pl.when(jnp.logical_and(has_keys, step == ns - 1))
    def _():
        gate = lax.dot_general(kmean_ref[...], wf_ref[...], _NT, precision=lax.Precision.HIGHEST,
                               preferred_element_type=F32)
        blk = lax.broadcasted_iota(jnp.int32, gate.shape, 0)
        sel = _topk_select(gate, blk, blk < nblk, axis=0)
        bias = jnp.concatenate([jnp.where(sel, 0.0, NEG_INF),
                                jnp.zeros((nrow - nblk, nrow), F32)], axis=0).T

        pad = jnp.zeros((nrow - ds, kvw), F32)
        knp = jnp.concatenate([by_head(kn_ref), pad], axis=0).astype(BF16)
        sn = lax.dot_general(wb_ref[...], knp, _NT, preferred_element_type=F32)
        qry_t = lax.broadcasted_iota(jnp.int32, sn.shape, 0) % ds
        key_t = lax.broadcasted_iota(jnp.int32, sn.shape, 1)
        sn = jnp.where(key_t <= qry_t, sn, NEG_INF)

        mx = jnp.full((nrow, MOBA_BLOCK), NEG_INF, F32)
        for j in range(nblk):
            sb = s_ref[j] + bias[:, j:j + 1]
            s_ref[j] = sb
            mx = jnp.maximum(mx, sb)
        m = jnp.maximum(jnp.max(mx, axis=1, keepdims=True), jnp.max(sn, axis=1, keepdims=True))
        lsum = jnp.zeros((nrow, MOBA_BLOCK), F32)
        for j in range(nblk):
            p = jnp.exp(s_ref[j] - m)
            lsum = lsum + p
            p_ref[j] = p.astype(BF16)
        pn = jnp.exp(sn - m)
        l_ref[...] = jnp.sum(lsum, axis=1, keepdims=True) + jnp.sum(pn, axis=1, keepdims=True)
        vnp = jnp.concatenate([by_head(vn_ref), pad], axis=0).astype(BF16)
        acc_ref[...] = jnp.dot(pn.astype(BF16), vnp, preferred_element_type=F32)


def _attn_decode(qarr, q_col, k4, v4, cache_k, cache_v, layer, page_table, DB, DS):
    depth, n_phys, page, nkv, hd = cache_k.shape
    n_pages = page_table.shape[1]
    past_len = n_pages * page
    kvw = nkv * hd
    assert (nkv, hd) == (N_KV_HEADS, HEAD_DIM) and MOBA_BLOCK == 2 * page
    assert past_len % MOBA_BLOCK == 0 and DS == SUBLANES
    nblk = past_len // MOBA_BLOCK
    pps = 32
    assert n_pages % pps == 0
    ns = n_pages // pps
    nrow = N_HEADS * DS
    ck = cache_k.reshape(depth * n_phys, page * nkv, hd)
    cv = cache_v.reshape(depth * n_phys, page * nkv, hd)
    base = layer * n_phys

    kseq = lambda r: jnp.minimum(r, DB - 1)
    vseq = lambda r: jnp.maximum(r - 1, 0)

    kstep = lambda r, s: jnp.where(r < DB, s, ns - 1)
    vstep = lambda r, s: jnp.where(r >= 1, s, 0)

    def k_map(i):
        return lambda r, s, pt: (base + pt[kseq(r), kstep(r, s) * pps + i], 0, 0)

    def v_map(i):
        return lambda r, s, pt: (base + pt[vseq(r), vstep(r, s) * pps + i], 0, 0)

    qw = N_HEADS * HEAD_DIM
    in_specs = [pl.BlockSpec((DS, qw), lambda r, s, pt: (kseq(r), q_col // qw)),
                pl.BlockSpec((DS * nkv, hd), lambda r, s, pt: (kseq(r), 0)),
                pl.BlockSpec((DS * nkv, hd), lambda r, s, pt: (kseq(r), 0))]
    in_specs += [pl.BlockSpec((None, page * nkv, hd), k_map(i)) for i in range(pps)]
    in_specs += [pl.BlockSpec((None, page * nkv, hd), v_map(i)) for i in range(pps)]
    kern = functools.partial(_attn_decode_kernel, pps=pps, ns=ns, nblk=nblk, ds=DS, nseq=DB,
                             scale=HEAD_DIM ** -0.5)
    grid_spec = pltpu.PrefetchScalarGridSpec(
        num_scalar_prefetch=1,
        grid=(DB + 1, ns),
        in_specs=in_specs,
        out_specs=pl.BlockSpec((DS, qw), lambda r, s, pt: (vseq(r), 0)),
        scratch_shapes=[pltpu.VMEM((nrow, kvw), F32),
                        pltpu.VMEM((nrow, kvw), BF16),
                        pltpu.VMEM((nblk, nrow, MOBA_BLOCK), F32),
                        pltpu.VMEM((nblk, nrow, MOBA_BLOCK), BF16),
                        pltpu.VMEM((nblk, kvw), F32),
                        pltpu.VMEM((nrow, 1), F32),
                        pltpu.VMEM((nrow, kvw), F32)])
    return pl.pallas_call(
        kern,
        grid_spec=grid_spec,
        out_shape=jax.ShapeDtypeStruct((DB * DS, qw), F32),
        compiler_params=_params("arbitrary", "arbitrary"),
        name="moba_decode",
    )(page_table, qarr, k4, v4, *([ck] * pps), *([cv] * pps))


def _rglru_kernel(xr_ref, yg_ref, h0_ref, cb_ref, cw_ref, cbias_ref, wa_ref, ba_ref, wx_ref, bx_ref,
                  lam_ref, rnn_ref, hlast_ref, cnew_ref, halo_ref, hcarry_ref, *, L):
    c = pl.program_id(1)
    C = xr_ref.shape[1]

    @pl.when(c == 0)
    def _():
        hcarry_ref[...] = h0_ref[...]
        halo_ref[...] = jnp.concatenate(
            [jnp.zeros((SUBLANES - (CONV_W - 1), C), F32), cb_ref[...]], axis=0)

    G = L // SUBLANES
    xr = xr_ref[...]
    xg = jnp.concatenate([halo_ref[...], xr], axis=0).reshape(G + 1, SUBLANES, C)
    rin = lax.broadcasted_iota(jnp.int32, (1, SUBLANES, 1), 1)
    cw = cw_ref[...]
    xc = cbias_ref[...].reshape(1, 1, C)
    for j in range(CONV_W):
        k = CONV_W - 1 - j
        if k == 0:
            xk = xg[1:]
        else:
            rolled = pltpu.roll(xg, k, 1)
            xk = jnp.where(rin >= k, rolled[1:], rolled[:-1])
        xc = xc + xk * cw[j:j + 1, :].reshape(1, 1, C)
    xc = xc.reshape(L, C)

    xcb = xc.astype(BF16)
    bw = C // RNN_BLOCKS
    r_parts, i_parts = [], []
    for n in range(RNN_BLOCKS):
        xs = xcb[:, n * bw:(n + 1) * bw]
        r_parts.append(jnp.dot(xs, wa_ref[n], preferred_element_type=F32))
        i_parts.append(jnp.dot(xs, wx_ref[n], preferred_element_type=F32))
    r = jax.nn.sigmoid(jnp.concatenate(r_parts, axis=1) + ba_ref[...])
    ig = jax.nn.sigmoid(jnp.concatenate(i_parts, axis=1) + bx_ref[...])

    nl = -lam_ref[...]
    softplus = jnp.maximum(nl, 0.0) + jnp.log1p(jnp.exp(-jnp.abs(nl)))
    log_a = -LRU_C * r * softplus
    a = jnp.exp(log_a)
    gain2 = -jnp.tanh(log_a) * (a * a + 1.0)
    gain = jnp.where(gain2 > 0.0, gain2 * lax.rsqrt(gain2), 0.0)
    bv = gain * (ig * xc)

    a = a.reshape(G, SUBLANES, C)
    bv = bv.reshape(G, SUBLANES, C)
    d = 1
    while d < SUBLANES:
        same_group = rin >= d
        bv = a * jnp.where(same_group, pltpu.roll(bv, d, 1), 0.0) + bv
        a = a * jnp.where(same_group, pltpu.roll(a, d, 1), 1.0)
        d *= 2
    carry = hcarry_ref[...]
    groups = []
    for g in range(G):
        hg = bv[g] + a[g] * carry
        groups.append(hg)
        carry = hg[SUBLANES - 1:SUBLANES, :]
    h = groups[0] if G == 1 else jnp.concatenate(groups, axis=0)
    hcarry_ref[...] = carry
    hlast_ref[...] = carry

    yg = yg_ref[...].astype(F32)
    cdf = 0.5 * (1.0 + jnp.tanh(0.7978845608028654 * (yg + 0.044715 * (yg * yg * yg))))
    rnn_ref[...] = (h * (yg * cdf)).astype(rnn_ref.dtype)

    halo_ref[...] = xr[L - SUBLANES:L, :]
    cnew_ref[...] = xr[L - (CONV_W - 1):L, :]


def _rglru(xr_arr, xr_col, yg_arr, yg_col, B, S, h0, conv_buf, conv_w, conv_b, wa, ba, wx, bx, lam):
    C = h0.shape[-1]
    L = min(S, 256)
    assert S % L == 0 and L % SUBLANES == 0 and xr_col % C == 0 and yg_col % C == 0
    nc = S // L
    out_dtype = BF16 if L % 16 == 0 else F32
    row = lambda v: v.reshape(1, C)
    const2 = lambda b, c: (0, 0)
    const3 = lambda b, c: (0, 0, 0)
    rnn, hlast, cnew = pl.pallas_call(
        functools.partial(_rglru_kernel, L=L),
        grid=(B, nc),
        in_specs=[pl.BlockSpec((L, C), lambda b, c: (b * nc + c, xr_col // C)),
                  pl.BlockSpec((L, C), lambda b, c: (b * nc + c, yg_col // C)),
                  pl.BlockSpec((None, 1, C), lambda b, c: (b, 0, 0)),
                  pl.BlockSpec((None, CONV_W - 1, C), lambda b, c: (b, 0, 0)),
                  pl.BlockSpec((CONV_W, C), const2),
                  pl.BlockSpec((1, C), const2),
                  pl.BlockSpec((RNN_BLOCKS, C // RNN_BLOCKS, C // RNN_BLOCKS), const3),
                  pl.BlockSpec((1, C), const2),
                  pl.BlockSpec((RNN_BLOCKS, C // RNN_BLOCKS, C // RNN_BLOCKS), const3),
                  pl.BlockSpec((1, C), const2),
                  pl.BlockSpec((1, C), const2)],
        out_specs=[pl.BlockSpec((L, C), lambda b, c: (b * nc + c, 0)),
                   pl.BlockSpec((None, 1, C), lambda b, c: (b, 0, 0)),
                   pl.BlockSpec((None, CONV_W - 1, C), lambda b, c: (b, 0, 0))],
        out_shape=[jax.ShapeDtypeStruct((B * S, C), out_dtype),
                   jax.ShapeDtypeStruct((B, 1, C), F32),
                   jax.ShapeDtypeStruct((B, CONV_W - 1, C), F32)],
        scratch_shapes=[pltpu.VMEM((SUBLANES, C), F32),
                        pltpu.VMEM((1, C), F32)],
        compiler_params=_params("arbitrary", "arbitrary"),
        name="rglru",
    )(xr_arr, yg_arr, h0.reshape(B, 1, C), conv_buf, conv_w, row(conv_b), wa, row(ba), wx, row(bx), row(lam))
    return rnn, hlast.reshape(B, C), cnew


def _merge_kernel(rnn_ref, attn_ref, u_ref, wa_ref, wb_ref, wga_ref, wgb_ref, bga_ref, bgb_ref, o_ref):
    u = u_ref[...]
    a = jnp.dot(rnn_ref[...].astype(BF16), wa_ref[...], preferred_element_type=F32)
    b = jnp.dot(attn_ref[...].astype(BF16), wb_ref[...], preferred_element_type=F32)
    ga = jax.nn.sigmoid(jnp.dot(u, wga_ref[...], preferred_element_type=F32) + bga_ref[...])
    gb = jax.nn.sigmoid(jnp.dot(u, wgb_ref[...], preferred_element_type=F32) + bgb_ref[...])
    o_ref[...] = (ga * a + gb * b).astype(o_ref.dtype)


def _merge(rnn, attn, u, wa, wb, w_in, gate_col, b_gate):
    T, D = u.shape
    tm = _row_tile(T, 1024)
    tn = COL_CHUNK
    nb = D // tn
    assert gate_col % tn == 0
    act = lambda m, n: (m, 0)
    col = lambda m, n: (0, n)
    col_hi = lambda m, n: (0, n + nb)
    gcol = lambda m, n: (0, gate_col // tn + n)
    gcol_hi = lambda m, n: (0, gate_col // tn + nb + n)
    bg = b_gate.reshape(1, 2 * D)
    return pl.pallas_call(
        _merge_kernel,
        grid=(T // tm, nb),
        in_specs=[pl.BlockSpec((tm, D), act), pl.BlockSpec((tm, D), act), pl.BlockSpec((tm, D), act),
                  pl.BlockSpec((D, tn), col), pl.BlockSpec((D, tn), col),
                  pl.BlockSpec((D, tn), gcol), pl.BlockSpec((D, tn), gcol_hi),
                  pl.BlockSpec((1, tn), col), pl.BlockSpec((1, tn), col_hi)],
        out_specs=pl.BlockSpec((tm, tn), lambda m, n: (m, n)),
        out_shape=jax.ShapeDtypeStruct((T, D), BF16),
        compiler_params=_params("arbitrary", "arbitrary"),
        name="merge",
    )(rnn, attn, u, wa, wb, w_in, w_in, bg, bg)


def _outproj_kernel(mg_ref, x_ref, w_ref, g_ref, x2_ref, u2_ref):
    mg = mg_ref[...]
    D = x_ref.shape[1]
    ssq = jnp.zeros((x_ref.shape[0], 1), F32)
    for c in range(D // COL_CHUNK):
        sl = slice(c * COL_CHUNK, (c + 1) * COL_CHUNK)
        y = x_ref[:, sl] + jnp.dot(mg, w_ref[:, sl], preferred_element_type=F32)
        x2_ref[:, sl] = y
        ssq = ssq + jnp.sum(y * y, axis=-1, keepdims=True)
    inv = lax.rsqrt(ssq / D + RMS_EPS)
    u2_ref[...] = (x2_ref[...] * inv * g_ref[...]).astype(u2_ref.dtype)


def _outproj(merged, x, w_out, g):
    T, D = x.shape
    tm = _row_tile(T, 512)
    row = lambda m: (m, 0)
    return pl.pallas_call(
        _outproj_kernel,
        grid=(T // tm,),
        in_specs=[pl.BlockSpec((tm, D), row), pl.BlockSpec((tm, D), row),
                  pl.BlockSpec((D, D), lambda m: (0, 0)), pl.BlockSpec((1, D), lambda m: (0, 0))],
        out_specs=[pl.BlockSpec((tm, D), row), pl.BlockSpec((tm, D), row)],
        out_shape=[jax.ShapeDtypeStruct((T, D), F32), jax.ShapeDtypeStruct((T, D), BF16)],
        compiler_params=_params("arbitrary"),
        name="out_proj",
    )(merged, x, w_out, g.reshape(1, D))


def _ffn_kernel(u2_ref, x2_ref, w1_ref, w2_ref, gf_ref, y_ref, *, nf, final_norm):
    f = pl.program_id(1)

    @pl.when(f == 0)
    def _():
        y_ref[...] = x2_ref[...]

    h = jnp.dot(u2_ref[...], w1_ref[...], preferred_element_type=F32)
    h = jnp.square(jnp.maximum(h, 0.0)).astype(BF16)
    for c in range(y_ref.shape[1] // COL_CHUNK):
        sl = slice(c * COL_CHUNK, (c + 1) * COL_CHUNK)
        y_ref[:, sl] += jnp.dot(h, w2_ref[:, sl], preferred_element_type=F32)

    if final_norm:
        @pl.when(f == nf - 1)
        def _():
            x3 = y_ref[...]
            ms = jnp.mean(x3 * x3, axis=-1, keepdims=True)
            y_ref[...] = x3 * lax.rsqrt(ms + RMS_EPS) * gf_ref[...]


def _ffn(u2, x2, w1, w2, g_final, final_norm):
    T, D = x2.shape
    F = w1.shape[1]
    tm = _row_tile(T, 1024)
    tf = 512 if tm == 1024 else 1024
    nf = F // tf
    row = lambda m, f: (m, 0)
    return pl.pallas_call(
        functools.partial(_ffn_kernel, nf=nf, final_norm=final_norm),
        grid=(T // tm, nf),
        in_specs=[pl.BlockSpec((tm, D), row), pl.BlockSpec((tm, D), row),
                  pl.BlockSpec((D, tf), lambda m, f: (0, f)), pl.BlockSpec((tf, D), lambda m, f: (f, 0)),
                  pl.BlockSpec((1, D), lambda m, f: (0, 0))],
        out_specs=pl.BlockSpec((tm, D), row),
        out_shape=jax.ShapeDtypeStruct((T, D), F32),
        compiler_params=_params("arbitrary", "arbitrary"),
        name="ffn",
    )(u2, x2, w1, w2, g_final.reshape(1, D))


def _rope_tables(pos):
    half = HEAD_DIM // 2
    inv = ROPE_THETA ** (-jnp.arange(half, dtype=F32) * (2.0 / HEAD_DIM))
    ang = pos.astype(F32)[:, None] * inv[None, :]
    cos, sin = jnp.cos(ang), jnp.sin(ang)
    return jnp.concatenate([cos, cos], axis=-1), jnp.concatenate([-sin, sin], axis=-1)


def _decoder_layer(x, B, S, cos, sin, past, h0, conv_buf, w, final_norm):
    C = h0.shape[-1]
    attn_w = N_HEADS * HEAD_DIM
    act_dtype = BF16 if S % 16 == 0 else F32

    kv_w = N_KV_HEADS * HEAD_DIM
    q_col, k_col, v_col = 0, attn_w, attn_w + kv_w
    xr_col = v_col + kv_w
    yg_col = xr_col + C
    gate_col = yg_col + C
    w_in = w["w_in"]

    u, xr = _norm_proj(x, w["norm_mix"], w_in, xr_col, C, F32)
    ygq = _proj(u, w_in, [(yg_col, C), (q_col, attn_w)], cos, sin,
                (C, C + attn_w), act_dtype)
    k4, v4 = _kv_proj(u, w_in, k_col, v_col, cos, sin)
    if past is None:
        attn = _attn_prompt(ygq, C, k4, v4, B, S)
    else:
        attn = _attn_decode(ygq, C, k4, v4, past[0], past[1], past[2], past[3], B, S)
    rnn, h_last, conv_new = _rglru(xr, 0, ygq, 0, B, S, h0, conv_buf, w["conv_w"], w["conv_b"],
                                   w["w_rg_a"], w["b_rg_a"], w["w_rg_x"], w["b_rg_x"], w["lru_lambda"])
    merged = _merge(rnn, attn, u, w["w_proj_a"], w["w_proj_b"], w_in, gate_col, w["b_gate"])
    x2, u2 = _outproj(merged, x, w["w_out"], w["norm_mlp"])
    y = _ffn(u2, x2, w["w_ff1"], w["w_ff2"], w["norm_final"], final_norm)
    k = k4.reshape(B, S, N_KV_HEADS, HEAD_DIM)
    v = v4.reshape(B, S, N_KV_HEADS, HEAD_DIM)
    return y, k, v, h_last, conv_new


def kernel(x_prompt, x_sample, cache_k, cache_v, state_h, state_conv, page_table, norm_mix, w_in, b_gate, conv_w, conv_b, w_rg_a, b_rg_a, w_rg_x, b_rg_x, lru_lambda, w_proj_a, w_proj_b, w_out, norm_mlp, w_ff1, w_ff2, norm_final):
    B, S, D = x_prompt.shape
    DB, DS, _ = x_sample.shape
    depth = w_in.shape[0]
    C = state_h.shape[-1]
    past_len = page_table.shape[1] * cache_k.shape[2]

    cos_p, sin_p = _rope_tables(jnp.arange(S, dtype=jnp.int32))
    cos_s, sin_s = _rope_tables(past_len + jnp.arange(DS, dtype=jnp.int32))
    cos_s, sin_s = jnp.tile(cos_s, (DB, 1)), jnp.tile(sin_s, (DB, 1))

    hp = x_prompt.reshape(B * S, D)
    hs = x_sample.reshape(DB * DS, D)
    outs = [[] for _ in range(8)]
    for l in range(depth):
        wl = w_in[l]
        w = dict(
            norm_mix=norm_mix[l], norm_mlp=norm_mlp[l], norm_final=norm_final,
            w_in=wl.astype(BF16), b_gate=b_gate[l],
            conv_w=conv_w[l], conv_b=conv_b[l],
            w_rg_a=w_rg_a[l].astype(BF16), b_rg_a=b_rg_a[l],
            w_rg_x=w_rg_x[l].astype(BF16), b_rg_x=b_rg_x[l], lru_lambda=lru_lambda[l],
            w_proj_a=w_proj_a[l].astype(BF16), w_proj_b=w_proj_b[l].astype(BF16),
            w_out=w_out[l].astype(BF16), w_ff1=w_ff1[l].astype(BF16), w_ff2=w_ff2[l].astype(BF16))
        final = l == depth - 1
        h0_p = jnp.zeros((B, C), F32)
        buf_p = jnp.zeros((B, CONV_W - 1, C), F32)
        hp, k1, v1, r1, c1 = _decoder_layer(hp, B, S, cos_p, sin_p, None, h0_p, buf_p, w, final)
        past = (cache_k, cache_v, l, page_table)
        hs, k2, v2, r2, c2 = _decoder_layer(hs, DB, DS, cos_s, sin_s, past, state_h[l], state_conv[l], w, final)
        for lst, val in zip(outs, (k1, v1, r1, c1, k2, v2, r2, c2)):
            lst.append(val)
    return (hp.reshape(B, S, D), hs.reshape(DB, DS, D), *[jnp.stack(o) for o in outs])
```
